```python
import math
import jax, jax.numpy as jnp
from jax import lax
import numpy as np

D_MODEL = 1024
BATCH = 4
SEQ = 4096
DEPTH = 2

MEM_LEN = 256
EPS = 1e-6
NEG = -1e30
BLK = 128

DIL_HEADS = 8
DIL_HEAD_DIM = 64
DIL_PATTERNS = ((128, 1), (512, 4), (2048, 16))
RET_HEADS = 4
RET_KEY_DIM = 64
RET_VAL_DIM = 128
ROPE_BASE = 10000.0
DIFF_HEADS = 4
DIFF_HEAD_DIM = 64
SSM_D_INNER = 512
SSM_HEAD_DIM = 64
SSM_HEADS = SSM_D_INNER // SSM_HEAD_DIM
SSM_GROUPS = 2
SSM_STATE = 128
SSM_CONV = 4
N_BRANCH = 4
BRANCH_WIDTH = 512
X_HEADS = 4
X_HEAD_DIM = 128
FFN_HIDDEN = -(-8 * D_MODEL // (3 * 256)) * 256

A_QKV = DIL_HEADS * DIL_HEAD_DIM
B_QK = RET_HEADS * RET_KEY_DIM
B_VG = RET_HEADS * RET_VAL_DIM
C_QK = DIFF_HEADS * 2 * DIFF_HEAD_DIM
C_V = DIFF_HEADS * 2 * DIFF_HEAD_DIM
D_XBC = SSM_D_INNER + 2 * SSM_GROUPS * SSM_STATE
IN_SPLITS = (A_QKV, A_QKV, A_QKV, B_QK, B_QK, B_VG, B_VG, C_QK, C_QK, C_V,
             SSM_D_INNER, D_XBC, SSM_HEADS, N_BRANCH * D_MODEL)
IN_WIDTH = sum(IN_SPLITS)

kernel_name = 'hybrid_gated_parallel_mixer_trunk'


def rmsnorm(x, g):
    xf = x.astype(jnp.float32)
    y = xf * lax.rsqrt(jnp.mean(xf * xf, axis=-1, keepdims=True) + EPS)
    return (y * g.astype(jnp.float32)).astype(x.dtype)


def split_cols(h, sizes):
    offs = np.cumsum(sizes)[:-1].tolist()
    return jnp.split(h, offs, axis=-1)


def band_attention(q, k, v, w):
    *lead, L, hd = q.shape
    nb = L // BLK
    qb = q.reshape(*lead, nb, BLK, hd)

    def windows(t):
        tb = t.reshape(*lead, nb, BLK, t.shape[-1])
        tp = jnp.pad(tb, [(0, 0)] * len(lead) + [(1, 0), (0, 0), (0, 0)])
        return jnp.concatenate([tp[..., :-1, :, :], tp[..., 1:, :, :]], axis=-2)

    kw, vw = windows(k), windows(v)
    s = jnp.einsum('...nqd,...nkd->...nqk', qb, kw).astype(jnp.float32) * (hd ** -0.5)
    blk = jnp.arange(nb)[:, None, None]
    qpos = blk * BLK + jnp.arange(BLK)[None, :, None]
    kpos = (blk - 1) * BLK + jnp.arange(2 * BLK)[None, None, :]
    dist = qpos - kpos
    mask = (dist >= 0) & (dist <= w) & (kpos >= 0)
    s = jnp.where(mask, s, NEG)
    m = jnp.max(s, axis=-1, keepdims=True)
    p = jnp.exp(s - m)
    den = jnp.sum(p, axis=-1, keepdims=True)
    o = jnp.einsum('...nqk,...nkd->...nqd', (p / den).astype(v.dtype), vw)
    lse = (m + jnp.log(den))[..., 0]
    return o.reshape(*lead, L, hd), lse.reshape(*lead, L)


def dilated_attention(q, k, v):
    Bsz, S, H, hd = q.shape
    outs, lses = [], []
    for window, dil in DIL_PATTERNS:
        L = S // dil
        Lp = -(-L // BLK) * BLK

        def strided(t):
            t = t.reshape(Bsz, L, dil, H, hd).transpose(0, 3, 2, 1, 4)
            return jnp.pad(t, ((0, 0), (0, 0), (0, 0), (0, Lp - L), (0, 0)))

        o, lse = band_attention(strided(q), strided(k), strided(v), window // dil)
        outs.append(o[..., :L, :].transpose(0, 3, 2, 1, 4).reshape(Bsz, S, H, hd))
        lses.append(lse[..., :L].transpose(0, 3, 2, 1).reshape(Bsz, S, H))
    wts = jax.nn.softmax(jnp.stack(lses, axis=0), axis=0)
    o = jnp.sum(wts[..., None] * jnp.stack(outs, axis=0).astype(jnp.float32), axis=0)
    return o.reshape(Bsz, S, H * hd).astype(q.dtype)


def rotary(t, pos):
    half = t.shape[-1] // 2
    inv_freq = ROPE_BASE ** (-jnp.arange(half, dtype=jnp.float32) / half)
    ang = pos[:, None] * inv_freq[None, :]
    cos, sin = jnp.cos(ang)[:, None, :], jnp.sin(ang)[:, None, :]
    t1, t2 = t[..., :half], t[..., half:]
    return jnp.concatenate([t1 * cos - t2 * sin, t1 * sin + t2 * cos], axis=-1)


def retention(q, k, v, g, gn_g):
    Bsz, S, H, dk = q.shape
    dv = RET_VAL_DIM
    f32 = jnp.float32
    pos = jnp.arange(S, dtype=f32)
    q = rotary(q.astype(f32), pos)
    k = rotary(k.astype(f32), pos) * (dk ** -0.5)
    nc = S // BLK
    qc = q.reshape(Bsz, nc, BLK, H, dk)
    kc = k.reshape(Bsz, nc, BLK, H, dk)
    vc = v.astype(f32).reshape(Bsz, nc, BLK, H, dv)
    log_gamma = jnp.log1p(-jnp.exp2(-5.0 - jnp.arange(H, dtype=f32)))
    idx = jnp.arange(BLK, dtype=f32)
    rel = idx[:, None] - idx[None, :]
    decay = jnp.where(rel >= 0, jnp.exp(log_gamma[:, None, None] * jnp.maximum(rel, 0.0)), 0.0)
    s = jnp.einsum('bnihd,bnjhd->bnhij', qc, kc) * decay
    o_intra = jnp.einsum('bnhij,bnjhe->bnihe', s, vc)
    k_end = kc * jnp.exp((BLK - 1 - idx)[:, None] * log_gamma[None, :])[..., None]
    kv = jnp.einsum('bnjhd,bnjhe->nbhde', k_end, vc)
    chunk_decay = jnp.exp(log_gamma * BLK)[:, None, None]

    def step(state, kv_n):
        return state * chunk_decay + kv_n, state

    _, r_prev = lax.scan(step, jnp.zeros((Bsz, H, dk, dv), f32), kv)
    q_in = qc * jnp.exp((idx + 1.0)[:, None] * log_gamma[None, :])[..., None]
    o_cross = jnp.einsum('bnihd,nbhde->bnihe', q_in, r_prev)
    o = (o_intra + o_cross).reshape(Bsz, S, H, dv)
    mu = jnp.mean(o, axis=-1, keepdims=True)
    var = jnp.mean(jnp.square(o - mu), axis=-1, keepdims=True)
    o = ((o - mu) * lax.rsqrt(var + EPS)).reshape(Bsz, S, H * dv) * gn_g.astype(f32)
    return (jax.nn.silu(g.astype(f32)) * o).astype(g.dtype)


def diff_attention(q, k, v, lam_params, subln_g, lam_init):
    Bsz, S, H, _, d = q.shape
    lp = lam_params.astype(jnp.float32)
    lam = jnp.exp(jnp.sum(lp[0] * lp[1])) - jnp.exp(jnp.sum(lp[2] * lp[3])) + lam_init
    nb = S // BLK
    qb = q.reshape(Bsz, nb, BLK, H, 2, d).transpose(1, 0, 3, 4, 2, 5)
    kt = k.transpose(0, 2, 3, 1, 4)
    vt = v.transpose(0, 2, 1, 3)
    kpos = jnp.arange(S)

    def block(args):
        qi, i = args
        s = jnp.einsum('bhcqd,bhckd->bhcqk', qi, kt).astype(jnp.float32) * (d ** -0.5)
        qpos = i * BLK + jnp.arange(BLK)
        s = jnp.where(kpos[None, :] <= qpos[:, None], s, NEG)
        a = jax.nn.softmax(s, axis=-1)
        attn = a[:, :, 0] - lam * a[:, :, 1]
        return jnp.einsum('bhqk,bhke->bhqe', attn.astype(vt.dtype), vt)

    o = lax.map(block, (qb, jnp.arange(nb)))
    o = o.transpose(1, 0, 3, 2, 4).reshape(Bsz, S, H, 2 * d)
    o = rmsnorm(o, subln_g) * (1.0 - lam_init)
    return o.reshape(Bsz, S, H * 2 * d)


def mamba2_ssd(z, xbc, dt, conv_w, conv_b, dt_bias, A_log, D_skip, norm_g):
    Bsz, S, C = xbc.shape
    f32 = jnp.float32
    G, Hg, P, N = SSM_GROUPS, SSM_HEADS // SSM_GROUPS, SSM_HEAD_DIM, SSM_STATE
    xbc = lax.conv_general_dilated(xbc, conv_w[:, None, :], window_strides=(1,),
                                   padding=[(SSM_CONV - 1, 0)],
                                   dimension_numbers=('NWC', 'WIO', 'NWC'),
                                   feature_group_count=C) + conv_b
    xbc = jax.nn.silu(xbc.astype(f32))
    xs, Bm, Cm = split_cols(xbc, (SSM_D_INNER, G * N, G * N))
    dt = jax.nn.softplus(dt.astype(f32) + dt_bias.astype(f32))
    A = -jnp.exp(A_log.astype(f32)).reshape(G, Hg)
    nc = S // BLK
    xc = xs.reshape(Bsz, nc, BLK, G, Hg, P)
    Bc = Bm.reshape(Bsz, nc, BLK, G, N)
    Cc = Cm.reshape(Bsz, nc, BLK, G, N)
    dtc = dt.reshape(Bsz, nc, BLK, G, Hg)
    a_cum = jnp.cumsum(dtc * A, axis=2)
    seg = a_cum[:, :, :, None] - a_cum[:, :, None, :]
    causal = jnp.tril(jnp.ones((BLK, BLK), dtype=bool))[:, :, None, None]
    Lmat = jnp.exp(jnp.where(causal, seg, NEG))
    cb = jnp.einsum('bcign,bcjgn->bcijg', Cc, Bc)
    xdt = xc * dtc[..., None]
    y_diag = jnp.einsum('bcijg,bcijgh,bcjghp->bcighp', cb, Lmat, xdt)
    decay_end = jnp.exp(a_cum[:, :, -1:] - a_cum)
    states = jnp.einsum('bcjgn,bcjgh,bcjghp->cbghpn', Bc, decay_end, xdt)
    chunk_decay = jnp.exp(a_cum[:, :, -1]).transpose(1, 0, 2, 3)

    def step(h, inp):
        st, dec = inp
        return h * dec[..., None, None] + st, h

    _, h_prev = lax.scan(step, jnp.zeros((Bsz, G, Hg, P, N), f32), (states, chunk_decay))
    y_off = jnp.einsum('bcign,cbghpn,bcigh->bcighp', Cc, h_prev, jnp.exp(a_cum))
    y = y_diag + y_off + xc * D_skip.astype(f32).reshape(G, Hg)[:, :, None]
    y = y.reshape(Bsz, S, SSM_D_INNER)
    y = rmsnorm(y * jax.nn.silu(z.astype(f32)), norm_g)
    return y.astype(z.dtype)


def hybrid_mixer(h, w_in, ret_gn_g, diff_lambda, diff_subln_g, conv_w, conv_b, dt_bias,
                 A_log, D_skip, ssm_norm_g, w_branch, w_out, lam_init):
    Bsz, S, _ = h.shape
    proj = h @ w_in
    (aq, ak, av, bq, bk, bv, bg, cq, ck, cv, dz, dxbc, ddt, gates) = split_cols(proj, IN_SPLITS)
    ash = (Bsz, S, DIL_HEADS, DIL_HEAD_DIM)
    y_a = dilated_attention(aq.reshape(ash), ak.reshape(ash), av.reshape(ash))
    bsh = (Bsz, S, RET_HEADS, RET_KEY_DIM)
    y_b = retention(bq.reshape(bsh), bk.reshape(bsh), bv, bg, ret_gn_g)
    csh = (Bsz, S, DIFF_HEADS, 2, DIFF_HEAD_DIM)
    y_c = diff_attention(cq.reshape(csh), ck.reshape(csh),
                         cv.reshape(Bsz, S, DIFF_HEADS, 2 * DIFF_HEAD_DIM),
                         diff_lambda, diff_subln_g, lam_init)
    y_d = mamba2_ssd(dz, dxbc, ddt, conv_w, conv_b, dt_bias, A_log, D_skip, ssm_norm_g)
    branches = jnp.stack([y_a, y_b, y_c, y_d], axis=2)
    up = jnp.einsum('bsim,imd->bsid', branches, w_branch)
    gate = jax.nn.sigmoid(gates.reshape(Bsz, S, N_BRANCH, D_MODEL))
    return jnp.sum(gate * up, axis=2) @ w_out


def cross_attention(h, m, w_q, w_kv, w_o):
    Bsz, S, _ = h.shape
    M = m.shape[1]
    q = (h @ w_q).reshape(Bsz, S, X_HEADS, X_HEAD_DIM)
    kv = (m @ w_kv).reshape(Bsz, M, 2, X_HEADS, X_HEAD_DIM)
    k, v = kv[:, :, 0], kv[:, :, 1]
    s = jnp.einsum('bshd,bmhd->bhsm', q, k).astype(jnp.float32) * (X_HEAD_DIM ** -0.5)
    a = jax.nn.softmax(s, axis=-1)
    o = jnp.einsum('bhsm,bmhd->bshd', a.astype(v.dtype), v).reshape(Bsz, S, X_HEADS * X_HEAD_DIM)
    return o @ w_o


def swiglu(h, w_in, w_out):
    a, b = jnp.split(h @ w_in, 2, axis=-1)
    return (jax.nn.silu(a) * b) @ w_out


def setup_inputs(seed: int = 0) -> dict:
    key = jax.random.key(seed)
    ks = jax.random.split(key, 24)
    nrm = jax.random.normal
    L = DEPTH

    def gain(k, n):
        return 1.0 + 0.05 * nrm(k, (L, n), jnp.float32)

    u = jax.random.uniform(ks[9], (L, SSM_HEADS), jnp.float32)
    dt0 = jnp.exp(u * (math.log(0.1) - math.log(0.001)) + math.log(0.001))
    return {
        'x': nrm(ks[0], (BATCH, SEQ, D_MODEL), jnp.float32),
        'mem': nrm(ks[1], (BATCH, MEM_LEN, D_MODEL), jnp.float32),
        'norm_mix_g': gain(ks[2], D_MODEL),
        'w_in': nrm(ks[3], (L, D_MODEL, IN_WIDTH), jnp.float32) * D_MODEL ** -0.5,
        'ret_gn_g': gain(ks[4], RET_HEADS * RET_VAL_DIM),
        'diff_lambda': 0.1 * nrm(ks[5], (L, 4, DIFF_HEAD_DIM), jnp.float32),
        'diff_subln_g': gain(ks[6], 2 * DIFF_HEAD_DIM),
        'ssm_conv_w': nrm(ks[7], (L, SSM_CONV, D_XBC), jnp.float32) * SSM_CONV ** -0.5,
        'ssm_conv_b': 0.02 * nrm(ks[8], (L, D_XBC), jnp.float32),
        'ssm_dt_bias': dt0 + jnp.log(-jnp.expm1(-dt0)),
        'ssm_A_log': jnp.log(jax.random.uniform(ks[10], (L, SSM_HEADS), jnp.float32, 1.0, 16.0)),
        'ssm_D': 1.0 + 0.1 * nrm(ks[11], (L, SSM_HEADS), jnp.float32),
        'ssm_norm_g': gain(ks[12], SSM_D_INNER),
        'w_branch': nrm(ks[13], (L, N_BRANCH, BRANCH_WIDTH, D_MODEL), jnp.float32) * BRANCH_WIDTH ** -0.5,
        'w_mix_out': nrm(ks[14], (L, D_MODEL, D_MODEL), jnp.float32) * D_MODEL ** -0.5,
        'norm_x_g': gain(ks[15], D_MODEL),
        'norm_mem_g': gain(ks[16], D_MODEL),
        'w_xq': nrm(ks[17], (L, D_MODEL, X_HEADS * X_HEAD_DIM), jnp.float32) * D_MODEL ** -0.5,
        'w_xkv': nrm(ks[18], (L, D_MODEL, 2 * X_HEADS * X_HEAD_DIM), jnp.float32) * D_MODEL ** -0.5,
        'w_xo': nrm(ks[19], (L, X_HEADS * X_HEAD_DIM, D_MODEL), jnp.float32) * (X_HEADS * X_HEAD_DIM) ** -0.5,
        'norm_ffn_g': gain(ks[20], D_MODEL),
        'w_ffn_in': nrm(ks[21], (L, D_MODEL, 2 * FFN_HIDDEN), jnp.float32) * D_MODEL ** -0.5,
        'w_ffn_out': nrm(ks[22], (L, FFN_HIDDEN, D_MODEL), jnp.float32) * FFN_HIDDEN ** -0.5,
        'norm_f_g': 1.0 + 0.05 * nrm(ks[23], (D_MODEL,), jnp.float32),
    }


def reference(x, mem, norm_mix_g, w_in, ret_gn_g, diff_lambda, diff_subln_g, ssm_conv_w,
              ssm_conv_b, ssm_dt_bias, ssm_A_log, ssm_D, ssm_norm_g, w_branch, w_mix_out,
              norm_x_g, norm_mem_g, w_xq, w_xkv, w_xo, norm_ffn_g, w_ffn_in, w_ffn_out,
              norm_f_g):
    for l in range(DEPTH):
        lam_init = 0.8 - 0.6 * math.exp(-0.3 * l)
        h = rmsnorm(x, norm_mix_g[l])
        x = x + hybrid_mixer(h, w_in[l], ret_gn_g[l], diff_lambda[l], diff_subln_g[l],
                             ssm_conv_w[l], ssm_conv_b[l], ssm_dt_bias[l], ssm_A_log[l],
                             ssm_D[l], ssm_norm_g[l], w_branch[l], w_mix_out[l], lam_init)
        m = rmsnorm(mem, norm_mem_g[l])
        x = x + cross_attention(rmsnorm(x, norm_x_g[l]), m, w_xq[l], w_xkv[l], w_xo[l])
        x = x + swiglu(rmsnorm(x, norm_ffn_g[l]), w_ffn_in[l], w_ffn_out[l])
    return rmsnorm(x, norm_f_g)
```

```python
import functools
import math

import jax
import jax.numpy as jnp
import numpy as np
from jax import lax
from jax.experimental import pallas as pl
from jax.experimental.pallas import tpu as pltpu

F32 = jnp.float32
BF16 = jnp.bfloat16

EPS = 1e-6
NEG = -1e30
BLK = 128

DIL_HEADS = 8
DIL_HEAD_DIM = 64
DIL_PATTERNS = ((128, 1), (512, 4), (2048, 16))
RET_HEADS = 4
RET_KEY_DIM = 64
RET_VAL_DIM = 128
ROPE_BASE = 10000.0
DIFF_HEADS = 4
DIFF_HEAD_DIM = 64
SSM_D_INNER = 512
SSM_HEAD_DIM = 64
SSM_HEADS = 8
SSM_GROUPS = 2
SSM_STATE = 128
SSM_CONV = 4
N_BRANCH = 4
BRANCH_WIDTH = 512
X_HEADS = 4
X_HEAD_DIM = 128
LANES = 128

MIB = 1024 * 1024


def _params(semantics, vmem_mib):
    return pltpu.CompilerParams(dimension_semantics=semantics,
                                vmem_limit_bytes=vmem_mib * MIB)


def _dot(a, b):
    return jnp.dot(a, b, preferred_element_type=F32)


def _dot_nt(a, b):
    return lax.dot_general(a, b, (((1,), (1,)), ((), ())), preferred_element_type=F32)


def _split3(x):
    x1 = x.astype(BF16)
    r1 = x - x1.astype(F32)
    x2 = r1.astype(BF16)
    x3 = (r1 - x2.astype(F32)).astype(BF16)
    return x1, x2, x3


def _dot_sel(x, sel):
    x1, x2, x3 = _split3(x)
    return _dot(x1, sel) + _dot(x2, sel) + _dot(x3, sel)


def _sel_dot(sel, x):
    x1, x2, x3 = _split3(x)
    return _dot(sel, x1) + _dot(sel, x2) + _dot(sel, x3)


def _sigmoid(x):
    return 1.0 / (1.0 + jnp.exp(-x))


def _silu(x):
    return x * _sigmoid(x)


def _rms(x, g):
    ms = jnp.mean(x * x, axis=-1, keepdims=True)
    return x * lax.rsqrt(ms + EPS) * g


def _norm_matmul_kernel(x_ref, g_ref, *refs, n_out, scales):
    w_refs, o_refs = refs[:n_out], refs[n_out:]
    h = _rms(x_ref[...], g_ref[...]).astype(BF16)
    for w_ref, o_ref, sc in zip(w_refs, o_refs, scales):
        y = _dot(h, w_ref[...])
        if sc != 1.0:
            y = y * sc
        o_ref[...] = y.astype(o_ref.dtype)


def _norm_matmul(x2d, g, ws, out_dtypes, scales, tm, vmem_mib=48):
    t, d = x2d.shape
    in_specs = [pl.BlockSpec((tm, d), lambda i: (i, 0)),
                pl.BlockSpec((1, d), lambda i: (0, 0))]
    in_specs += [pl.BlockSpec(w.shape, lambda i: (0, 0)) for w in ws]
    out_specs = [pl.BlockSpec((tm, w.shape[1]), lambda i: (i, 0)) for w in ws]
    out_shape = [jax.ShapeDtypeStruct((t, w.shape[1]), dt) for w, dt in zip(ws, out_dtypes)]
    return pl.pallas_call(
        functools.partial(_norm_matmul_kernel, n_out=len(ws), scales=tuple(scales)),
        grid=(t // tm,), in_specs=in_specs, out_specs=out_specs, out_shape=out_shape,
        compiler_params=_params(("parallel",), vmem_mib),
        name="norm_matmul",
    )(x2d, g.reshape(1, d), *ws)


def _band_attn_kernel(q_ref, kp_ref, kc_ref, vp_ref, vc_ref, o_ref, lse_ref, *, window):
    n = pl.program_id(2)
    q = q_ref[0]
    k2 = jnp.concatenate([kp_ref[0], kc_ref[0]], axis=0)
    v2 = jnp.concatenate([vp_ref[0], vc_ref[0]], axis=0)
    i = lax.broadcasted_iota(jnp.int32, (BLK, 2 * BLK), 0)
    j = lax.broadcasted_iota(jnp.int32, (BLK, 2 * BLK), 1)
    dist = BLK + i - j
    ok = jnp.where(dist >= 0, jnp.where(dist <= window, 1, 0), 0)
    ok = jnp.where(j >= BLK, ok, jnp.where(n > 0, ok, 0))
    mask = ok > 0
    outs, lses = [], []
    for h in range(DIL_HEADS):
        sl = slice(h * DIL_HEAD_DIM, (h + 1) * DIL_HEAD_DIM)
        s = jnp.where(mask, _dot_nt(q[:, sl], k2[:, sl]), NEG)
        m = jnp.max(s, axis=-1, keepdims=True)
        p = jnp.exp(s - m)
        den = jnp.sum(p, axis=-1, keepdims=True)
        outs.append(_dot((p / den).astype(BF16), v2[:, sl]))
        lses.append(jnp.broadcast_to(m + jnp.log(den), (BLK, DIL_HEAD_DIM)))
    o_ref[0] = jnp.concatenate(outs, axis=-1).astype(o_ref.dtype)
    lse_ref[0] = jnp.concatenate(lses, axis=-1)


def _band_attention(q, k, v, window, dil):
    b, s, w = q.shape
    l = s // dil
    nb = l // BLK
    view = lambda t: t.reshape(b, l, dil * w)
    cur = pl.BlockSpec((1, BLK, w), lambda bi, r, n: (bi, n, r))
    prev = pl.BlockSpec((1, BLK, w), lambda bi, r, n: (bi, jnp.maximum(n - 1, 0), r))
    o, lse = pl.pallas_call(
        functools.partial(_band_attn_kernel, window=window // dil),
        grid=(b, dil, nb),
        in_specs=[cur, prev, cur, prev, cur],
        out_specs=[cur, cur],
        out_shape=[jax.ShapeDtypeStruct((b, l, dil * w), BF16),
                   jax.ShapeDtypeStruct((b, l, dil * w), F32)],
        compiler_params=_params(("parallel", "parallel", "parallel"), 32),
        name="band_attention",
    )(view(q), view(k), view(k), view(v), view(v))
    return o.reshape(b, s, w), lse.reshape(b, s, w)


def _retention_kernel(q_ref, k_ref, v_ref, g_ref, cos_ref, sin_ref, dq_ref, dk_ref, dec_ref,
                      cdec_ref, gn_ref, o_ref, state_ref):
    @pl.when(pl.program_id(1) == 0)
    def _():
        state_ref[...] = jnp.zeros_like(state_ref)

    cos, sin = cos_ref[...], sin_ref[...]
    half = RET_HEADS * RET_KEY_DIM // 2

    def rot(t):
        t1, t2 = t[:, :half], t[:, half:]
        return jnp.concatenate([t1 * cos - t2 * sin, t1 * sin + t2 * cos], axis=-1)

    rq = rot(q_ref[0])
    rk = rot(k_ref[0]) * (RET_KEY_DIM ** -0.5)
    v = v_ref[0]
    rkb = rk.astype(BF16)
    lane_head = (lax.broadcasted_iota(jnp.int32, (1, 2 * half), 1) % half) // (RET_KEY_DIM // 2)
    state = state_ref[...]
    o_cross = _dot((rq * dq_ref[...]).astype(BF16), state.astype(BF16))
    outs = []
    for h in range(RET_HEADS):
        qh = jnp.where(lane_head == h, rq, 0.0).astype(BF16)
        s = _dot_nt(qh, rkb) * dec_ref[h]
        outs.append(_dot(s.astype(BF16), v[:, h * RET_VAL_DIM:(h + 1) * RET_VAL_DIM]))
    o = jnp.concatenate(outs, axis=-1) + o_cross
    k_end_t = (rk * dk_ref[...]).T.astype(BF16)
    kv = _dot(k_end_t, v)
    row_head = (lax.broadcasted_iota(jnp.int32, kv.shape, 0) % half) // (RET_KEY_DIM // 2)
    col_head = lax.broadcasted_iota(jnp.int32, kv.shape, 1) // RET_VAL_DIM
    state_ref[...] = state * cdec_ref[...] + jnp.where(row_head == col_head, kv, 0.0)
    normed = []
    for h in range(RET_HEADS):
        oh = o[:, h * RET_VAL_DIM:(h + 1) * RET_VAL_DIM]
        mu = jnp.mean(oh, axis=-1, keepdims=True)
        cen = oh - mu
        var = jnp.mean(cen * cen, axis=-1, keepdims=True)
        normed.append(cen * lax.rsqrt(var + EPS))
    y = jnp.concatenate(normed, axis=-1) * gn_ref[...]
    o_ref[0] = (_silu(g_ref[0].astype(F32)) * y).astype(o_ref.dtype)


def _retention_tables(s):
    h, dk, dv = RET_HEADS, RET_KEY_DIM, RET_VAL_DIM
    quarter = dk // 2
    pos = jnp.arange(s, dtype=F32)
    inv_freq = ROPE_BASE ** (-jnp.arange(quarter, dtype=F32) / quarter)
    ang = pos[:, None] * jnp.tile(inv_freq, h)[None, :]
    log_gamma = jnp.log1p(-jnp.exp2(-5.0 - jnp.arange(h, dtype=F32)))
    idx = jnp.arange(BLK, dtype=F32)
    rel = idx[:, None] - idx[None, :]
    decay = jnp.where(rel >= 0, jnp.exp(log_gamma[:, None, None] * jnp.maximum(rel, 0.0)), 0.0)
    lane_gamma = jnp.tile(jnp.repeat(log_gamma, quarter), 2)
    dq = jnp.exp((idx + 1.0)[:, None] * lane_gamma[None, :])
    dkt = jnp.exp((BLK - 1 - idx)[:, None] * lane_gamma[None, :])
    cdec = jnp.repeat(jnp.exp(log_gamma * BLK), dv)[None, :]
    return jnp.cos(ang), jnp.sin(ang), dq, dkt, decay, cdec


def _retention(q, k, v, g, gn_g):
    b, s, wk = q.shape
    wv = v.shape[-1]
    cos, sin, dq, dkt, decay, cdec = _retention_tables(s)
    tok = lambda w: pl.BlockSpec((1, BLK, w), lambda bi, c: (bi, c, 0))
    const = lambda shape: pl.BlockSpec(shape, lambda bi, c: (0,) * len(shape))
    return pl.pallas_call(
        _retention_kernel,
        grid=(b, s // BLK),
        in_specs=[tok(wk), tok(wk), tok(wv), tok(wv),
                  pl.BlockSpec((BLK, wk // 2), lambda bi, c: (c, 0)),
                  pl.BlockSpec((BLK, wk // 2), lambda bi, c: (c, 0)),
                  const((BLK, wk)), const((BLK, wk)), const((RET_HEADS, BLK, BLK)),
                  const((1, wv)), const((1, wv))],
        out_specs=tok(wv),
        out_shape=jax.ShapeDtypeStruct((b, s, wv), BF16),
        scratch_shapes=[pltpu.VMEM((wk, wv), F32)],
        compiler_params=_params(("parallel", "arbitrary"), 32),
        name="retention",
    )(q, k, v, g, cos, sin, dq, dkt, decay, cdec, gn_g.reshape(1, wv))


def _diff_attn_kernel(lam_ref, g_ref, q_ref, k_ref, v_ref, o_ref, *, tq, lam_init):
    d = DIFF_HEAD_DIM
    qi = pl.program_id(2)
    lp = lam_ref[...]
    lam = (jnp.exp(jnp.sum(lp[0:1] * lp[1:2], keepdims=True))
           - jnp.exp(jnp.sum(lp[2:3] * lp[3:4], keepdims=True)) + lam_init)
    q = q_ref[0]
    q1, q2 = q[:, :d], q[:, d:]

    def update(carry, s, v):
        m, l, acc = carry
        m_new = jnp.maximum(m, jnp.max(s, axis=-1, keepdims=True))
        alpha = jnp.exp(m - m_new)
        p = jnp.exp(s - m_new)
        l = alpha * l + jnp.sum(p, axis=-1, keepdims=True)
        acc = alpha * acc + _dot(p.astype(BF16), v)
        return m_new, l, acc

    def step(j, carry, diagonal):
        start = pl.multiple_of(j * tq, tq)
        k = k_ref[0, pl.ds(start, tq), :]
        v = v_ref[0, pl.ds(start, tq), :]
        s1 = _dot_nt(q1, k[:, :d])
        s2 = _dot_nt(q2, k[:, d:])
        if diagonal:
            row = lax.broadcasted_iota(jnp.int32, (tq, tq), 0)
            col = lax.broadcasted_iota(jnp.int32, (tq, tq), 1)
            s1 = jnp.where(col <= row, s1, NEG)
            s2 = jnp.where(col <= row, s2, NEG)
        return update(carry[0], s1, v), update(carry[1], s2, v)

    def init():
        return (jnp.full((tq, 1), NEG, F32), jnp.zeros((tq, 1), F32), jnp.zeros((tq, 2 * d), F32))

    carry = lax.fori_loop(0, qi, lambda j, c: step(j, c, False), (init(), init()))
    (_, l1, acc1), (_, l2, acc2) = step(qi, carry, True)
    o = acc1 / l1 - lam * (acc2 / l2)
    o_ref[0] = (_rms(o, g_ref[...]) * (1.0 - lam_init)).astype(o_ref.dtype)


def _diff_attention(q, k, v, lam_params, subln_g, lam_init, tq=256):
    b, s, w = q.shape
    hw = 2 * DIFF_HEAD_DIM
    return pl.pallas_call(
        functools.partial(_diff_attn_kernel, tq=tq, lam_init=lam_init),
        grid=(b, DIFF_HEADS, s // tq),
        in_specs=[pl.BlockSpec(lam_params.shape, lambda bi, h, i: (0, 0)),
                  pl.BlockSpec((1, hw), lambda bi, h, i: (0, 0)),
                  pl.BlockSpec((1, tq, hw), lambda bi, h, i: (bi, i, h)),
                  pl.BlockSpec((1, s, hw), lambda bi, h, i: (bi, 0, h)),
                  pl.BlockSpec((1, s, hw), lambda bi, h, i: (bi, 0, h))],
        out_specs=pl.BlockSpec((1, tq, hw), lambda bi, h, i: (bi, i, h)),
        out_shape=jax.ShapeDtypeStruct((b, s, w), BF16),
        compiler_params=_params(("parallel", "parallel", "arbitrary"), 32),
        name="diff_attention",
    )(lam_params, subln_g.reshape(1, hw), q, k, v)


def _ssd_kernel(z_ref, xbc_ref, dt_ref, cw_ref, cb_ref, dtb_ref, alog_ref, dskip_ref, ng_ref,
                tri_ref, exp_ref, o_ref, carry_ref, ext_ref, state_ref):
    gn = SSM_GROUPS * SSM_STATE
    hpg = SSM_HEADS // SSM_GROUPS
    gw = hpg * SSM_HEAD_DIM
    tail = 8

    @pl.when(pl.program_id(1) == 0)
    def _():
        carry_ref[...] = jnp.zeros_like(carry_ref)
        state_ref[...] = jnp.zeros_like(state_ref)

    xbc = xbc_ref[0]
    ext_ref[0:tail, :] = carry_ref[...]
    ext_ref[tail:tail + BLK, :] = xbc
    carry_ref[...] = xbc[BLK - tail:, :]
    cw = cw_ref[...]
    conv = cb_ref[...] + cw[SSM_CONV - 1:SSM_CONV, :] * xbc
    for sft in range(1, SSM_CONV):
        conv = conv + cw[SSM_CONV - 1 - sft:SSM_CONV - sft, :] * ext_ref[tail - sft:tail - sft + BLK, :]
    xc = _silu(conv)
    xs, bm, cm = xc[:, :SSM_D_INNER], xc[:, SSM_D_INNER:SSM_D_INNER + gn], xc[:, SSM_D_INNER + gn:]

    pre = dt_ref[0] + dtb_ref[...]
    dt = jnp.maximum(pre, 0.0) + jnp.log1p(jnp.exp(-jnp.abs(pre)))
    lane = lax.broadcasted_iota(jnp.int32, (1, LANES), 1)
    a_neg = jnp.where(lane < SSM_HEADS, -jnp.exp(alog_ref[...]), 0.0)
    expand = exp_ref[...]
    a_col = _sel_dot(tri_ref[...], dt * a_neg)
    a_row = a_col.T
    a_exp = _dot_sel(a_col, expand)
    dt_exp = _dot_sel(dt, expand)
    total = a_exp[BLK - 1:BLK, :]
    xdt = xs * dt_exp
    x_end = (xdt * jnp.exp(total - a_exp)).astype(BF16)
    xdt_b = xdt.astype(BF16)
    row = lax.broadcasted_iota(jnp.int32, (BLK, BLK), 0)
    col = lax.broadcasted_iota(jnp.int32, (BLK, BLK), 1)
    col_head = lax.broadcasted_iota(jnp.int32, (1, gw), 1) // SSM_HEAD_DIM
    ys = []
    for g in range(SSM_GROUPS):
        b_g = bm[:, g * SSM_STATE:(g + 1) * SSM_STATE]
        c_g = cm[:, g * SSM_STATE:(g + 1) * SSM_STATE].astype(BF16)
        cb = _dot_nt(c_g, b_g.astype(BF16))
        xg = xdt_b[:, g * gw:(g + 1) * gw]
        y_g = jnp.zeros((BLK, gw), F32)
        for hh in range(hpg):
            h = g * hpg + hh
            seg = a_col[:, h:h + 1] - a_row[h:h + 1, :]
            m_h = (cb * jnp.exp(jnp.where(col <= row, seg, NEG))).astype(BF16)
            y_g = y_g + _dot(m_h, jnp.where(col_head == hh, xg, jnp.zeros_like(xg)))
        state = state_ref[g]
        y_g = y_g + _dot(c_g, state.astype(BF16)) * jnp.exp(a_exp[:, g * gw:(g + 1) * gw])
        new = _dot(b_g.T.astype(BF16), x_end[:, g * gw:(g + 1) * gw])
        state_ref[g] = state * jnp.exp(total[:, g * gw:(g + 1) * gw]) + new
        ys.append(y_g)
    y = jnp.concatenate(ys, axis=-1) + xs * dskip_ref[...]
    y = y * _silu(z_ref[0].astype(F32))
    o_ref[0] = _rms(y, ng_ref[...]).astype(o_ref.dtype)


def _ssd(z, xbc, dt, conv_w, conv_b, dt_bias, a_log, d_skip, norm_g):
    b, s, c = xbc.shape
    di = SSM_D_INNER
    pad = lambda t: jnp.pad(t.astype(F32), (0, LANES - t.shape[0])).reshape(1, LANES)
    tri = jnp.tril(jnp.ones((BLK, BLK), BF16))
    expand = (jnp.arange(LANES)[:, None] == (jnp.arange(di) // SSM_HEAD_DIM)[None, :]).astype(BF16)
    tok = lambda w: pl.BlockSpec((1, BLK, w), lambda bi, ci: (bi, ci, 0))
    const = lambda shape: pl.BlockSpec(shape, lambda bi, ci: (0,) * len(shape))
    return pl.pallas_call(
        _ssd_kernel,
        grid=(b, s // BLK),
        in_specs=[tok(di), tok(c), tok(LANES), const((SSM_CONV, c)), const((1, c)),
                  const((1, LANES)), const((1, LANES)), const((1, di)), const((1, di)),
                  const((BLK, BLK)), const((LANES, di))],
        out_specs=tok(di),
        out_shape=jax.ShapeDtypeStruct((b, s, di), BF16),
        scratch_shapes=[pltpu.VMEM((8, c), F32), pltpu.VMEM((8 + BLK, c), F32),
                        pltpu.VMEM((SSM_GROUPS, SSM_STATE, di // SSM_GROUPS), F32)],
        compiler_params=_params(("parallel", "arbitrary"), 32),
        name="ssd",
    )(z, xbc, dt, conv_w, conv_b.reshape(1, c), pad(dt_bias), pad(a_log),
      jnp.repeat(d_skip.astype(F32), SSM_HEAD_DIM).reshape(1, di), norm_g.reshape(1, di), tri, expand)


def _merge_kernel(x_ref, o1_ref, o2_ref, o3_ref, l1_ref, l2_ref, l3_ref, yb_ref, yc_ref, yd_ref,
                  gate_ref, wb_ref, wo_ref, out_ref):
    d = x_ref.shape[-1]
    l1, l2, l3 = l1_ref[...], l2_ref[...], l3_ref[...]
    m = jnp.maximum(jnp.maximum(l1, l2), l3)
    e1, e2, e3 = jnp.exp(l1 - m), jnp.exp(l2 - m), jnp.exp(l3 - m)
    ya = (e1 * o1_ref[...].astype(F32) + e2 * o2_ref[...].astype(F32)
          + e3 * o3_ref[...].astype(F32)) / (e1 + e2 + e3)
    branches = (ya.astype(BF16), yb_ref[...], yc_ref[...], yd_ref[...])
    merged = jnp.zeros((x_ref.shape[0], d), F32)
    for i, y in enumerate(branches):
        gate = _sigmoid(gate_ref[:, i * d:(i + 1) * d].astype(F32))
        merged = merged + gate * _dot(y, wb_ref[i])
    out_ref[...] = x_ref[...] + _dot(merged.astype(BF16), wo_ref[...])


def _merge(x2d, o_parts, lse_parts, yb, yc, yd, gates, w_branch, w_out, tm=256):
    t, d = x2d.shape
    bw = yb.shape[-1]
    row = lambda w: pl.BlockSpec((tm, w), lambda i: (i, 0))
    return pl.pallas_call(
        _merge_kernel,
        grid=(t // tm,),
        in_specs=[row(d)] + [row(bw)] * 9 + [row(N_BRANCH * d),
                  pl.BlockSpec(w_branch.shape, lambda i: (0, 0, 0)),
                  pl.BlockSpec(w_out.shape, lambda i: (0, 0))],
        out_specs=row(d),
        out_shape=jax.ShapeDtypeStruct((t, d), F32),
        compiler_params=_params(("parallel",), 48),
        name="merge",
    )(x2d, *o_parts, *lse_parts, yb, yc, yd, gates, w_branch, w_out)


def _xattn_kernel(x_ref, g_ref, wq_ref, k_ref, v_ref, wo_ref, out_ref):
    x = x_ref[0]
    q = _dot(_rms(x, g_ref[...]).astype(BF16), wq_ref[...]).astype(BF16)
    k, v = k_ref[0], v_ref[0]
    outs = []
    for h in range(X_HEADS):
        sl = slice(h * X_HEAD_DIM, (h + 1) * X_HEAD_DIM)
        s = _dot_nt(q[:, sl], k[:, sl]) * (X_HEAD_DIM ** -0.5)
        p = jnp.exp(s - jnp.max(s, axis=-1, keepdims=True))
        a = p / jnp.sum(p, axis=-1, keepdims=True)
        outs.append(_dot(a.astype(BF16), v[:, sl]))
    o = jnp.concatenate(outs, axis=-1).astype(BF16)
    out_ref[0] = x + _dot(o, wo_ref[...])


def _cross_attention(x, g, w_q, k, v, w_o, tm=512):
    b, s, d = x.shape
    m, w = k.shape[1], k.shape[2]
    return pl.pallas_call(
        _xattn_kernel,
        grid=(b, s // tm),
        in_specs=[pl.BlockSpec((1, tm, d), lambda bi, i: (bi, i, 0)),
                  pl.BlockSpec((1, d), lambda bi, i: (0, 0)),
                  pl.BlockSpec(w_q.shape, lambda bi, i: (0, 0)),
                  pl.BlockSpec((1, m, w), lambda bi, i: (bi, 0, 0)),
                  pl.BlockSpec((1, m, w), lambda bi, i: (bi, 0, 0)),
                  pl.BlockSpec(w_o.shape, lambda bi, i: (0, 0))],
        out_specs=pl.BlockSpec((1, tm, d), lambda bi, i: (bi, i, 0)),
        out_shape=jax.ShapeDtypeStruct((b, s, d), F32),
        compiler_params=_params(("parallel", "parallel"), 48),
        name="cross_attention",
    )(x, g.reshape(1, d), w_q, k, v, w_o)


def _ffn_kernel(x_ref, g_ref, wa_ref, wb_ref, wo_ref, fg_ref, out_ref, *, final_norm):
    x = x_ref[...]
    h = _rms(x, g_ref[...]).astype(BF16)
    a = _dot(h, wa_ref[...])
    b = _dot(h, wb_ref[...])
    y = x + _dot((_silu(a) * b).astype(BF16), wo_ref[...])
    if final_norm:
        y = _rms(y, fg_ref[...])
    out_ref[...] = y


def _ffn(x2d, g, w_a, w_b, w_o, final_g, final_norm, tm=256):
    t, d = x2d.shape
    full = lambda w: pl.BlockSpec(w.shape, lambda i: (0, 0))
    return pl.pallas_call(
        functools.partial(_ffn_kernel, final_norm=final_norm),
        grid=(t // tm,),
        in_specs=[pl.BlockSpec((tm, d), lambda i: (i, 0)), pl.BlockSpec((1, d), lambda i: (0, 0)),
                  full(w_a), full(w_b), full(w_o), pl.BlockSpec((1, d), lambda i: (0, 0))],
        out_specs=pl.BlockSpec((tm, d), lambda i: (i, 0)),
        out_shape=jax.ShapeDtypeStruct((t, d), F32),
        compiler_params=_params(("parallel",), 56),
        name="ffn",
    )(x2d, g.reshape(1, d), w_a, w_b, w_o, final_g.reshape(1, d))


def _ret_feature_order():
    quarter = RET_KEY_DIM // 2
    first = [h * RET_KEY_DIM + f for h in range(RET_HEADS) for f in range(quarter)]
    return np.array(first + [c + quarter for c in first])


def _layer(x, mem, p, lam_init, final_g, final_norm):
    b, s, d = x.shape
    t = b * s
    x2d = x.reshape(t, d)
    a_w = DIL_HEADS * DIL_HEAD_DIM
    b_qk, b_vg = RET_HEADS * RET_KEY_DIM, RET_HEADS * RET_VAL_DIM
    c_w = DIFF_HEADS * 2 * DIFF_HEAD_DIM
    d_xbc = SSM_D_INNER + 2 * SSM_GROUPS * SSM_STATE
    sizes = (a_w, a_w, a_w, b_qk, b_qk, b_vg, b_vg, c_w, c_w, c_w, SSM_D_INNER, d_xbc, SSM_HEADS,
             N_BRANCH * d)
    offs = np.concatenate([[0], np.cumsum(sizes)])
    w_in = p['w_in']
    col = lambda i: w_in[:, offs[i]:offs[i + 1]]
    order = _ret_feature_order()
    w_dt = jnp.pad(col(12), ((0, 0), (0, LANES - SSM_HEADS)))
    bf = lambda w: w.astype(BF16)

    aq, ak, av, cq, ck, cv = _norm_matmul(
        x2d, p['norm_mix_g'], [bf(col(i)) for i in (0, 1, 2, 7, 8, 9)], [BF16] * 6,
        [DIL_HEAD_DIM ** -0.5, 1.0, 1.0, DIFF_HEAD_DIM ** -0.5, 1.0, 1.0], tm=512)
    bq, bk, bv, bg, dz, dxbc, ddt = _norm_matmul(
        x2d, p['norm_mix_g'],
        [bf(col(3)[:, order]), bf(col(4)[:, order]), bf(col(5)), bf(col(6)), bf(col(10)), bf(col(11)),
         bf(w_dt)],
        [F32, F32, BF16, BF16, BF16, F32, F32], [1.0] * 7, tm=512)
    (gates,) = _norm_matmul(x2d, p['norm_mix_g'], [bf(col(13))], [BF16], [1.0], tm=512)

    r3 = lambda y: y.reshape(b, s, y.shape[-1])
    o_parts, lse_parts = [], []
    for window, dil in DIL_PATTERNS:
        o_p, lse_p = _band_attention(r3(aq), r3(ak), r3(av), window, dil)
        o_parts.append(o_p.reshape(t, a_w))
        lse_parts.append(lse_p.reshape(t, a_w))
    y_b = _retention(r3(bq), r3(bk), r3(bv), r3(bg), p['ret_gn_g'])
    y_c = _diff_attention(r3(cq), r3(ck), r3(cv), p['diff_lambda'], p['diff_subln_g'], lam_init)
    y_d = _ssd(r3(dz), r3(dxbc), r3(ddt), p['ssm_conv_w'], p['ssm_conv_b'], p['ssm_dt_bias'],
               p['ssm_A_log'], p['ssm_D'], p['ssm_norm_g'])
    x2d = _merge(x2d, o_parts, lse_parts, y_b.reshape(t, -1), y_c.reshape(t, -1), y_d.reshape(t, -1),
                 gates, bf(p['w_branch']), bf(p['w_mix_out']))

    xw = X_HEADS * X_HEAD_DIM
    m2d = mem.reshape(-1, d)
    mk, mv = _norm_matmul(m2d, p['norm_mem_g'], [bf(p['w_xkv'][:, :xw]), bf(p['w_xkv'][:, xw:])],
                          [BF16, BF16], [1.0, 1.0], tm=256)
    mlen = mem.shape[1]
    x3 = _cross_attention(x2d.reshape(b, s, d), p['norm_x_g'], bf(p['w_xq']),
                          mk.reshape(b, mlen, xw), mv.reshape(b, mlen, xw), bf(p['w_xo']))
    hid = p['w_ffn_out'].shape[0]
    x2d = _ffn(x3.reshape(t, d), p['norm_ffn_g'], bf(p['w_ffn_in'][:, :hid]), bf(p['w_ffn_in'][:, hid:]),
               bf(p['w_ffn_out']), final_g, final_norm)
    return x2d.reshape(b, s, d)


def kernel(x, mem, norm_mix_g, w_in, ret_gn_g, diff_lambda, diff_subln_g, ssm_conv_w, ssm_conv_b,
           ssm_dt_bias, ssm_A_log, ssm_D, ssm_norm_g, w_branch, w_mix_out, norm_x_g, norm_mem_g, w_xq,
           w_xkv, w_xo, norm_ffn_g, w_ffn_in, w_ffn_out, norm_f_g):
    stacked = dict(norm_mix_g=norm_mix_g, w_in=w_in, ret_gn_g=ret_gn_g, diff_lambda=diff_lambda,
                   diff_subln_g=diff_subln_g, ssm_conv_w=ssm_conv_w, ssm_conv_b=ssm_conv_b,
                   ssm_dt_bias=ssm_dt_bias, ssm_A_log=ssm_A_log, ssm_D=ssm_D, ssm_norm_g=ssm_norm_g,
                   w_branch=w_branch, w_mix_out=w_mix_out, norm_x_g=norm_x_g, norm_mem_g=norm_mem_g,
                   w_xq=w_xq, w_xkv=w_xkv, w_xo=w_xo, norm_ffn_g=norm_ffn_g, w_ffn_in=w_ffn_in,
                   w_ffn_out=w_ffn_out)
    depth = w_in.shape[0]
    for l in range(depth):
        lam_init = 0.8 - 0.6 * math.exp(-0.3 * l)
        p = {name: val[l] for name, val in stacked.items()}
        x = _layer(x, mem, p, lam_init, norm_f_g, final_norm=(l == depth - 1))
    return x
```

```python
import functools
import math

import jax
import jax.numpy as jnp
import numpy as np
from jax import lax
from jax.experimental import pallas as pl
from jax.experimental.pallas import tpu as pltpu

F32 = jnp.float32
BF16 = jnp.bfloat16

EPS = 1e-6
NEG = -1e30
LOG2E = math.log2(math.e)
BLK = 128

DIL_HEADS = 8
DIL_HEAD_DIM = 64
DIL_PATTERNS = ((128, 1), (512, 4), (2048, 16))
RET_HEADS = 4
RET_KEY_DIM = 64
RET_VAL_DIM = 128
ROPE_BASE = 10000.0
DIFF_HEADS = 4
DIFF_HEAD_DIM = 64
SSM_D_INNER = 512
SSM_HEAD_DIM = 64
SSM_HEADS = 8
SSM_GROUPS = 2
SSM_STATE = 128
SSM_CONV = 4
N_BRANCH = 4
BRANCH_WIDTH = 512
X_HEADS = 4
X_HEAD_DIM = 128
LANES = 128

MIB = 1024 * 1024


def _params(semantics, vmem_mib):
    return pltpu.CompilerParams(dimension_semantics=semantics,
                                vmem_limit_bytes=vmem_mib * MIB)


def _dot(a, b):
    return jnp.dot(a, b, preferred_element_type=F32)


def _dot_nt(a, b):
    return lax.dot_general(a, b, (((1,), (1,)), ((), ())), preferred_element_type=F32)


def _split3(x):
    x1 = x.astype(BF16)
    r1 = x - x1.astype(F32)
    x2 = r1.astype(BF16)
    x3 = (r1 - x2.astype(F32)).astype(BF16)
    return x1, x2, x3


def _dot_sel(x, sel):
    x1, x2, x3 = _split3(x)
    return _dot(x1, sel) + _dot(x2, sel) + _dot(x3, sel)


def _sel_dot(sel, x):
    x1, x2, x3 = _split3(x)
    return _dot(sel, x1) + _dot(sel, x2) + _dot(sel, x3)


def _sigmoid(x):
    return 1.0 / (1.0 + jnp.exp(-x))


def _silu(x):
    return x * _sigmoid(x)


def _rms(x, g):
    ms = jnp.mean(x * x, axis=-1, keepdims=True)
    return x * lax.rsqrt(ms + EPS) * g


def _norm_matmul_kernel(x_ref, g_ref, *refs, n_out, scales):
    w_refs, o_refs = refs[:n_out], refs[n_out:]
    h = _rms(x_ref[...], g_ref[...]).astype(BF16)
    for w_ref, o_ref, sc in zip(w_refs, o_refs, scales):
        y = _dot(h, w_ref[...])
        if sc != 1.0:
            y = y * sc
        o_ref[...] = y.astype(o_ref.dtype)


def _norm_matmul(x2d, g, ws, out_dtypes, scales, tm, vmem_mib=48):
    t, d = x2d.shape
    in_specs = [pl.BlockSpec((tm, d), lambda i: (i, 0)),
                pl.BlockSpec((1, d), lambda i: (0, 0))]
    in_specs += [pl.BlockSpec(w.shape, lambda i: (0, 0)) for w in ws]
    out_specs = [pl.BlockSpec((tm, w.shape[1]), lambda i: (i, 0)) for w in ws]
    out_shape = [jax.ShapeDtypeStruct((t, w.shape[1]), dt) for w, dt in zip(ws, out_dtypes)]
    return pl.pallas_call(
        functools.partial(_norm_matmul_kernel, n_out=len(ws), scales=tuple(scales)),
        grid=(t // tm,), in_specs=in_specs, out_specs=out_specs, out_shape=out_shape,
        compiler_params=_params(("parallel",), vmem_mib),
        name="norm_matmul",
    )(x2d, g.reshape(1, d), *ws)


def _proj_views_kernel(x_ref, g_ref, *refs, n_view, n_plain, dils, view_scales, plain_scales):
    nd = len(dils)
    n_w = n_view + n_plain
    w_refs = refs[:n_w]
    view_refs = refs[n_w:n_w + n_view * nd]
    plain_refs = refs[n_w + n_view * nd:n_w + n_view * nd + n_plain]
    slab_ref = refs[-1]
    tm = x_ref.shape[0]
    h = _rms(x_ref[...], g_ref[...]).astype(BF16)
    for i in range(n_view):
        y = _dot(h, w_refs[i][...])
        if view_scales[i] != 1.0:
            y = y * view_scales[i]
        width = y.shape[1]
        tiles = width // LANES
        for t in range(tiles):
            slab_ref[t] = y[:, t * LANES:(t + 1) * LANES]
        for di, dil in enumerate(dils):
            o_ref = view_refs[i * nd + di]
            if dil == 1:
                o_ref[...] = y.astype(o_ref.dtype)
                continue
            for r in range(dil):
                for t in range(tiles):
                    col = r * width + t * LANES
                    o_ref[:, col:col + LANES] = slab_ref[t, pl.ds(r, tm // dil, stride=dil), :].astype(o_ref.dtype)
    for i in range(n_plain):
        y = _dot(h, w_refs[n_view + i][...])
        if plain_scales[i] != 1.0:
            y = y * plain_scales[i]
        plain_refs[i][...] = y.astype(plain_refs[i].dtype)


def _proj_views(x2d, g, view_ws, view_scales, plain_ws, plain_scales, dils, tm=512):
    t, d = x2d.shape
    ws = list(view_ws) + list(plain_ws)
    full = lambda w: pl.BlockSpec(w.shape, lambda i: (0, 0))
    out_specs, out_shape = [], []
    for w in view_ws:
        n = w.shape[1]
        for dil in dils:
            out_specs.append(pl.BlockSpec((tm // dil, dil * n), lambda i: (i, 0)))
            out_shape.append(jax.ShapeDtypeStruct((t // dil, dil * n), BF16))
    for w in plain_ws:
        out_specs.append(pl.BlockSpec((tm, w.shape[1]), lambda i: (i, 0)))
        out_shape.append(jax.ShapeDtypeStruct((t, w.shape[1]), BF16))
    width = view_ws[0].shape[1]
    return pl.pallas_call(
        functools.partial(_proj_views_kernel, n_view=len(view_ws), n_plain=len(plain_ws), dils=tuple(dils),
                          view_scales=tuple(view_scales), plain_scales=tuple(plain_scales)),
        grid=(t // tm,),
        in_specs=[pl.BlockSpec((tm, d), lambda i: (i, 0)), pl.BlockSpec((1, d), lambda i: (0, 0))]
                 + [full(w) for w in ws],
        out_specs=out_specs, out_shape=out_shape,
        scratch_shapes=[pltpu.VMEM((width // LANES, tm, LANES), F32)],
        compiler_params=_params(("parallel",), 48),
        name="proj_views",
    )(x2d, g.reshape(1, d), *ws)


def _band_attn_kernel(bias_ref, q_ref, kp_ref, kc_ref, vp_ref, vc_ref, o_ref, lse_ref):
    hd = DIL_HEAD_DIM
    bias = bias_ref[0]
    lane = lax.broadcasted_iota(jnp.int32, (1, LANES), 1)
    ones = jnp.ones((2 * BLK, LANES), BF16)
    lse_tile = jnp.zeros((BLK, LANES), F32)
    for t in range(DIL_HEADS * hd // LANES):
        sl = slice(t * LANES, (t + 1) * LANES)
        q = q_ref[0, :, sl]
        k2 = jnp.concatenate([kp_ref[0, :, sl], kc_ref[0, :, sl]], axis=0)
        v2 = jnp.concatenate([vp_ref[0, :, sl], vc_ref[0, :, sl]], axis=0)
        v_ext = jnp.concatenate([v2, ones], axis=-1)
        pair = []
        for half in range(LANES // hd):
            in_head = (lane // hd) == half
            s = _dot_nt(jnp.where(in_head, q, jnp.zeros_like(q)), k2) + bias
            m = jnp.max(jnp.maximum(s[:, :BLK], s[:, BLK:]), axis=-1, keepdims=True)
            p = jnp.exp(s - m).astype(BF16)
            pv = _dot(p, v_ext)
            den = pv[:, LANES:]
            pair.append(pv[:, :LANES] / den)
            h = t * (LANES // hd) + half
            per_head = LANES // DIL_HEADS
            lse_tile = jnp.where((lane // per_head) == h, m + jnp.log(den), lse_tile)
        o_ref[0, :, sl] = jnp.where((lane // hd) == 0, pair[0], pair[1]).astype(o_ref.dtype)
    lse_ref[0] = lse_tile


def _band_bias(window):
    i = jnp.arange(BLK)[:, None]
    j = jnp.arange(2 * BLK)[None, :]
    dist = BLK + i - j
    band = (dist >= 0) & (dist <= window)
    first = band & (j >= BLK)
    return jnp.where(jnp.stack([first, band]), 0.0, NEG).astype(F32)


def _band_attention(q, k, v, batch, window, dil):
    w = DIL_HEADS * DIL_HEAD_DIM
    l = q.shape[0] // batch
    nb = l // BLK
    view = lambda t: t.reshape(batch, l, dil * w)
    cur = pl.BlockSpec((1, BLK, w), lambda bi, r, n: (bi, n, r))
    prev = pl.BlockSpec((1, BLK, w), lambda bi, r, n: (bi, jnp.maximum(n - 1, 0), r))
    o, lse = pl.pallas_call(
        _band_attn_kernel,
        grid=(batch, dil, nb),
        in_specs=[pl.BlockSpec((1, BLK, 2 * BLK), lambda bi, r, n: (jnp.minimum(n, 1), 0, 0)),
                  cur, prev, cur, prev, cur],
        out_specs=[cur, pl.BlockSpec((1, BLK, LANES), lambda bi, r, n: (bi, n, r))],
        out_shape=[jax.ShapeDtypeStruct((batch, l, dil * w), BF16),
                   jax.ShapeDtypeStruct((batch, l, dil * LANES), F32)],
        compiler_params=_params(("parallel", "parallel", "parallel"), 32),
        name="band_attention",
    )(_band_bias(window // dil), view(q), view(k), view(k), view(v), view(v))
    return o.reshape(batch * l, dil * w), lse.reshape(batch * l, dil * LANES)


def _retention_kernel(q_ref, k_ref, v_ref, g_ref, cos_ref, sin_ref, dq_ref, dk_ref, dec_ref,
                      cdec_ref, gn_ref, o_ref, state_ref):
    @pl.when(pl.program_id(1) == 0)
    def _():
        state_ref[...] = jnp.zeros_like(state_ref)

    cos, sin = cos_ref[...], sin_ref[...]
    half = RET_HEADS * RET_KEY_DIM // 2

    def rot(t):
        t1, t2 = t[:, :half], t[:, half:]
        return jnp.concatenate([t1 * cos - t2 * sin, t1 * sin + t2 * cos], axis=-1)

    rq = rot(q_ref[0])
    rk = rot(k_ref[0]) * (RET_KEY_DIM ** -0.5)
    v = v_ref[0]
    rkb = rk.astype(BF16)
    lane_head = (lax.broadcasted_iota(jnp.int32, (1, 2 * half), 1) % half) // (RET_KEY_DIM // 2)
    state = state_ref[...]
    o_cross = _dot((rq * dq_ref[...]).astype(BF16), state.astype(BF16))
    outs = []
    for h in range(RET_HEADS):
        qh = jnp.where(lane_head == h, rq, 0.0).astype(BF16)
        s = _dot_nt(qh, rkb) * dec_ref[h]
        outs.append(_dot(s.astype(BF16), v[:, h * RET_VAL_DIM:(h + 1) * RET_VAL_DIM]))
    o = jnp.concatenate(outs, axis=-1) + o_cross
    k_end_t = (rk * dk_ref[...]).T.astype(BF16)
    kv = _dot(k_end_t, v)
    row_head = (lax.broadcasted_iota(jnp.int32, kv.shape, 0) % half) // (RET_KEY_DIM // 2)
    col_head = lax.broadcasted_iota(jnp.int32, kv.shape, 1) // RET_VAL_DIM
    state_ref[...] = state * cdec_ref[...] + jnp.where(row_head == col_head, kv, 0.0)
    normed = []
    for h in range(RET_HEADS):
        oh = o[:, h * RET_VAL_DIM:(h + 1) * RET_VAL_DIM]
        mu = jnp.mean(oh, axis=-1, keepdims=True)
        cen = oh - mu
        var = jnp.mean(cen * cen, axis=-1, keepdims=True)
        normed.append(cen * lax.rsqrt(var + EPS))
    y = jnp.concatenate(normed, axis=-1) * gn_ref[...]
    o_ref[0] = (_silu(g_ref[0].astype(F32)) * y).astype(o_ref.dtype)


def _retention_tables(s):
    h, dk, dv = RET_HEADS, RET_KEY_DIM, RET_VAL_DIM
    quarter = dk // 2
    pos = jnp.arange(s, dtype=F32)
    inv_freq = ROPE_BASE ** (-jnp.arange(quarter, dtype=F32) / quarter)
    ang = pos[:, None] * jnp.tile(inv_freq, h)[None, :]
    log_gamma = jnp.log1p(-jnp.exp2(-5.0 - jnp.arange(h, dtype=F32)))
    idx = jnp.arange(BLK, dtype=F32)
    rel = idx[:, None] - idx[None, :]
    decay = jnp.where(rel >= 0, jnp.exp(log_gamma[:, None, None] * jnp.maximum(rel, 0.0)), 0.0)
    lane_gamma = jnp.tile(jnp.repeat(log_gamma, quarter), 2)
    dq = jnp.exp((idx + 1.0)[:, None] * lane_gamma[None, :])
    dkt = jnp.exp((BLK - 1 - idx)[:, None] * lane_gamma[None, :])
    cdec = jnp.repeat(jnp.exp(log_gamma * BLK), dv)[None, :]
    return jnp.cos(ang), jnp.sin(ang), dq, dkt, decay, cdec


def _retention(q, k, v, g, gn_g):
    b, s, wk = q.shape
    wv = v.shape[-1]
    cos, sin, dq, dkt, decay, cdec = _retention_tables(s)
    tok = lambda w: pl.BlockSpec((1, BLK, w), lambda bi, c: (bi, c, 0))
    const = lambda shape: pl.BlockSpec(shape, lambda bi, c: (0,) * len(shape))
    return pl.pallas_call(
        _retention_kernel,
        grid=(b, s // BLK),
        in_specs=[tok(wk), tok(wk), tok(wv), tok(wv),
                  pl.BlockSpec((BLK, wk // 2), lambda bi, c: (c, 0)),
                  pl.BlockSpec((BLK, wk // 2), lambda bi, c: (c, 0)),
                  const((BLK, wk)), const((BLK, wk)), const((RET_HEADS, BLK, BLK)),
                  const((1, wv)), const((1, wv))],
        out_specs=tok(wv),
        out_shape=jax.ShapeDtypeStruct((b, s, wv), BF16),
        scratch_shapes=[pltpu.VMEM((wk, wv), F32)],
        compiler_params=_params(("parallel", "arbitrary"), 32),
        name="retention",
    )(q, k, v, g, cos, sin, dq, dkt, decay, cdec, gn_g.reshape(1, wv))


def _diff_attn_kernel(lam_ref, g_ref, q_ref, k_ref, v_ref, o_ref, s_ref, m_ref, vext_ref, acc_ref, *,
                      tq, lam_init):
    d = DIFF_HEAD_DIM
    hw = 2 * d
    nl = tq // LANES
    qi = pl.program_id(2)
    q = q_ref[0]
    qs = (q[:, :d], q[:, d:])

    @pl.when(qi == 0)
    def _():
        vext_ref[:, :hw] = v_ref[0]
        vext_ref[:, hw:] = jnp.ones((vext_ref.shape[0], hw), BF16)

    def fold_max(x):
        r = x[:, :LANES]
        for c in range(1, nl):
            r = jnp.maximum(r, x[:, c * LANES:(c + 1) * LANES])
        return r

    m_ref[...] = jnp.full(m_ref.shape, NEG, F32)

    def scores(j, diagonal):
        k = k_ref[0, pl.ds(pl.multiple_of(j * tq, tq), tq), :]
        for c in range(2):
            s = _dot_nt(qs[c], k[:, c * d:(c + 1) * d]) * LOG2E
            if diagonal:
                row = lax.broadcasted_iota(jnp.int32, (tq, tq), 0)
                col = lax.broadcasted_iota(jnp.int32, (tq, tq), 1)
                s = jnp.where(col <= row, s, NEG)
            s_ref[c, j] = s
            m_ref[c] = jnp.maximum(m_ref[c], fold_max(s))

    def score_body(j, carry):
        scores(j, False)
        return carry

    lax.fori_loop(0, qi, score_body, 0)
    scores(qi, True)

    for c in range(2):
        m_ref[c] = jnp.broadcast_to(jnp.max(m_ref[c], axis=-1, keepdims=True), (tq, LANES))
    acc_ref[...] = jnp.zeros(acc_ref.shape, F32)

    def accumulate(j, carry):
        v = vext_ref[pl.ds(pl.multiple_of(j * tq, tq), tq), :]
        for c in range(2):
            m = m_ref[c]
            s = s_ref[c, j]
            p = jnp.concatenate([jnp.exp2(s[:, t * LANES:(t + 1) * LANES] - m) for t in range(nl)], axis=-1)
            acc_ref[c] += _dot(p.astype(BF16), v)
        return carry

    lax.fori_loop(0, qi + 1, accumulate, 0)

    lp = lam_ref[...]
    lam = (jnp.exp(jnp.sum(lp[0:1] * lp[1:2], keepdims=True))
           - jnp.exp(jnp.sum(lp[2:3] * lp[3:4], keepdims=True)) + lam_init)
    o1 = acc_ref[0, :, :hw] / acc_ref[0, :, hw:hw + 1]
    o2 = acc_ref[1, :, :hw] / acc_ref[1, :, hw:hw + 1]
    o_ref[0] = (_rms(o1 - lam * o2, g_ref[...]) * (1.0 - lam_init)).astype(o_ref.dtype)


def _diff_attention(q, k, v, lam_params, subln_g, lam_init, tq=512):
    b, s, w = q.shape
    hw = 2 * DIFF_HEAD_DIM
    return pl.pallas_call(
        functools.partial(_diff_attn_kernel, tq=tq, lam_init=lam_init),
        grid=(b, DIFF_HEADS, s // tq),
        in_specs=[pl.BlockSpec(lam_params.shape, lambda bi, h, i: (0, 0)),
                  pl.BlockSpec((1, hw), lambda bi, h, i: (0, 0)),
                  pl.BlockSpec((1, tq, hw), lambda bi, h, i: (bi, i, h)),
                  pl.BlockSpec((1, s, hw), lambda bi, h, i: (bi, 0, h)),
                  pl.BlockSpec((1, s, hw), lambda bi, h, i: (bi, 0, h))],
        out_specs=pl.BlockSpec((1, tq, hw), lambda bi, h, i: (bi, i, h)),
        out_shape=jax.ShapeDtypeStruct((b, s, w), BF16),
        scratch_shapes=[pltpu.VMEM((2, s // tq, tq, tq), F32), pltpu.VMEM((2, tq, LANES), F32),
                        pltpu.VMEM((s, 2 * hw), BF16), pltpu.VMEM((2, tq, 2 * hw), F32)],
        compiler_params=_params(("parallel", "parallel", "arbitrary"), 40),
        name="diff_attention",
    )(lam_params, subln_g.reshape(1, hw), q, k, v)


def _ssd_kernel(z_ref, xbc_ref, dt_ref, cw_ref, cb_ref, dtb_ref, alog_ref, dskip_ref, ng_ref,
                tri_ref, exp_ref, o_ref, carry_ref, ext_ref, state_ref):
    gn = SSM_GROUPS * SSM_STATE
    hpg = SSM_HEADS // SSM_GROUPS
    gw = hpg * SSM_HEAD_DIM
    tail = 8

    @pl.when(pl.program_id(1) == 0)
    def _():
        carry_ref[...] = jnp.zeros_like(carry_ref)
        state_ref[...] = jnp.zeros_like(state_ref)

    xbc = xbc_ref[0]
    ext_ref[0:tail, :] = carry_ref[...]
    ext_ref[tail:tail + BLK, :] = xbc
    carry_ref[...] = xbc[BLK - tail:, :]
    cw = cw_ref[...]
    conv = cb_ref[...] + cw[SSM_CONV - 1:SSM_CONV, :] * xbc
    for sft in range(1, SSM_CONV):
        conv = conv + cw[SSM_CONV - 1 - sft:SSM_CONV - sft, :] * ext_ref[tail - sft:tail - sft + BLK, :]
    xc = _silu(conv)
    xs, bm, cm = xc[:, :SSM_D_INNER], xc[:, SSM_D_INNER:SSM_D_INNER + gn], xc[:, SSM_D_INNER + gn:]

    pre = dt_ref[0] + dtb_ref[...]
    dt = jnp.maximum(pre, 0.0) + jnp.log1p(jnp.exp(-jnp.abs(pre)))
    lane = lax.broadcasted_iota(jnp.int32, (1, LANES), 1)
    a_neg = jnp.where(lane < SSM_HEADS, -jnp.exp(alog_ref[...]), 0.0)
    expand = exp_ref[...]
    a_col = _sel_dot(tri_ref[...], dt * a_neg)
    a_row = a_col.T
    a_exp = _dot_sel(a_col, expand)
    dt_exp = _dot_sel(dt, expand)
    total = a_exp[BLK - 1:BLK, :]
    xdt = xs * dt_exp
    x_end = (xdt * jnp.exp(total - a_exp)).astype(BF16)
    xdt_b = xdt.astype(BF16)
    row = lax.broadcasted_iota(jnp.int32, (BLK, BLK), 0)
    col = lax.broadcasted_iota(jnp.int32, (BLK, BLK), 1)
    col_head = lax.broadcasted_iota(jnp.int32, (1, gw), 1) // SSM_HEAD_DIM
    ys = []
    for g in range(SSM_GROUPS):
        b_g = bm[:, g * SSM_STATE:(g + 1) * SSM_STATE]
        c_g = cm[:, g * SSM_STATE:(g + 1) * SSM_STATE].astype(BF16)
        cb = _dot_nt(c_g, b_g.astype(BF16))
        xg = xdt_b[:, g * gw:(g + 1) * gw]
        y_g = jnp.zeros((BLK, gw), F32)
        for hh in range(hpg):
            h = g * hpg + hh
            seg = a_col[:, h:h + 1] - a_row[h:h + 1, :]
            m_h = (cb * jnp.exp(jnp.where(col <= row, seg, NEG))).astype(BF16)
            y_g = y_g + _dot(m_h, jnp.where(col_head == hh, xg, jnp.zeros_like(xg)))
        state = state_ref[g]
        y_g = y_g + _dot(c_g, state.astype(BF16)) * jnp.exp(a_exp[:, g * gw:(g + 1) * gw])
        new = _dot(b_g.T.astype(BF16), x_end[:, g * gw:(g + 1) * gw])
        state_ref[g] = state * jnp.exp(total[:, g * gw:(g + 1) * gw]) + new
        ys.append(y_g)
    y = jnp.concatenate(ys, axis=-1) + xs * dskip_ref[...]
    y = y * _silu(z_ref[0].astype(F32))
    o_ref[0] = _rms(y, ng_ref[...]).astype(o_ref.dtype)


def _ssd(z, xbc, dt, conv_w, conv_b, dt_bias, a_log, d_skip, norm_g):
    b, s, c = xbc.shape
    di = SSM_D_INNER
    pad = lambda t: jnp.pad(t.astype(F32), (0, LANES - t.shape[0])).reshape(1, LANES)
    tri = jnp.tril(jnp.ones((BLK, BLK), BF16))
    expand = (jnp.arange(LANES)[:, None] == (jnp.arange(di) // SSM_HEAD_DIM)[None, :]).astype(BF16)
    tok = lambda w: pl.BlockSpec((1, BLK, w), lambda bi, ci: (bi, ci, 0))
    const = lambda shape: pl.BlockSpec(shape, lambda bi, ci: (0,) * len(shape))
    return pl.pallas_call(
        _ssd_kernel,
        grid=(b, s // BLK),
        in_specs=[tok(di), tok(c), tok(LANES), const((SSM_CONV, c)), const((1, c)),
                  const((1, LANES)), const((1, LANES)), const((1, di)), const((1, di)),
                  const((BLK, BLK)), const((LANES, di))],
        out_specs=tok(di),
        out_shape=jax.ShapeDtypeStruct((b, s, di), BF16),
        scratch_shapes=[pltpu.VMEM((8, c), F32), pltpu.VMEM((8 + BLK, c), F32),
                        pltpu.VMEM((SSM_GROUPS, SSM_STATE, di // SSM_GROUPS), F32)],
        compiler_params=_params(("parallel", "arbitrary"), 32),
        name="ssd",
    )(z, xbc, dt, conv_w, conv_b.reshape(1, c), pad(dt_bias), pad(a_log),
      jnp.repeat(d_skip.astype(F32), SSM_HEAD_DIM).reshape(1, di), norm_g.reshape(1, di), tri, expand)


def _from_residue_view(src_ref, slab_ref, dil, width):
    if dil == 1:
        return src_ref[...].astype(F32)
    rows = src_ref.shape[0]
    tiles = width // LANES
    for r in range(dil):
        for t in range(tiles):
            col = r * width + t * LANES
            slab_ref[t, pl.ds(r, rows, stride=dil), :] = src_ref[:, col:col + LANES].astype(F32)
    return jnp.concatenate([slab_ref[t] for t in range(tiles)], axis=-1)


def _merge_kernel(x_ref, *refs, dils):
    n = len(dils)
    o_refs, l_refs = refs[:n], refs[n:2 * n]
    yb_ref, yc_ref, yd_ref, gate_ref, wb_ref, wo_ref, hexp_ref, out_ref = refs[2 * n:2 * n + 8]
    o_slabs, l_slabs = refs[2 * n + 8:3 * n + 8], refs[3 * n + 8:]
    d = x_ref.shape[-1]
    bw = yb_ref.shape[-1]
    lses = [_from_residue_view(l_refs[i], l_slabs[i], dils[i], LANES) for i in range(n)]
    m = functools.reduce(jnp.maximum, lses)
    es = [jnp.exp(l - m) for l in lses]
    inv = 1.0 / functools.reduce(jnp.add, es)
    ya = jnp.zeros((x_ref.shape[0], bw), F32)
    for i in range(n):
        wt = _dot((es[i] * inv).astype(BF16), hexp_ref[...])
        ya = ya + wt * _from_residue_view(o_refs[i], o_slabs[i], dils[i], bw)
    branches = (ya.astype(BF16), yb_ref[...], yc_ref[...], yd_ref[...])
    merged = jnp.zeros((x_ref.shape[0], d), F32)
    for i, y in enumerate(branches):
        gate = _sigmoid(gate_ref[:, i * d:(i + 1) * d].astype(F32))
        merged = merged + gate * _dot(y, wb_ref[i])
    out_ref[...] = x_ref[...] + _dot(merged.astype(BF16), wo_ref[...])


def _merge(x2d, o_parts, lse_parts, dils, yb, yc, yd, gates, w_branch, w_out, tm=256):
    t, d = x2d.shape
    bw = yb.shape[-1]
    row = lambda w: pl.BlockSpec((tm, w), lambda i: (i, 0))
    view = lambda w, dil: pl.BlockSpec((tm // dil, dil * w), lambda i: (i, 0))
    per_head = LANES // DIL_HEADS
    hexp = (jnp.arange(LANES)[:, None]
            == (jnp.arange(bw) // DIL_HEAD_DIM * per_head)[None, :]).astype(BF16)
    return pl.pallas_call(
        functools.partial(_merge_kernel, dils=tuple(dils)),
        grid=(t // tm,),
        in_specs=[row(d)] + [view(bw, dil) for dil in dils] + [view(LANES, dil) for dil in dils]
                 + [row(bw)] * 3 + [row(N_BRANCH * d),
                                    pl.BlockSpec(w_branch.shape, lambda i: (0, 0, 0)),
                                    pl.BlockSpec(w_out.shape, lambda i: (0, 0)),
                                    pl.BlockSpec(hexp.shape, lambda i: (0, 0))],
        out_specs=row(d),
        out_shape=jax.ShapeDtypeStruct((t, d), F32),
        scratch_shapes=[pltpu.VMEM((bw // LANES, tm, LANES), F32) for _ in dils]
                       + [pltpu.VMEM((1, tm, LANES), F32) for _ in dils],
        compiler_params=_params(("parallel",), 48),
        name="merge",
    )(x2d, *o_parts, *lse_parts, yb, yc, yd, gates, w_branch, w_out, hexp)


def _xattn_kernel(x_ref, g_ref, wq_ref, k_ref, v_ref, wo_ref, out_ref):
    x = x_ref[0]
    q = _dot(_rms(x, g_ref[...]).astype(BF16), wq_ref[...]).astype(BF16)
    k, v = k_ref[0], v_ref[0]
    outs = []
    for h in range(X_HEADS):
        sl = slice(h * X_HEAD_DIM, (h + 1) * X_HEAD_DIM)
        s = _dot_nt(q[:, sl], k[:, sl]) * (X_HEAD_DIM ** -0.5)
        p = jnp.exp(s - jnp.max(s, axis=-1, keepdims=True))
        a = p / jnp.sum(p, axis=-1, keepdims=True)
        outs.append(_dot(a.astype(BF16), v[:, sl]))
    o = jnp.concatenate(outs, axis=-1).astype(BF16)
    out_ref[0] = x + _dot(o, wo_ref[...])


def _cross_attention(x, g, w_q, k, v, w_o, tm=512):
    b, s, d = x.shape
    m, w = k.shape[1], k.shape[2]
    return pl.pallas_call(
        _xattn_kernel,
        grid=(b, s // tm),
        in_specs=[pl.BlockSpec((1, tm, d), lambda bi, i: (bi, i, 0)),
                  pl.BlockSpec((1, d), lambda bi, i: (0, 0)),
                  pl.BlockSpec(w_q.shape, lambda bi, i: (0, 0)),
                  pl.BlockSpec((1, m, w), lambda bi, i: (bi, 0, 0)),
                  pl.BlockSpec((1, m, w), lambda bi, i: (bi, 0, 0)),
                  pl.BlockSpec(w_o.shape, lambda bi, i: (0, 0))],
        out_specs=pl.BlockSpec((1, tm, d), lambda bi, i: (bi, i, 0)),
        out_shape=jax.ShapeDtypeStruct((b, s, d), F32),
        compiler_params=_params(("parallel", "parallel"), 48),
        name="cross_attention",
    )(x, g.reshape(1, d), w_q, k, v, w_o)


def _ffn_kernel(x_ref, g_ref, wa_ref, wb_ref, wo_ref, fg_ref, out_ref, *, final_norm):
    x = x_ref[...]
    h = _rms(x, g_ref[...]).astype(BF16)
    a = _dot(h, wa_ref[...])
    b = _dot(h, wb_ref[...])
    y = x + _dot((_silu(a) * b).astype(BF16), wo_ref[...])
    if final_norm:
        y = _rms(y, fg_ref[...])
    out_ref[...] = y


def _ffn(x2d, g, w_a, w_b, w_o, final_g, final_norm, tm=256):
    t, d = x2d.shape
    full = lambda w: pl.BlockSpec(w.shape, lambda i: (0, 0))
    return pl.pallas_call(
        functools.partial(_ffn_kernel, final_norm=final_norm),
        grid=(t // tm,),
        in_specs=[pl.BlockSpec((tm, d), lambda i: (i, 0)), pl.BlockSpec((1, d), lambda i: (0, 0)),
                  full(w_a), full(w_b), full(w_o), pl.BlockSpec((1, d), lambda i: (0, 0))],
        out_specs=pl.BlockSpec((tm, d), lambda i: (i, 0)),
        out_shape=jax.ShapeDtypeStruct((t, d), F32),
        compiler_params=_params(("parallel",), 56),
        name="ffn",
    )(x2d, g.reshape(1, d), w_a, w_b, w_o, final_g.reshape(1, d))


def _ret_feature_order():
    quarter = RET_KEY_DIM // 2
    first = [h * RET_KEY_DIM + f for h in range(RET_HEADS) for f in range(quarter)]
    return np.array(first + [c + quarter for c in first])


def _layer(x, mem, p, lam_init, final_g, final_norm):
    b, s, d = x.shape
    t = b * s
    x2d = x.reshape(t, d)
    a_w = DIL_HEADS * DIL_HEAD_DIM
    b_qk, b_vg = RET_HEADS * RET_KEY_DIM, RET_HEADS * RET_VAL_DIM
    c_w = DIFF_HEADS * 2 * DIFF_HEAD_DIM
    d_xbc = SSM_D_INNER + 2 * SSM_GROUPS * SSM_STATE
    sizes = (a_w, a_w, a_w, b_qk, b_qk, b_vg, b_vg, c_w, c_w, c_w, SSM_D_INNER, d_xbc, SSM_HEADS,
             N_BRANCH * d)
    offs = np.concatenate([[0], np.cumsum(sizes)])
    w_in = p['w_in']
    col = lambda i: w_in[:, offs[i]:offs[i + 1]]
    order = _ret_feature_order()
    w_dt = jnp.pad(col(12), ((0, 0), (0, LANES - SSM_HEADS)))
    bf = lambda w: w.astype(BF16)

    dils = [dil for _, dil in DIL_PATTERNS]
    nd = len(dils)
    proj = _proj_views(x2d, p['norm_mix_g'], [bf(col(i)) for i in (0, 1, 2)],
                       [DIL_HEAD_DIM ** -0.5, 1.0, 1.0], [bf(col(i)) for i in (7, 8, 9)],
                       [DIFF_HEAD_DIM ** -0.5, 1.0, 1.0], dils)
    aq, ak, av = proj[0:nd], proj[nd:2 * nd], proj[2 * nd:3 * nd]
    cq, ck, cv = proj[3 * nd:]
    bq, bk, bv, bg, dz, dxbc, ddt = _norm_matmul(
        x2d, p['norm_mix_g'],
        [bf(col(3)[:, order]), bf(col(4)[:, order]), bf(col(5)), bf(col(6)), bf(col(10)), bf(col(11)),
         bf(w_dt)],
        [F32, F32, BF16, BF16, BF16, F32, F32], [1.0] * 7, tm=512)
    (gates,) = _norm_matmul(x2d, p['norm_mix_g'], [bf(col(13))], [BF16], [1.0], tm=512)

    r3 = lambda y: y.reshape(b, s, y.shape[-1])
    o_parts, lse_parts = [], []
    for i, (window, dil) in enumerate(DIL_PATTERNS):
        o_p, lse_p = _band_attention(aq[i], ak[i], av[i], b, window, dil)
        o_parts.append(o_p)
        lse_parts.append(lse_p)
    y_b = _retention(r3(bq), r3(bk), r3(bv), r3(bg), p['ret_gn_g'])
    y_c = _diff_attention(r3(cq), r3(ck), r3(cv), p['diff_lambda'], p['diff_subln_g'], lam_init)
    y_d = _ssd(r3(dz), r3(dxbc), r3(ddt), p['ssm_conv_w'], p['ssm_conv_b'], p['ssm_dt_bias'],
               p['ssm_A_log'], p['ssm_D'], p['ssm_norm_g'])
    x2d = _merge(x2d, o_parts, lse_parts, dils, y_b.reshape(t, -1), y_c.reshape(t, -1),
                 y_d.reshape(t, -1), gates, bf(p['w_branch']), bf(p['w_mix_out']))

    xw = X_HEADS * X_HEAD_DIM
    m2d = mem.reshape(-1, d)
    mk, mv = _norm_matmul(m2d, p['norm_mem_g'], [bf(p['w_xkv'][:, :xw]), bf(p['w_xkv'][:, xw:])],
                          [BF16, BF16], [1.0, 1.0], tm=256)
    mlen = mem.shape[1]
    x3 = _cross_attention(x2d.reshape(b, s, d), p['norm_x_g'], bf(p['w_xq']),
                          mk.reshape(b, mlen, xw), mv.reshape(b, mlen, xw), bf(p['w_xo']))
    hid = p['w_ffn_out'].shape[0]
    x2d = _ffn(x3.reshape(t, d), p['norm_ffn_g'], bf(p['w_ffn_in'][:, :hid]), bf(p['w_ffn_in'][:, hid:]),
               bf(p['w_ffn_out']), final_g, final_norm)
    return x2d.reshape(b, s, d)


def kernel(x, mem, norm_mix_g, w_in, ret_gn_g, diff_lambda, diff_subln_g, ssm_conv_w, ssm_conv_b,
           ssm_dt_bias, ssm_A_log, ssm_D, ssm_norm_g, w_branch, w_mix_out, norm_x_g, norm_mem_g, w_xq,
           w_xkv, w_xo, norm_ffn_g, w_ffn_in, w_ffn_out, norm_f_g):
    stacked = dict(norm_mix_g=norm_mix_g, w_in=w_in, ret_gn_g=ret_gn_g, diff_lambda=diff_lambda,
                   diff_subln_g=diff_subln_g, ssm_conv_w=ssm_conv_w, ssm_conv_b=ssm_conv_b,
                   ssm_dt_bias=ssm_dt_bias, ssm_A_log=ssm_A_log, ssm_D=ssm_D, ssm_norm_g=ssm_norm_g,
                   w_branch=w_branch, w_mix_out=w_mix_out, norm_x_g=norm_x_g, norm_mem_g=norm_mem_g,
                   w_xq=w_xq, w_xkv=w_xkv, w_xo=w_xo, norm_ffn_g=norm_ffn_g, w_ffn_in=w_ffn_in,
                   w_ffn_out=w_ffn_out)
    depth = w_in.shape[0]
    for l in range(depth):
        lam_init = 0.8 - 0.6 * math.exp(-0.3 * l)
        p = {name: val[l] for name, val in stacked.items()}
        x = _layer(x, mem, p, lam_init, norm_f_g, final_norm=(l == depth - 1))
    return x
```

```python
import functools
import math

import jax
import jax.numpy as jnp
import numpy as np
from jax import lax
from jax.experimental import pallas as pl
from jax.experimental.pallas import tpu as pltpu

F32 = jnp.float32
BF16 = jnp.bfloat16

EPS = 1e-6
NEG = -1e30
LOG2E = math.log2(math.e)
BLK = 128

DIL_HEADS = 8
DIL_HEAD_DIM = 64
DIL_PATTERNS = ((128, 1), (512, 4), (2048, 16))
RET_HEADS = 4
RET_KEY_DIM = 64
RET_VAL_DIM = 128
ROPE_BASE = 10000.0
DIFF_HEADS = 4
DIFF_HEAD_DIM = 64
SSM_D_INNER = 512
SSM_HEAD_DIM = 64
SSM_HEADS = 8
SSM_GROUPS = 2
SSM_STATE = 128
SSM_CONV = 4
N_BRANCH = 4
BRANCH_WIDTH = 512
X_HEADS = 4
X_HEAD_DIM = 128
LANES = 128

MIB = 1024 * 1024


def _params(semantics, vmem_mib):
    return pltpu.CompilerParams(dimension_semantics=semantics,
                                vmem_limit_bytes=vmem_mib * MIB)


def _dot(a, b):
    return jnp.dot(a, b, preferred_element_type=F32)


def _dot_nt(a, b):
    return lax.dot_general(a, b, (((1,), (1,)), ((), ())), preferred_element_type=F32)


def _split3(x):
    x1 = x.astype(BF16)
    r1 = x - x1.astype(F32)
    x2 = r1.astype(BF16)
    x3 = (r1 - x2.astype(F32)).astype(BF16)
    return x1, x2, x3


def _dot_sel(x, sel):
    x1, x2, x3 = _split3(x)
    return _dot(x1, sel) + _dot(x2, sel) + _dot(x3, sel)


def _sel_dot(sel, x):
    x1, x2, x3 = _split3(x)
    return _dot(sel, x1) + _dot(sel, x2) + _dot(sel, x3)


def _sigmoid(x):
    return 1.0 / (1.0 + jnp.exp(-x))


def _silu(x):
    return x * _sigmoid(x)


def _rms(x, g):
    ms = jnp.mean(x * x, axis=-1, keepdims=True)
    return x * lax.rsqrt(ms + EPS) * g


def _norm_matmul_kernel(x_ref, g_ref, *refs, n_out, scales):
    w_refs, o_refs = refs[:n_out], refs[n_out:]
    h = _rms(x_ref[...], g_ref[...]).astype(BF16)
    for w_ref, o_ref, sc in zip(w_refs, o_refs, scales):
        y = _dot(h, w_ref[...])
        if sc != 1.0:
            y = y * sc
        o_ref[...] = y.astype(o_ref.dtype)


def _norm_matmul(x2d, g, ws, out_dtypes, scales, tm, vmem_mib=48):
    t, d = x2d.shape
    in_specs = [pl.BlockSpec((tm, d), lambda i: (i, 0)),
                pl.BlockSpec((1, d), lambda i: (0, 0))]
    in_specs += [pl.BlockSpec(w.shape, lambda i: (0, 0)) for w in ws]
    out_specs = [pl.BlockSpec((tm, w.shape[1]), lambda i: (i, 0)) for w in ws]
    out_shape = [jax.ShapeDtypeStruct((t, w.shape[1]), dt) for w, dt in zip(ws, out_dtypes)]
    return pl.pallas_call(
        functools.partial(_norm_matmul_kernel, n_out=len(ws), scales=tuple(scales)),
        grid=(t // tm,), in_specs=in_specs, out_specs=out_specs, out_shape=out_shape,
        compiler_params=_params(("parallel",), vmem_mib),
        name="norm_matmul",
    )(x2d, g.reshape(1, d), *ws)


def _proj_views_kernel(x_ref, g_ref, *refs, n_view, n_plain, dils, view_scales, plain_scales):
    nd = len(dils)
    n_w = n_view + n_plain
    w_refs = refs[:n_w]
    view_refs = refs[n_w:n_w + n_view * nd]
    plain_refs = refs[n_w + n_view * nd:n_w + n_view * nd + n_plain]
    slab_ref = refs[-1]
    tm = x_ref.shape[0]
    h = _rms(x_ref[...], g_ref[...]).astype(BF16)
    for i in range(n_view):
        y = _dot(h, w_refs[i][...])
        if view_scales[i] != 1.0:
            y = y * view_scales[i]
        width = y.shape[1]
        tiles = width // LANES
        for t in range(tiles):
            slab_ref[t] = y[:, t * LANES:(t + 1) * LANES]
        for di, dil in enumerate(dils):
            o_ref = view_refs[i * nd + di]
            if dil == 1:
                o_ref[...] = y.astype(o_ref.dtype)
                continue
            for r in range(dil):
                for t in range(tiles):
                    col = r * width + t * LANES
                    o_ref[:, col:col + LANES] = slab_ref[t, pl.ds(r, tm // dil, stride=dil), :].astype(o_ref.dtype)
    for i in range(n_plain):
        y = _dot(h, w_refs[n_view + i][...])
        if plain_scales[i] != 1.0:
            y = y * plain_scales[i]
        plain_refs[i][...] = y.astype(plain_refs[i].dtype)


def _proj_views(x2d, g, view_ws, view_scales, plain_ws, plain_scales, dils, tm=512):
    t, d = x2d.shape
    ws = list(view_ws) + list(plain_ws)
    full = lambda w: pl.BlockSpec(w.shape, lambda i: (0, 0))
    out_specs, out_shape = [], []
    for w in view_ws:
        n = w.shape[1]
        for dil in dils:
            out_specs.append(pl.BlockSpec((tm // dil, dil * n), lambda i: (i, 0)))
            out_shape.append(jax.ShapeDtypeStruct((t // dil, dil * n), BF16))
    for w in plain_ws:
        out_specs.append(pl.BlockSpec((tm, w.shape[1]), lambda i: (i, 0)))
        out_shape.append(jax.ShapeDtypeStruct((t, w.shape[1]), BF16))
    width = view_ws[0].shape[1]
    return pl.pallas_call(
        functools.partial(_proj_views_kernel, n_view=len(view_ws), n_plain=len(plain_ws), dils=tuple(dils),
                          view_scales=tuple(view_scales), plain_scales=tuple(plain_scales)),
        grid=(t // tm,),
        in_specs=[pl.BlockSpec((tm, d), lambda i: (i, 0)), pl.BlockSpec((1, d), lambda i: (0, 0))]
                 + [full(w) for w in ws],
        out_specs=out_specs, out_shape=out_shape,
        scratch_shapes=[pltpu.VMEM((width // LANES, tm, LANES), F32)],
        compiler_params=_params(("parallel",), 48),
        name="proj_views",
    )(x2d, g.reshape(1, d), *ws)


def _band_attn_kernel(*refs, n_pat):
    ins, outs = refs[:6 * n_pat], refs[6 * n_pat:]
    for i in range(n_pat):
        _band_block(*ins[6 * i:6 * i + 6], *outs[2 * i:2 * i + 2])


def _band_block(bias_ref, q_ref, kp_ref, kc_ref, vp_ref, vc_ref, o_ref, lse_ref):
    hd = DIL_HEAD_DIM
    bias = bias_ref[0]
    lane = lax.broadcasted_iota(jnp.int32, (1, LANES), 1)
    ones = jnp.ones((2 * BLK, LANES), BF16)
    lse_tile = jnp.zeros((BLK, LANES), F32)
    for t in range(DIL_HEADS * hd // LANES):
        sl = slice(t * LANES, (t + 1) * LANES)
        q = q_ref[0, :, sl]
        k2 = jnp.concatenate([kp_ref[0, :, sl], kc_ref[0, :, sl]], axis=0)
        v2 = jnp.concatenate([vp_ref[0, :, sl], vc_ref[0, :, sl]], axis=0)
        v_ext = jnp.concatenate([v2, ones], axis=-1)
        pair = []
        for half in range(LANES // hd):
            in_head = (lane // hd) == half
            s = _dot_nt(jnp.where(in_head, q, jnp.zeros_like(q)), k2) + bias
            m = jnp.max(jnp.maximum(s[:, :BLK], s[:, BLK:]), axis=-1, keepdims=True)
            p = jnp.exp(s - m).astype(BF16)
            pv = _dot(p, v_ext)
            den = pv[:, LANES:]
            pair.append(pv[:, :LANES] / den)
            h = t * (LANES // hd) + half
            per_head = LANES // DIL_HEADS
            lse_tile = jnp.where((lane // per_head) == h, m + jnp.log(den), lse_tile)
        o_ref[0, :, sl] = jnp.where((lane // hd) == 0, pair[0], pair[1]).astype(o_ref.dtype)
    lse_ref[0] = lse_tile


def _band_bias(window):
    i = jnp.arange(BLK)[:, None]
    j = jnp.arange(2 * BLK)[None, :]
    dist = BLK + i - j
    band = (dist >= 0) & (dist <= window)
    first = band & (j >= BLK)
    return jnp.where(jnp.stack([first, band]), 0.0, NEG).astype(F32)


def _band_attention(qs, ks, vs, batch, seq):
    w = DIL_HEADS * DIL_HEAD_DIM
    steps = seq // BLK
    args, in_specs, out_specs, out_shape = [], [], [], []
    for (window, dil), q, k, v in zip(DIL_PATTERNS, qs, ks, vs):
        l = seq // dil
        nb = l // BLK
        view = lambda t, l=l, dil=dil: t.reshape(batch, l, dil * w)
        cur = lambda wd, nb=nb: pl.BlockSpec((1, BLK, wd), lambda bi, n: (bi, n % nb, n // nb))
        prev = pl.BlockSpec((1, BLK, w), lambda bi, n, nb=nb: (bi, jnp.maximum(n % nb - 1, 0), n // nb))
        bias = pl.BlockSpec((1, BLK, 2 * BLK), lambda bi, n, nb=nb: (jnp.minimum(n % nb, 1), 0, 0))
        args += [_band_bias(window // dil), view(q), view(k), view(k), view(v), view(v)]
        in_specs += [bias, cur(w), prev, cur(w), prev, cur(w)]
        out_specs += [cur(w), cur(LANES)]
        out_shape += [jax.ShapeDtypeStruct((batch, l, dil * w), BF16),
                      jax.ShapeDtypeStruct((batch, l, dil * LANES), F32)]
    outs = pl.pallas_call(
        functools.partial(_band_attn_kernel, n_pat=len(DIL_PATTERNS)),
        grid=(batch, steps),
        in_specs=in_specs, out_specs=out_specs, out_shape=out_shape,
        compiler_params=_params(("parallel", "parallel"), 32),
        name="band_attention",
    )(*args)
    flat = [o.reshape(-1, o.shape[-1]) for o in outs]
    return flat[0::2], flat[1::2]


def _retention_kernel(q_ref, k_ref, v_ref, g_ref, cos_ref, sin_ref, dq_ref, dk_ref, dec_ref,
                      cdec_ref, gn_ref, o_ref, state_ref):
    @pl.when(pl.program_id(0) == 0)
    def _():
        state_ref[...] = jnp.zeros_like(state_ref)

    cos, sin = cos_ref[...], sin_ref[...]
    half = RET_HEADS * RET_KEY_DIM // 2

    def rot(t):
        t1, t2 = t[:, :half], t[:, half:]
        return jnp.concatenate([t1 * cos - t2 * sin, t1 * sin + t2 * cos], axis=-1)

    lane_head = (lax.broadcasted_iota(jnp.int32, (1, 2 * half), 1) % half) // (RET_KEY_DIM // 2)
    for bi in range(q_ref.shape[0]):
        rq = rot(q_ref[bi])
        rk = rot(k_ref[bi]) * (RET_KEY_DIM ** -0.5)
        v = v_ref[bi]
        rkb = rk.astype(BF16)
        state = state_ref[bi]
        o_cross = _dot((rq * dq_ref[...]).astype(BF16), state.astype(BF16))
        outs = []
        for h in range(RET_HEADS):
            qh = jnp.where(lane_head == h, rq, 0.0).astype(BF16)
            s = _dot_nt(qh, rkb) * dec_ref[h]
            outs.append(_dot(s.astype(BF16), v[:, h * RET_VAL_DIM:(h + 1) * RET_VAL_DIM]))
        o = jnp.concatenate(outs, axis=-1) + o_cross
        k_end_t = (rk * dk_ref[...]).T.astype(BF16)
        kv = _dot(k_end_t, v)
        row_head = (lax.broadcasted_iota(jnp.int32, kv.shape, 0) % half) // (RET_KEY_DIM // 2)
        col_head = lax.broadcasted_iota(jnp.int32, kv.shape, 1) // RET_VAL_DIM
        state_ref[bi] = state * cdec_ref[...] + jnp.where(row_head == col_head, kv, 0.0)
        normed = []
        for h in range(RET_HEADS):
            oh = o[:, h * RET_VAL_DIM:(h + 1) * RET_VAL_DIM]
            mu = jnp.mean(oh, axis=-1, keepdims=True)
            cen = oh - mu
            var = jnp.mean(cen * cen, axis=-1, keepdims=True)
            normed.append(cen * lax.rsqrt(var + EPS))
        y = jnp.concatenate(normed, axis=-1) * gn_ref[...]
        o_ref[bi] = (_silu(g_ref[bi].astype(F32)) * y).astype(o_ref.dtype)


def _retention_tables(s):
    h, dk, dv = RET_HEADS, RET_KEY_DIM, RET_VAL_DIM
    quarter = dk // 2
    pos = jnp.arange(s, dtype=F32)
    inv_freq = ROPE_BASE ** (-jnp.arange(quarter, dtype=F32) / quarter)
    ang = pos[:, None] * jnp.tile(inv_freq, h)[None, :]
    log_gamma = jnp.log1p(-jnp.exp2(-5.0 - jnp.arange(h, dtype=F32)))
    idx = jnp.arange(BLK, dtype=F32)
    rel = idx[:, None] - idx[None, :]
    decay = jnp.where(rel >= 0, jnp.exp(log_gamma[:, None, None] * jnp.maximum(rel, 0.0)), 0.0)
    lane_gamma = jnp.tile(jnp.repeat(log_gamma, quarter), 2)
    dq = jnp.exp((idx + 1.0)[:, None] * lane_gamma[None, :])
    dkt = jnp.exp((BLK - 1 - idx)[:, None] * lane_gamma[None, :])
    cdec = jnp.repeat(jnp.exp(log_gamma * BLK), dv)[None, :]
    return jnp.cos(ang), jnp.sin(ang), dq, dkt, decay, cdec


def _retention(q, k, v, g, gn_g):
    b, s, wk = q.shape
    wv = v.shape[-1]
    cos, sin, dq, dkt, decay, cdec = _retention_tables(s)
    tok = lambda w: pl.BlockSpec((b, BLK, w), lambda c: (0, c, 0))
    const = lambda shape: pl.BlockSpec(shape, lambda c: (0,) * len(shape))
    return pl.pallas_call(
        _retention_kernel,
        grid=(s // BLK,),
        in_specs=[tok(wk), tok(wk), tok(wv), tok(wv),
                  pl.BlockSpec((BLK, wk // 2), lambda c: (c, 0)),
                  pl.BlockSpec((BLK, wk // 2), lambda c: (c, 0)),
                  const((BLK, wk)), const((BLK, wk)), const((RET_HEADS, BLK, BLK)),
                  const((1, wv)), const((1, wv))],
        out_specs=tok(wv),
        out_shape=jax.ShapeDtypeStruct((b, s, wv), BF16),
        scratch_shapes=[pltpu.VMEM((b, wk, wv), F32)],
        compiler_params=_params(("arbitrary",), 32),
        name="retention",
    )(q, k, v, g, cos, sin, dq, dkt, decay, cdec, gn_g.reshape(1, wv))


def _diff_attn_kernel(lam_ref, g_ref, q_ref, k_ref, v_ref, o_ref, s_ref, m_ref, vext_ref, acc_ref, *,
                      tq, lam_init):
    d = DIFF_HEAD_DIM
    hw = 2 * d
    nl = tq // LANES
    qi = pl.program_id(2)
    q = q_ref[0]
    qs = (q[:, :d], q[:, d:])

    @pl.when(qi == 0)
    def _():
        vext_ref[:, :hw] = v_ref[0]
        vext_ref[:, hw:] = jnp.ones((vext_ref.shape[0], hw), BF16)

    def fold_max(x):
        r = x[:, :LANES]
        for c in range(1, nl):
            r = jnp.maximum(r, x[:, c * LANES:(c + 1) * LANES])
        return r

    m_ref[...] = jnp.full(m_ref.shape, NEG, F32)

    def scores(j, diagonal):
        k = k_ref[0, pl.ds(pl.multiple_of(j * tq, tq), tq), :]
        for c in range(2):
            s = _dot_nt(qs[c], k[:, c * d:(c + 1) * d]) * LOG2E
            if diagonal:
                row = lax.broadcasted_iota(jnp.int32, (tq, tq), 0)
                col = lax.broadcasted_iota(jnp.int32, (tq, tq), 1)
                s = jnp.where(col <= row, s, NEG)
            s_ref[c, j] = s
            m_ref[c] = jnp.maximum(m_ref[c], fold_max(s))

    def for_each_block(n, body):
        def pair(jj, carry):
            body(2 * jj)
            body(2 * jj + 1)
            return carry

        lax.fori_loop(0, lax.shift_right_logical(n, 1), pair, 0)

        @pl.when((n & 1) == 1)
        def _():
            body(n - 1)

    for_each_block(qi, lambda j: scores(j, False))
    scores(qi, True)

    for c in range(2):
        m_ref[c] = jnp.broadcast_to(jnp.max(m_ref[c], axis=-1, keepdims=True), (tq, LANES))
    acc_ref[...] = jnp.zeros(acc_ref.shape, F32)

    def accumulate(j):
        v = vext_ref[pl.ds(pl.multiple_of(j * tq, tq), tq), :]
        for c in range(2):
            m = m_ref[c]
            s = s_ref[c, j]
            p = jnp.concatenate([jnp.exp2(s[:, t * LANES:(t + 1) * LANES] - m) for t in range(nl)], axis=-1)
            acc_ref[c] += _dot(p.astype(BF16), v)

    for_each_block(qi + 1, accumulate)

    lp = lam_ref[...]
    lam = (jnp.exp(jnp.sum(lp[0:1] * lp[1:2], keepdims=True))
           - jnp.exp(jnp.sum(lp[2:3] * lp[3:4], keepdims=True)) + lam_init)
    o1 = acc_ref[0, :, :hw] / acc_ref[0, :, hw:hw + 1]
    o2 = acc_ref[1, :, :hw] / acc_ref[1, :, hw:hw + 1]
    o_ref[0] = (_rms(o1 - lam * o2, g_ref[...]) * (1.0 - lam_init)).astype(o_ref.dtype)


def _diff_attention(q, k, v, lam_params, subln_g, lam_init, tq=512):
    b, s, w = q.shape
    hw = 2 * DIFF_HEAD_DIM
    return pl.pallas_call(
        functools.partial(_diff_attn_kernel, tq=tq, lam_init=lam_init),
        grid=(b, DIFF_HEADS, s // tq),
        in_specs=[pl.BlockSpec(lam_params.shape, lambda bi, h, i: (0, 0)),
                  pl.BlockSpec((1, hw), lambda bi, h, i: (0, 0)),
                  pl.BlockSpec((1, tq, hw), lambda bi, h, i: (bi, i, h)),
                  pl.BlockSpec((1, s, hw), lambda bi, h, i: (bi, 0, h)),
                  pl.BlockSpec((1, s, hw), lambda bi, h, i: (bi, 0, h))],
        out_specs=pl.BlockSpec((1, tq, hw), lambda bi, h, i: (bi, i, h)),
        out_shape=jax.ShapeDtypeStruct((b, s, w), BF16),
        scratch_shapes=[pltpu.VMEM((2, s // tq, tq, tq), F32), pltpu.VMEM((2, tq, LANES), F32),
                        pltpu.VMEM((s, 2 * hw), BF16), pltpu.VMEM((2, tq, 2 * hw), F32)],
        compiler_params=_params(("parallel", "parallel", "arbitrary"), 40),
        name="diff_attention",
    )(lam_params, subln_g.reshape(1, hw), q, k, v)


def _ssd_kernel(z_ref, xbc_ref, dt_ref, cw_ref, cb_ref, dtb_ref, alog_ref, dskip_ref, ng_ref,
                tri_ref, exp_ref, o_ref, carry_ref, ext_ref, state_ref):
    gn = SSM_GROUPS * SSM_STATE
    hpg = SSM_HEADS // SSM_GROUPS
    gw = hpg * SSM_HEAD_DIM
    tail = 8

    @pl.when(pl.program_id(0) == 0)
    def _():
        carry_ref[...] = jnp.zeros_like(carry_ref)
        state_ref[...] = jnp.zeros_like(state_ref)

    cw = cw_ref[...]
    lane = lax.broadcasted_iota(jnp.int32, (1, LANES), 1)
    a_neg = jnp.where(lane < SSM_HEADS, -jnp.exp(alog_ref[...]), 0.0)
    expand = exp_ref[...]
    row = lax.broadcasted_iota(jnp.int32, (BLK, BLK), 0)
    col = lax.broadcasted_iota(jnp.int32, (BLK, BLK), 1)
    col_head = lax.broadcasted_iota(jnp.int32, (1, gw), 1) // SSM_HEAD_DIM
    for bi in range(z_ref.shape[0]):
        xbc = xbc_ref[bi]
        ext_ref[bi, 0:tail, :] = carry_ref[bi]
        ext_ref[bi, tail:tail + BLK, :] = xbc
        carry_ref[bi] = xbc[BLK - tail:, :]
        conv = cb_ref[...] + cw[SSM_CONV - 1:SSM_CONV, :] * xbc
        for sft in range(1, SSM_CONV):
            conv = conv + (cw[SSM_CONV - 1 - sft:SSM_CONV - sft, :]
                           * ext_ref[bi, tail - sft:tail - sft + BLK, :])
        xc = _silu(conv)
        xs, bm, cm = xc[:, :SSM_D_INNER], xc[:, SSM_D_INNER:SSM_D_INNER + gn], xc[:, SSM_D_INNER + gn:]

        pre = dt_ref[bi] + dtb_ref[...]
        dt = jnp.maximum(pre, 0.0) + jnp.log1p(jnp.exp(-jnp.abs(pre)))
        a_col = _sel_dot(tri_ref[...], dt * a_neg)
        a_row = a_col.T
        a_exp = _dot_sel(a_col, expand)
        dt_exp = _dot_sel(dt, expand)
        total = a_exp[BLK - 1:BLK, :]
        xdt = xs * dt_exp
        x_end = (xdt * jnp.exp(total - a_exp)).astype(BF16)
        xdt_b = xdt.astype(BF16)
        ys = []
        for g in range(SSM_GROUPS):
            b_g = bm[:, g * SSM_STATE:(g + 1) * SSM_STATE]
            c_g = cm[:, g * SSM_STATE:(g + 1) * SSM_STATE].astype(BF16)
            cb = _dot_nt(c_g, b_g.astype(BF16))
            xg = xdt_b[:, g * gw:(g + 1) * gw]
            y_g = jnp.zeros((BLK, gw), F32)
            for hh in range(hpg):
                h = g * hpg + hh
                seg = a_col[:, h:h + 1] - a_row[h:h + 1, :]
                m_h = (cb * jnp.exp(jnp.where(col <= row, seg, NEG))).astype(BF16)
                y_g = y_g + _dot(m_h, jnp.where(col_head == hh, xg, jnp.zeros_like(xg)))
            state = state_ref[bi, g]
            y_g = y_g + _dot(c_g, state.astype(BF16)) * jnp.exp(a_exp[:, g * gw:(g + 1) * gw])
            new = _dot(b_g.T.astype(BF16), x_end[:, g * gw:(g + 1) * gw])
            state_ref[bi, g] = state * jnp.exp(total[:, g * gw:(g + 1) * gw]) + new
            ys.append(y_g)
        y = jnp.concatenate(ys, axis=-1) + xs * dskip_ref[...]
        y = y * _silu(z_ref[bi].astype(F32))
        o_ref[bi] = _rms(y, ng_ref[...]).astype(o_ref.dtype)


def _ssd(z, xbc, dt, conv_w, conv_b, dt_bias, a_log, d_skip, norm_g):
    b, s, c = xbc.shape
    di = SSM_D_INNER
    pad = lambda t: jnp.pad(t.astype(F32), (0, LANES - t.shape[0])).reshape(1, LANES)
    tri = jnp.tril(jnp.ones((BLK, BLK), BF16))
    expand = (jnp.arange(LANES)[:, None] == (jnp.arange(di) // SSM_HEAD_DIM)[None, :]).astype(BF16)
    tok = lambda w: pl.BlockSpec((b, BLK, w), lambda ci: (0, ci, 0))
    const = lambda shape: pl.BlockSpec(shape, lambda ci: (0,) * len(shape))
    return pl.pallas_call(
        _ssd_kernel,
        grid=(s // BLK,),
        in_specs=[tok(di), tok(c), tok(LANES), const((SSM_CONV, c)), const((1, c)),
                  const((1, LANES)), const((1, LANES)), const((1, di)), const((1, di)),
                  const((BLK, BLK)), const((LANES, di))],
        out_specs=tok(di),
        out_shape=jax.ShapeDtypeStruct((b, s, di), BF16),
        scratch_shapes=[pltpu.VMEM((b, 8, c), F32), pltpu.VMEM((b, 8 + BLK, c), F32),
                        pltpu.VMEM((b, SSM_GROUPS, SSM_STATE, di // SSM_GROUPS), F32)],
        compiler_params=_params(("arbitrary",), 32),
        name="ssd",
    )(z, xbc, dt, conv_w, conv_b.reshape(1, c), pad(dt_bias), pad(a_log),
      jnp.repeat(d_skip.astype(F32), SSM_HEAD_DIM).reshape(1, di), norm_g.reshape(1, di), tri, expand)


def _from_residue_view(src_ref, slab_ref, dil, width):
    if dil == 1:
        return src_ref[...].astype(F32)
    rows = src_ref.shape[0]
    tiles = width // LANES
    for r in range(dil):
        for t in range(tiles):
            col = r * width + t * LANES
            slab_ref[t, pl.ds(r, rows, stride=dil), :] = src_ref[:, col:col + LANES].astype(F32)
    return jnp.concatenate([slab_ref[t] for t in range(tiles)], axis=-1)


def _merge_kernel(x_ref, *refs, dils):
    n = len(dils)
    o_refs, l_refs = refs[:n], refs[n:2 * n]
    yb_ref, yc_ref, yd_ref, gate_ref, wb_ref, wo_ref, hexp_ref, out_ref = refs[2 * n:2 * n + 8]
    o_slabs, l_slabs = refs[2 * n + 8:3 * n + 8], refs[3 * n + 8:]
    d = x_ref.shape[-1]
    bw = yb_ref.shape[-1]
    lses = [_from_residue_view(l_refs[i], l_slabs[i], dils[i], LANES) for i in range(n)]
    m = functools.reduce(jnp.maximum, lses)
    es = [jnp.exp(l - m) for l in lses]
    inv = 1.0 / functools.reduce(jnp.add, es)
    ya = jnp.zeros((x_ref.shape[0], bw), F32)
    for i in range(n):
        wt = _dot((es[i] * inv).astype(BF16), hexp_ref[...])
        ya = ya + wt * _from_residue_view(o_refs[i], o_slabs[i], dils[i], bw)
    branches = (ya.astype(BF16), yb_ref[...], yc_ref[...], yd_ref[...])
    merged = jnp.zeros((x_ref.shape[0], d), F32)
    for i, y in enumerate(branches):
        gate = _sigmoid(gate_ref[:, i * d:(i + 1) * d].astype(F32))
        merged = merged + gate * _dot(y, wb_ref[i])
    out_ref[...] = x_ref[...] + _dot(merged.astype(BF16), wo_ref[...])


def _merge(x2d, o_parts, lse_parts, dils, yb, yc, yd, gates, w_branch, w_out, tm=256):
    t, d = x2d.shape
    bw = yb.shape[-1]
    row = lambda w: pl.BlockSpec((tm, w), lambda i: (i, 0))
    view = lambda w, dil: pl.BlockSpec((tm // dil, dil * w), lambda i: (i, 0))
    per_head = LANES // DIL_HEADS
    hexp = (jnp.arange(LANES)[:, None]
            == (jnp.arange(bw) // DIL_HEAD_DIM * per_head)[None, :]).astype(BF16)
    return pl.pallas_call(
        functools.partial(_merge_kernel, dils=tuple(dils)),
        grid=(t // tm,),
        in_specs=[row(d)] + [view(bw, dil) for dil in dils] + [view(LANES, dil) for dil in dils]
                 + [row(bw)] * 3 + [row(N_BRANCH * d),
                                    pl.BlockSpec(w_branch.shape, lambda i: (0, 0, 0)),
                                    pl.BlockSpec(w_out.shape, lambda i: (0, 0)),
                                    pl.BlockSpec(hexp.shape, lambda i: (0, 0))],
        out_specs=row(d),
        out_shape=jax.ShapeDtypeStruct((t, d), F32),
        scratch_shapes=[pltpu.VMEM((bw // LANES, tm, LANES), F32) for _ in dils]
                       + [pltpu.VMEM((1, tm, LANES), F32) for _ in dils],
        compiler_params=_params(("parallel",), 48),
        name="merge",
    )(x2d, *o_parts, *lse_parts, yb, yc, yd, gates, w_branch, w_out, hexp)


def _xattn_kernel(x_ref, g_ref, wq_ref, k_ref, v_ref, wo_ref, out_ref):
    x = x_ref[0]
    q = _dot(_rms(x, g_ref[...]).astype(BF16), wq_ref[...]).astype(BF16)
    k, v = k_ref[0], v_ref[0]
    outs = []
    for h in range(X_HEADS):
        sl = slice(h * X_HEAD_DIM, (h + 1) * X_HEAD_DIM)
        s = _dot_nt(q[:, sl], k[:, sl]) * (X_HEAD_DIM ** -0.5)
        p = jnp.exp(s - jnp.max(s, axis=-1, keepdims=True))
        a = p / jnp.sum(p, axis=-1, keepdims=True)
        outs.append(_dot(a.astype(BF16), v[:, sl]))
    o = jnp.concatenate(outs, axis=-1).astype(BF16)
    out_ref[0] = x + _dot(o, wo_ref[...])


def _cross_attention(x, g, w_q, k, v, w_o, tm=512):
    b, s, d = x.shape
    m, w = k.shape[1], k.shape[2]
    return pl.pallas_call(
        _xattn_kernel,
        grid=(b, s // tm),
        in_specs=[pl.BlockSpec((1, tm, d), lambda bi, i: (bi, i, 0)),
                  pl.BlockSpec((1, d), lambda bi, i: (0, 0)),
                  pl.BlockSpec(w_q.shape, lambda bi, i: (0, 0)),
                  pl.BlockSpec((1, m, w), lambda bi, i: (bi, 0, 0)),
                  pl.BlockSpec((1, m, w), lambda bi, i: (bi, 0, 0)),
                  pl.BlockSpec(w_o.shape, lambda bi, i: (0, 0))],
        out_specs=pl.BlockSpec((1, tm, d), lambda bi, i: (bi, i, 0)),
        out_shape=jax.ShapeDtypeStruct((b, s, d), F32),
        compiler_params=_params(("parallel", "parallel"), 48),
        name="cross_attention",
    )(x, g.reshape(1, d), w_q, k, v, w_o)


def _ffn_kernel(x_ref, g_ref, wa_ref, wb_ref, wo_ref, fg_ref, out_ref, *, final_norm):
    x = x_ref[...]
    h = _rms(x, g_ref[...]).astype(BF16)
    a = _dot(h, wa_ref[...])
    b = _dot(h, wb_ref[...])
    y = x + _dot((_silu(a) * b).astype(BF16), wo_ref[...])
    if final_norm:
        y = _rms(y, fg_ref[...])
    out_ref[...] = y


def _ffn(x2d, g, w_a, w_b, w_o, final_g, final_norm, tm=256):
    t, d = x2d.shape
    full = lambda w: pl.BlockSpec(w.shape, lambda i: (0, 0))
    return pl.pallas_call(
        functools.partial(_ffn_kernel, final_norm=final_norm),
        grid=(t // tm,),
        in_specs=[pl.BlockSpec((tm, d), lambda i: (i, 0)), pl.BlockSpec((1, d), lambda i: (0, 0)),
                  full(w_a), full(w_b), full(w_o), pl.BlockSpec((1, d), lambda i: (0, 0))],
        out_specs=pl.BlockSpec((tm, d), lambda i: (i, 0)),
        out_shape=jax.ShapeDtypeStruct((t, d), F32),
        compiler_params=_params(("parallel",), 56),
        name="ffn",
    )(x2d, g.reshape(1, d), w_a, w_b, w_o, final_g.reshape(1, d))


def _ret_feature_order():
    quarter = RET_KEY_DIM // 2
    first = [h * RET_KEY_DIM + f for h in range(RET_HEADS) for f in range(quarter)]
    return np.array(first + [c + quarter for c in first])


def _layer(x, mem, p, lam_init, final_g, final_norm):
    b, s, d = x.shape
    t = b * s
    x2d = x.reshape(t, d)
    a_w = DIL_HEADS * DIL_HEAD_DIM
    b_qk, b_vg = RET_HEADS * RET_KEY_DIM, RET_HEADS * RET_VAL_DIM
    c_w = DIFF_HEADS * 2 * DIFF_HEAD_DIM
    d_xbc = SSM_D_INNER + 2 * SSM_GROUPS * SSM_STATE
    sizes = (a_w, a_w, a_w, b_qk, b_qk, b_vg, b_vg, c_w, c_w, c_w, SSM_D_INNER, d_xbc, SSM_HEADS,
             N_BRANCH * d)
    offs = np.concatenate([[0], np.cumsum(sizes)])
    w_in = p['w_in']
    col = lambda i: w_in[:, offs[i]:offs[i + 1]]
    order = _ret_feature_order()
    w_dt = jnp.pad(col(12), ((0, 0), (0, LANES - SSM_HEADS)))
    bf = lambda w: w.astype(BF16)

    dils = [dil for _, dil in DIL_PATTERNS]
    nd = len(dils)
    proj = _proj_views(x2d, p['norm_mix_g'], [bf(col(i)) for i in (0, 1, 2)],
                       [DIL_HEAD_DIM ** -0.5, 1.0, 1.0], [bf(col(i)) for i in (7, 8, 9)],
                       [DIFF_HEAD_DIM ** -0.5, 1.0, 1.0], dils)
    aq, ak, av = proj[0:nd], proj[nd:2 * nd], proj[2 * nd:3 * nd]
    cq, ck, cv = proj[3 * nd:]
    bq, bk, bv, bg, dz, dxbc, ddt = _norm_matmul(
        x2d, p['norm_mix_g'],
        [bf(col(3)[:, order]), bf(col(4)[:, order]), bf(col(5)), bf(col(6)), bf(col(10)), bf(col(11)),
         bf(w_dt)],
        [F32, F32, BF16, BF16, BF16, F32, F32], [1.0] * 7, tm=512)
    (gates,) = _norm_matmul(x2d, p['norm_mix_g'], [bf(col(13))], [BF16], [1.0], tm=512)

    r3 = lambda y: y.reshape(b, s, y.shape[-1])
    o_parts, lse_parts = _band_attention(aq, ak, av, b, s)
    y_b = _retention(r3(bq), r3(bk), r3(bv), r3(bg), p['ret_gn_g'])
    y_c = _diff_attention(r3(cq), r3(ck), r3(cv), p['diff_lambda'], p['diff_subln_g'], lam_init)
    y_d = _ssd(r3(dz), r3(dxbc), r3(ddt), p['ssm_conv_w'], p['ssm_conv_b'], p['ssm_dt_bias'],
               p['ssm_A_log'], p['ssm_D'], p['ssm_norm_g'])
    x2d = _merge(x2d, o_parts, lse_parts, dils, y_b.reshape(t, -1), y_c.reshape(t, -1),
                 y_d.reshape(t, -1), gates, bf(p['w_branch']), bf(p['w_mix_out']))

    xw = X_HEADS * X_HEAD_DIM
    m2d = mem.reshape(-1, d)
    mk, mv = _norm_matmul(m2d, p['norm_mem_g'], [bf(p['w_xkv'][:, :xw]), bf(p['w_xkv'][:, xw:])],
                          [BF16, BF16], [1.0, 1.0], tm=256)
    mlen = mem.shape[1]
    x3 = _cross_attention(x2d.reshape(b, s, d), p['norm_x_g'], bf(p['w_xq']),
                          mk.reshape(b, mlen, xw), mv.reshape(b, mlen, xw), bf(p['w_xo']))
    hid = p['w_ffn_out'].shape[0]
    x2d = _ffn(x3.reshape(t, d), p['norm_ffn_g'], bf(p['w_ffn_in'][:, :hid]), bf(p['w_ffn_in'][:, hid:]),
               bf(p['w_ffn_out']), final_g, final_norm)
    return x2d.reshape(b, s, d)


def kernel(x, mem, norm_mix_g, w_in, ret_gn_g, diff_lambda, diff_subln_g, ssm_conv_w, ssm_conv_b,
           ssm_dt_bias, ssm_A_log, ssm_D, ssm_norm_g, w_branch, w_mix_out, norm_x_g, norm_mem_g, w_xq,
           w_xkv, w_xo, norm_ffn_g, w_ffn_in, w_ffn_out, norm_f_g):
    stacked = dict(norm_mix_g=norm_mix_g, w_in=w_in, ret_gn_g=ret_gn_g, diff_lambda=diff_lambda,
                   diff_subln_g=diff_subln_g, ssm_conv_w=ssm_conv_w, ssm_conv_b=ssm_conv_b,
                   ssm_dt_bias=ssm_dt_bias, ssm_A_log=ssm_A_log, ssm_D=ssm_D, ssm_norm_g=ssm_norm_g,
                   w_branch=w_branch, w_mix_out=w_mix_out, norm_x_g=norm_x_g, norm_mem_g=norm_mem_g,
                   w_xq=w_xq, w_xkv=w_xkv, w_xo=w_xo, norm_ffn_g=norm_ffn_g, w_ffn_in=w_ffn_in,
                   w_ffn_out=w_ffn_out)
    depth = w_in.shape[0]
    for l in range(depth):
        lam_init = 0.8 - 0.6 * math.exp(-0.3 * l)
        p = {name: val[l] for name, val in stacked.items()}
        x = _layer(x, mem, p, lam_init, norm_f_g, final_norm=(l == depth - 1))
    return x
```

```python
import functools
import math

import jax
import jax.numpy as jnp
import numpy as np
from jax import lax
from jax.experimental import pallas as pl
from jax.experimental.pallas import tpu as pltpu

F32 = jnp.float32
BF16 = jnp.bfloat16

EPS = 1e-6
NEG = -1e30
LOG2E = math.log2(math.e)
BLK = 128

DIL_HEADS = 8
DIL_HEAD_DIM = 64
DIL_PATTERNS = ((128, 1), (512, 4), (2048, 16))
RET_HEADS = 4
RET_KEY_DIM = 64
RET_VAL_DIM = 128
ROPE_BASE = 10000.0
DIFF_HEADS = 4
DIFF_HEAD_DIM = 64
SSM_D_INNER = 512
SSM_HEAD_DIM = 64
SSM_HEADS = 8
SSM_GROUPS = 2
SSM_STATE = 128
SSM_CONV = 4
N_BRANCH = 4
BRANCH_WIDTH = 512
X_HEADS = 4
X_HEAD_DIM = 128
LANES = 128

MIB = 1024 * 1024


def _params(semantics, vmem_mib):
    return pltpu.CompilerParams(dimension_semantics=semantics,
                                vmem_limit_bytes=vmem_mib * MIB)


def _dot(a, b):
    return jnp.dot(a, b, preferred_element_type=F32)


def _dot_nt(a, b):
    return lax.dot_general(a, b, (((1,), (1,)), ((), ())), preferred_element_type=F32)


def _split3(x):
    x1 = x.astype(BF16)
    r1 = x - x1.astype(F32)
    x2 = r1.astype(BF16)
    x3 = (r1 - x2.astype(F32)).astype(BF16)
    return x1, x2, x3


def _dot_sel(x, sel):
    x1, x2, x3 = _split3(x)
    return _dot(x1, sel) + _dot(x2, sel) + _dot(x3, sel)


def _sel_dot(sel, x):
    x1, x2, x3 = _split3(x)
    return _dot(sel, x1) + _dot(sel, x2) + _dot(sel, x3)


def _sigmoid(x):
    return 1.0 / (1.0 + jnp.exp(-x))


def _silu(x):
    return x * _sigmoid(x)


def _rms(x, g):
    ms = jnp.mean(x * x, axis=-1, keepdims=True)
    return x * lax.rsqrt(ms + EPS) * g


def _norm_matmul_kernel(x_ref, g_ref, *refs, n_out, scales):
    w_refs, o_refs = refs[:n_out], refs[n_out:]
    h = _rms(x_ref[...], g_ref[...]).astype(BF16)
    for w_ref, o_ref, sc in zip(w_refs, o_refs, scales):
        y = _dot(h, w_ref[...])
        if sc != 1.0:
            y = y * sc
        o_ref[...] = y.astype(o_ref.dtype)


def _norm_matmul(x2d, g, ws, out_dtypes, scales, tm, vmem_mib=48):
    t, d = x2d.shape
    in_specs = [pl.BlockSpec((tm, d), lambda i: (i, 0)),
                pl.BlockSpec((1, d), lambda i: (0, 0))]
    in_specs += [pl.BlockSpec(w.shape, lambda i: (0, 0)) for w in ws]
    out_specs = [pl.BlockSpec((tm, w.shape[1]), lambda i: (i, 0)) for w in ws]
    out_shape = [jax.ShapeDtypeStruct((t, w.shape[1]), dt) for w, dt in zip(ws, out_dtypes)]
    return pl.pallas_call(
        functools.partial(_norm_matmul_kernel, n_out=len(ws), scales=tuple(scales)),
        grid=(t // tm,), in_specs=in_specs, out_specs=out_specs, out_shape=out_shape,
        compiler_params=_params(("parallel",), vmem_mib),
        name="norm_matmul",
    )(x2d, g.reshape(1, d), *ws)


def _proj_views_kernel(x_ref, g_ref, *refs, n_view, n_plain, dils, view_scales, plain_scales):
    nd = len(dils)
    n_w = n_view + n_plain
    w_refs = refs[:n_w]
    view_refs = refs[n_w:n_w + n_view * nd]
    plain_refs = refs[n_w + n_view * nd:n_w + n_view * nd + n_plain]
    slab_ref = refs[-1]
    tm = x_ref.shape[0]
    h = _rms(x_ref[...], g_ref[...]).astype(BF16)
    for i in range(n_view):
        y = _dot(h, w_refs[i][...])
        if view_scales[i] != 1.0:
            y = y * view_scales[i]
        width = y.shape[1]
        tiles = width // LANES
        for t in range(tiles):
            slab_ref[t] = y[:, t * LANES:(t + 1) * LANES]
        for di, dil in enumerate(dils):
            o_ref = view_refs[i * nd + di]
            if dil == 1:
                o_ref[...] = y.astype(o_ref.dtype)
                continue
            for r in range(dil):
                for t in range(tiles):
                    col = r * width + t * LANES
                    o_ref[:, col:col + LANES] = slab_ref[t, pl.ds(r, tm // dil, stride=dil), :].astype(o_ref.dtype)
    for i in range(n_plain):
        y = _dot(h, w_refs[n_view + i][...])
        if plain_scales[i] != 1.0:
            y = y * plain_scales[i]
        plain_refs[i][...] = y.astype(plain_refs[i].dtype)


def _proj_views(x2d, g, view_ws, view_scales, plain_ws, plain_scales, dils, tm=512):
    t, d = x2d.shape
    ws = list(view_ws) + list(plain_ws)
    full = lambda w: pl.BlockSpec(w.shape, lambda i: (0, 0))
    out_specs, out_shape = [], []
    for w in view_ws:
        n = w.shape[1]
        for dil in dils:
            out_specs.append(pl.BlockSpec((tm // dil, dil * n), lambda i: (i, 0)))
            out_shape.append(jax.ShapeDtypeStruct((t // dil, dil * n), BF16))
    for w in plain_ws:
        out_specs.append(pl.BlockSpec((tm, w.shape[1]), lambda i: (i, 0)))
        out_shape.append(jax.ShapeDtypeStruct((t, w.shape[1]), BF16))
    width = view_ws[0].shape[1]
    return pl.pallas_call(
        functools.partial(_proj_views_kernel, n_view=len(view_ws), n_plain=len(plain_ws), dils=tuple(dils),
                          view_scales=tuple(view_scales), plain_scales=tuple(plain_scales)),
        grid=(t // tm,),
        in_specs=[pl.BlockSpec((tm, d), lambda i: (i, 0)), pl.BlockSpec((1, d), lambda i: (0, 0))]
                 + [full(w) for w in ws],
        out_specs=out_specs, out_shape=out_shape,
        scratch_shapes=[pltpu.VMEM((width // LANES, tm, LANES), F32)],
        compiler_params=_params(("parallel",), 48),
        name="proj_views",
    )(x2d, g.reshape(1, d), *ws)


def _band_attn_kernel(*refs, n_pat):
    ins, outs = refs[:6 * n_pat], refs[6 * n_pat:]
    for i in range(n_pat):
        _band_block(*ins[6 * i:6 * i + 6], *outs[2 * i:2 * i + 2])


def _band_block(bias_ref, q_ref, kp_ref, kc_ref, vp_ref, vc_ref, o_ref, lse_ref):
    hd = DIL_HEAD_DIM
    bias = bias_ref[0]
    lane = lax.broadcasted_iota(jnp.int32, (1, LANES), 1)
    ones = jnp.ones((2 * BLK, LANES), BF16)
    lse_tile = jnp.zeros((BLK, LANES), F32)
    for t in range(DIL_HEADS * hd // LANES):
        sl = slice(t * LANES, (t + 1) * LANES)
        q = q_ref[0, :, sl]
        k2 = jnp.concatenate([kp_ref[0, :, sl], kc_ref[0, :, sl]], axis=0)
        v2 = jnp.concatenate([vp_ref[0, :, sl], vc_ref[0, :, sl]], axis=0)
        v_ext = jnp.concatenate([v2, ones], axis=-1)
        pair = []
        for half in range(LANES // hd):
            in_head = (lane // hd) == half
            s = _dot_nt(jnp.where(in_head, q, jnp.zeros_like(q)), k2) + bias
            m = jnp.max(jnp.maximum(s[:, :BLK], s[:, BLK:]), axis=-1, keepdims=True)
            p = jnp.exp(s - m).astype(BF16)
            pv = _dot(p, v_ext)
            den = pv[:, LANES:]
            pair.append(pv[:, :LANES] / den)
            h = t * (LANES // hd) + half
            per_head = LANES // DIL_HEADS
            lse_tile = jnp.where((lane // per_head) == h, m + jnp.log(den), lse_tile)
        o_ref[0, :, sl] = jnp.where((lane // hd) == 0, pair[0], pair[1]).astype(o_ref.dtype)
    lse_ref[0] = lse_tile


def _band_bias(window):
    i = jnp.arange(BLK)[:, None]
    j = jnp.arange(2 * BLK)[None, :]
    dist = BLK + i - j
    band = (dist >= 0) & (dist <= window)
    first = band & (j >= BLK)
    return jnp.where(jnp.stack([first, band]), 0.0, NEG).astype(F32)


def _band_attention(qs, ks, vs, batch, seq):
    w = DIL_HEADS * DIL_HEAD_DIM
    steps = seq // BLK
    args, in_specs, out_specs, out_shape = [], [], [], []
    for (window, dil), q, k, v in zip(DIL_PATTERNS, qs, ks, vs):
        l = seq // dil
        nb = l // BLK
        view = lambda t, l=l, dil=dil: t.reshape(batch, l, dil * w)
        cur = lambda wd, nb=nb: pl.BlockSpec((1, BLK, wd), lambda bi, n: (bi, n % nb, n // nb))
        prev = pl.BlockSpec((1, BLK, w), lambda bi, n, nb=nb: (bi, jnp.maximum(n % nb - 1, 0), n // nb))
        bias = pl.BlockSpec((1, BLK, 2 * BLK), lambda bi, n, nb=nb: (jnp.minimum(n % nb, 1), 0, 0))
        args += [_band_bias(window // dil), view(q), view(k), view(k), view(v), view(v)]
        in_specs += [bias, cur(w), prev, cur(w), prev, cur(w)]
        out_specs += [cur(w), cur(LANES)]
        out_shape += [jax.ShapeDtypeStruct((batch, l, dil * w), BF16),
                      jax.ShapeDtypeStruct((batch, l, dil * LANES), F32)]
    outs = pl.pallas_call(
        functools.partial(_band_attn_kernel, n_pat=len(DIL_PATTERNS)),
        grid=(batch, steps),
        in_specs=in_specs, out_specs=out_specs, out_shape=out_shape,
        compiler_params=_params(("parallel", "parallel"), 32),
        name="band_attention",
    )(*args)
    flat = [o.reshape(-1, o.shape[-1]) for o in outs]
    return flat[0::2], flat[1::2]


def _retention_kernel(q_ref, k_ref, v_ref, g_ref, cos_ref, sin_ref, dq_ref, dk_ref, dec_ref,
                      cdec_ref, gn_ref, o_ref, state_ref):
    @pl.when(pl.program_id(0) == 0)
    def _():
        state_ref[...] = jnp.zeros_like(state_ref)

    cos, sin = cos_ref[...], sin_ref[...]
    half = RET_HEADS * RET_KEY_DIM // 2

    def rot(t):
        t1, t2 = t[:, :half], t[:, half:]
        return jnp.concatenate([t1 * cos - t2 * sin, t1 * sin + t2 * cos], axis=-1)

    lane_head = (lax.broadcasted_iota(jnp.int32, (1, 2 * half), 1) % half) // (RET_KEY_DIM // 2)
    for bi in range(q_ref.shape[0]):
        rq = rot(q_ref[bi])
        rk = rot(k_ref[bi]) * (RET_KEY_DIM ** -0.5)
        v = v_ref[bi]
        rkb = rk.astype(BF16)
        state = state_ref[bi]
        o_cross = _dot((rq * dq_ref[...]).astype(BF16), state.astype(BF16))
        outs = []
        for h in range(RET_HEADS):
            qh = jnp.where(lane_head == h, rq, 0.0).astype(BF16)
            s = _dot_nt(qh, rkb) * dec_ref[h]
            outs.append(_dot(s.astype(BF16), v[:, h * RET_VAL_DIM:(h + 1) * RET_VAL_DIM]))
        o = jnp.concatenate(outs, axis=-1) + o_cross
        k_end_t = (rk * dk_ref[...]).T.astype(BF16)
        kv = _dot(k_end_t, v)
        row_head = (lax.broadcasted_iota(jnp.int32, kv.shape, 0) % half) // (RET_KEY_DIM // 2)
        col_head = lax.broadcasted_iota(jnp.int32, kv.shape, 1) // RET_VAL_DIM
        state_ref[bi] = state * cdec_ref[...] + jnp.where(row_head == col_head, kv, 0.0)
        normed = []
        for h in range(RET_HEADS):
            oh = o[:, h * RET_VAL_DIM:(h + 1) * RET_VAL_DIM]
            mu = jnp.mean(oh, axis=-1, keepdims=True)
            cen = oh - mu
            var = jnp.mean(cen * cen, axis=-1, keepdims=True)
            normed.append(cen * lax.rsqrt(var + EPS))
        y = jnp.concatenate(normed, axis=-1) * gn_ref[...]
        o_ref[bi] = (_silu(g_ref[bi].astype(F32)) * y).astype(o_ref.dtype)


def _retention_tables(s):
    h, dk, dv = RET_HEADS, RET_KEY_DIM, RET_VAL_DIM
    quarter = dk // 2
    pos = jnp.arange(s, dtype=F32)
    inv_freq = ROPE_BASE ** (-jnp.arange(quarter, dtype=F32) / quarter)
    ang = pos[:, None] * jnp.tile(inv_freq, h)[None, :]
    log_gamma = jnp.log1p(-jnp.exp2(-5.0 - jnp.arange(h, dtype=F32)))
    idx = jnp.arange(BLK, dtype=F32)
    rel = idx[:, None] - idx[None, :]
    decay = jnp.where(rel >= 0, jnp.exp(log_gamma[:, None, None] * jnp.maximum(rel, 0.0)), 0.0)
    lane_gamma = jnp.tile(jnp.repeat(log_gamma, quarter), 2)
    dq = jnp.exp((idx + 1.0)[:, None] * lane_gamma[None, :])
    dkt = jnp.exp((BLK - 1 - idx)[:, None] * lane_gamma[None, :])
    cdec = jnp.repeat(jnp.exp(log_gamma * BLK), dv)[None, :]
    return jnp.cos(ang), jnp.sin(ang), dq, dkt, decay, cdec


def _retention(q, k, v, g, gn_g):
    b, s, wk = q.shape
    wv = v.shape[-1]
    cos, sin, dq, dkt, decay, cdec = _retention_tables(s)
    tok = lambda w: pl.BlockSpec((b, BLK, w), lambda c: (0, c, 0))
    const = lambda shape: pl.BlockSpec(shape, lambda c: (0,) * len(shape))
    return pl.pallas_call(
        _retention_kernel,
        grid=(s // BLK,),
        in_specs=[tok(wk), tok(wk), tok(wv), tok(wv),
                  pl.BlockSpec((BLK, wk // 2), lambda c: (c, 0)),
                  pl.BlockSpec((BLK, wk // 2), lambda c: (c, 0)),
                  const((BLK, wk)), const((BLK, wk)), const((RET_HEADS, BLK, BLK)),
                  const((1, wv)), const((1, wv))],
        out_specs=tok(wv),
        out_shape=jax.ShapeDtypeStruct((b, s, wv), BF16),
        scratch_shapes=[pltpu.VMEM((b, wk, wv), F32)],
        compiler_params=_params(("arbitrary",), 32),
        name="retention",
    )(q, k, v, g, cos, sin, dq, dkt, decay, cdec, gn_g.reshape(1, wv))


def _diff_attn_kernel(lam_ref, g_ref, q_ref, k_ref, v_ref, o_ref, s_ref, m_ref, vext_ref, acc_ref, *,
                      tq, lam_init):
    d = DIFF_HEAD_DIM
    hw = 2 * d
    nl = tq // LANES
    qi = pl.program_id(2)
    q = q_ref[0]
    qs = (q[:, :d], q[:, d:])

    @pl.when(qi == 0)
    def _():
        vext_ref[:, :hw] = v_ref[0]
        vext_ref[:, hw:] = jnp.ones((vext_ref.shape[0], hw), BF16)

    def fold_max(x):
        r = x[:, :LANES]
        for c in range(1, nl):
            r = jnp.maximum(r, x[:, c * LANES:(c + 1) * LANES])
        return r

    m_ref[...] = jnp.full(m_ref.shape, NEG, F32)

    def scores(j, diagonal):
        k = k_ref[0, pl.ds(pl.multiple_of(j * tq, tq), tq), :]
        for c in range(2):
            s = _dot_nt(qs[c], k[:, c * d:(c + 1) * d]) * LOG2E
            if diagonal:
                row = lax.broadcasted_iota(jnp.int32, (tq, tq), 0)
                col = lax.broadcasted_iota(jnp.int32, (tq, tq), 1)
                s = jnp.where(col <= row, s, NEG)
            s_ref[c, j] = s
            m_ref[c] = jnp.maximum(m_ref[c], fold_max(s))

    def for_each_block(n, body):
        def quad(jj, carry):
            for u in range(4):
                body(4 * jj + u)
            return carry

        lax.fori_loop(0, lax.shift_right_logical(n, 2), quad, 0)
        done = n & ~3

        @pl.when((n & 2) == 2)
        def _():
            body(done)
            body(done + 1)

        @pl.when((n & 1) == 1)
        def _():
            body(n - 1)

    for_each_block(qi, lambda j: scores(j, False))
    scores(qi, True)

    for c in range(2):
        m_ref[c] = jnp.broadcast_to(jnp.max(m_ref[c], axis=-1, keepdims=True), (tq, LANES))
    acc_ref[...] = jnp.zeros(acc_ref.shape, F32)

    def accumulate(j):
        v = vext_ref[pl.ds(pl.multiple_of(j * tq, tq), tq), :]
        for c in range(2):
            m = m_ref[c]
            s = s_ref[c, j]
            p = jnp.concatenate([jnp.exp2(s[:, t * LANES:(t + 1) * LANES] - m) for t in range(nl)], axis=-1)
            acc_ref[c] += _dot(p.astype(BF16), v)

    for_each_block(qi + 1, accumulate)

    lp = lam_ref[...]
    lam = (jnp.exp(jnp.sum(lp[0:1] * lp[1:2], keepdims=True))
           - jnp.exp(jnp.sum(lp[2:3] * lp[3:4], keepdims=True)) + lam_init)
    o1 = acc_ref[0, :, :hw] / acc_ref[0, :, hw:hw + 1]
    o2 = acc_ref[1, :, :hw] / acc_ref[1, :, hw:hw + 1]
    o_ref[0] = (_rms(o1 - lam * o2, g_ref[...]) * (1.0 - lam_init)).astype(o_ref.dtype)


def _diff_attention(q, k, v, lam_params, subln_g, lam_init, tq=512):
    b, s, w = q.shape
    hw = 2 * DIFF_HEAD_DIM
    return pl.pallas_call(
        functools.partial(_diff_attn_kernel, tq=tq, lam_init=lam_init),
        grid=(b, DIFF_HEADS, s // tq),
        in_specs=[pl.BlockSpec(lam_params.shape, lambda bi, h, i: (0, 0)),
                  pl.BlockSpec((1, hw), lambda bi, h, i: (0, 0)),
                  pl.BlockSpec((1, tq, hw), lambda bi, h, i: (bi, i, h)),
                  pl.BlockSpec((1, s, hw), lambda bi, h, i: (bi, 0, h)),
                  pl.BlockSpec((1, s, hw), lambda bi, h, i: (bi, 0, h))],
        out_specs=pl.BlockSpec((1, tq, hw), lambda bi, h, i: (bi, i, h)),
        out_shape=jax.ShapeDtypeStruct((b, s, w), BF16),
        scratch_shapes=[pltpu.VMEM((2, s // tq, tq, tq), F32), pltpu.VMEM((2, tq, LANES), F32),
                        pltpu.VMEM((s, 2 * hw), BF16), pltpu.VMEM((2, tq, 2 * hw), F32)],
        compiler_params=_params(("parallel", "parallel", "arbitrary"), 40),
        name="diff_attention",
    )(lam_params, subln_g.reshape(1, hw), q, k, v)


def _ssd_kernel(z_ref, xbc_ref, dt_ref, cw_ref, cb_ref, dtb_ref, alog_ref, dskip_ref, ng_ref,
                tri_ref, exp_ref, o_ref, carry_ref, ext_ref, state_ref):
    gn = SSM_GROUPS * SSM_STATE
    hpg = SSM_HEADS // SSM_GROUPS
    gw = hpg * SSM_HEAD_DIM
    tail = 8

    @pl.when(pl.program_id(0) == 0)
    def _():
        carry_ref[...] = jnp.zeros_like(carry_ref)
        state_ref[...] = jnp.zeros_like(state_ref)

    cw = cw_ref[...]
    lane = lax.broadcasted_iota(jnp.int32, (1, LANES), 1)
    a_neg = jnp.where(lane < SSM_HEADS, -jnp.exp(alog_ref[...]), 0.0)
    expand = exp_ref[...]
    row = lax.broadcasted_iota(jnp.int32, (BLK, BLK), 0)
    col = lax.broadcasted_iota(jnp.int32, (BLK, BLK), 1)
    col_head = lax.broadcasted_iota(jnp.int32, (1, gw), 1) // SSM_HEAD_DIM
    for bi in range(z_ref.shape[0]):
        xbc = xbc_ref[bi]
        ext_ref[bi, 0:tail, :] = carry_ref[bi]
        ext_ref[bi, tail:tail + BLK, :] = xbc
        carry_ref[bi] = xbc[BLK - tail:, :]
        conv = cb_ref[...] + cw[SSM_CONV - 1:SSM_CONV, :] * xbc
        for sft in range(1, SSM_CONV):
            conv = conv + (cw[SSM_CONV - 1 - sft:SSM_CONV - sft, :]
                           * ext_ref[bi, tail - sft:tail - sft + BLK, :])
        xc = _silu(conv)
        xs, bm, cm = xc[:, :SSM_D_INNER], xc[:, SSM_D_INNER:SSM_D_INNER + gn], xc[:, SSM_D_INNER + gn:]

        pre = dt_ref[bi] + dtb_ref[...]
        dt = jnp.maximum(pre, 0.0) + jnp.log1p(jnp.exp(-jnp.abs(pre)))
        a_col = _sel_dot(tri_ref[...], dt * a_neg)
        a_row = a_col.T
        a_exp = _dot_sel(a_col, expand)
        dt_exp = _dot_sel(dt, expand)
        total = a_exp[BLK - 1:BLK, :]
        xdt = xs * dt_exp
        x_end = (xdt * jnp.exp(total - a_exp)).astype(BF16)
        xdt_b = xdt.astype(BF16)
        ys = []
        for g in range(SSM_GROUPS):
            b_g = bm[:, g * SSM_STATE:(g + 1) * SSM_STATE]
            c_g = cm[:, g * SSM_STATE:(g + 1) * SSM_STATE].astype(BF16)
            cb = _dot_nt(c_g, b_g.astype(BF16))
            xg = xdt_b[:, g * gw:(g + 1) * gw]
            y_g = jnp.zeros((BLK, gw), F32)
            for hh in range(hpg):
                h = g * hpg + hh
                seg = a_col[:, h:h + 1] - a_row[h:h + 1, :]
                m_h = (cb * jnp.exp(jnp.where(col <= row, seg, NEG))).astype(BF16)
                y_g = y_g + _dot(m_h, jnp.where(col_head == hh, xg, jnp.zeros_like(xg)))
            state = state_ref[bi, g]
            y_g = y_g + _dot(c_g, state.astype(BF16)) * jnp.exp(a_exp[:, g * gw:(g + 1) * gw])
            new = _dot(b_g.T.astype(BF16), x_end[:, g * gw:(g + 1) * gw])
            state_ref[bi, g] = state * jnp.exp(total[:, g * gw:(g + 1) * gw]) + new
            ys.append(y_g)
        y = jnp.concatenate(ys, axis=-1) + xs * dskip_ref[...]
        y = y * _silu(z_ref[bi].astype(F32))
        o_ref[bi] = _rms(y, ng_ref[...]).astype(o_ref.dtype)


def _ssd(z, xbc, dt, conv_w, conv_b, dt_bias, a_log, d_skip, norm_g):
    b, s, c = xbc.shape
    di = SSM_D_INNER
    pad = lambda t: jnp.pad(t.astype(F32), (0, LANES - t.shape[0])).reshape(1, LANES)
    tri = jnp.tril(jnp.ones((BLK, BLK), BF16))
    expand = (jnp.arange(LANES)[:, None] == (jnp.arange(di) // SSM_HEAD_DIM)[None, :]).astype(BF16)
    tok = lambda w: pl.BlockSpec((b, BLK, w), lambda ci: (0, ci, 0))
    const = lambda shape: pl.BlockSpec(shape, lambda ci: (0,) * len(shape))
    return pl.pallas_call(
        _ssd_kernel,
        grid=(s // BLK,),
        in_specs=[tok(di), tok(c), tok(LANES), const((SSM_CONV, c)), const((1, c)),
                  const((1, LANES)), const((1, LANES)), const((1, di)), const((1, di)),
                  const((BLK, BLK)), const((LANES, di))],
        out_specs=tok(di),
        out_shape=jax.ShapeDtypeStruct((b, s, di), BF16),
        scratch_shapes=[pltpu.VMEM((b, 8, c), F32), pltpu.VMEM((b, 8 + BLK, c), F32),
                        pltpu.VMEM((b, SSM_GROUPS, SSM_STATE, di // SSM_GROUPS), F32)],
        compiler_params=_params(("arbitrary",), 32),
        name="ssd",
    )(z, xbc, dt, conv_w, conv_b.reshape(1, c), pad(dt_bias), pad(a_log),
      jnp.repeat(d_skip.astype(F32), SSM_HEAD_DIM).reshape(1, di), norm_g.reshape(1, di), tri, expand)


def _from_residue_view(src_ref, slab_ref, dil, width):
    if dil == 1:
        return src_ref[...].astype(F32)
    rows = src_ref.shape[0]
    tiles = width // LANES
    for r in range(dil):
        for t in range(tiles):
            col = r * width + t * LANES
            slab_ref[t, pl.ds(r, rows, stride=dil), :] = src_ref[:, col:col + LANES].astype(F32)
    return jnp.concatenate([slab_ref[t] for t in range(tiles)], axis=-1)


def _merge_kernel(x_ref, *refs, dils):
    n = len(dils)
    o_refs, l_refs = refs[:n], refs[n:2 * n]
    yb_ref, yc_ref, yd_ref, gate_ref, wb_ref, wo_ref, hexp_ref, out_ref = refs[2 * n:2 * n + 8]
    o_slabs, l_slabs = refs[2 * n + 8:3 * n + 8], refs[3 * n + 8:]
    d = x_ref.shape[-1]
    bw = yb_ref.shape[-1]
    lses = [_from_residue_view(l_refs[i], l_slabs[i], dils[i], LANES) for i in range(n)]
    m = functools.reduce(jnp.maximum, lses)
    es = [jnp.exp(l - m) for l in lses]
    inv = 1.0 / functools.reduce(jnp.add, es)
    ya = jnp.zeros((x_ref.shape[0], bw), F32)
    for i in range(n):
        wt = _dot((es[i] * inv).astype(BF16), hexp_ref[...])
        ya = ya + wt * _from_residue_view(o_refs[i], o_slabs[i], dils[i], bw)
    branches = (ya.astype(BF16), yb_ref[...], yc_ref[...], yd_ref[...])
    merged = jnp.zeros((x_ref.shape[0], d), F32)
    for i, y in enumerate(branches):
        gate = _sigmoid(gate_ref[:, i * d:(i + 1) * d].astype(F32))
        merged = merged + gate * _dot(y, wb_ref[i])
    out_ref[...] = x_ref[...] + _dot(merged.astype(BF16), wo_ref[...])


def _merge(x2d, o_parts, lse_parts, dils, yb, yc, yd, gates, w_branch, w_out, tm=256):
    t, d = x2d.shape
    bw = yb.shape[-1]
    row = lambda w: pl.BlockSpec((tm, w), lambda i: (i, 0))
    view = lambda w, dil: pl.BlockSpec((tm // dil, dil * w), lambda i: (i, 0))
    per_head = LANES // DIL_HEADS
    hexp = (jnp.arange(LANES)[:, None]
            == (jnp.arange(bw) // DIL_HEAD_DIM * per_head)[None, :]).astype(BF16)
    return pl.pallas_call(
        functools.partial(_merge_kernel, dils=tuple(dils)),
        grid=(t // tm,),
        in_specs=[row(d)] + [view(bw, dil) for dil in dils] + [view(LANES, dil) for dil in dils]
                 + [row(bw)] * 3 + [row(N_BRANCH * d),
                                    pl.BlockSpec(w_branch.shape, lambda i: (0, 0, 0)),
                                    pl.BlockSpec(w_out.shape, lambda i: (0, 0)),
                                    pl.BlockSpec(hexp.shape, lambda i: (0, 0))],
        out_specs=row(d),
        out_shape=jax.ShapeDtypeStruct((t, d), F32),
        scratch_shapes=[pltpu.VMEM((bw // LANES, tm, LANES), F32) for _ in dils]
                       + [pltpu.VMEM((1, tm, LANES), F32) for _ in dils],
        compiler_params=_params(("parallel",), 48),
        name="merge",
    )(x2d, *o_parts, *lse_parts, yb, yc, yd, gates, w_branch, w_out, hexp)


def _xattn_ffn_kernel(x_ref, gx_ref, wq_ref, k_ref, v_ref, wxo_ref, gf_ref, wa_ref, wb_ref, wo_ref,
                      fg_ref, out_ref, *, final_norm):
    x = x_ref[0]
    q = _dot(_rms(x, gx_ref[...]).astype(BF16), wq_ref[...]).astype(BF16)
    k, v = k_ref[0], v_ref[0]
    outs = []
    for h in range(X_HEADS):
        sl = slice(h * X_HEAD_DIM, (h + 1) * X_HEAD_DIM)
        s = _dot_nt(q[:, sl], k[:, sl]) * (X_HEAD_DIM ** -0.5)
        p = jnp.exp(s - jnp.max(s, axis=-1, keepdims=True))
        a = p / jnp.sum(p, axis=-1, keepdims=True)
        outs.append(_dot(a.astype(BF16), v[:, sl]))
    x = x + _dot(jnp.concatenate(outs, axis=-1).astype(BF16), wxo_ref[...])
    h = _rms(x, gf_ref[...]).astype(BF16)
    a = _dot(h, wa_ref[...])
    b = _dot(h, wb_ref[...])
    y = x + _dot((_silu(a) * b).astype(BF16), wo_ref[...])
    if final_norm:
        y = _rms(y, fg_ref[...])
    out_ref[0] = y


def _xattn_ffn(x, gx, w_q, k, v, w_xo, gf, w_a, w_b, w_o, final_g, final_norm, tm=512):
    b, s, d = x.shape
    m, w = k.shape[1], k.shape[2]
    full = lambda wt: pl.BlockSpec(wt.shape, lambda bi, i: (0, 0), pipeline_mode=pl.Buffered(1))
    vec = pl.BlockSpec((1, d), lambda bi, i: (0, 0))
    mem = pl.BlockSpec((1, m, w), lambda bi, i: (bi, 0, 0))
    return pl.pallas_call(
        functools.partial(_xattn_ffn_kernel, final_norm=final_norm),
        grid=(b, s // tm),
        in_specs=[pl.BlockSpec((1, tm, d), lambda bi, i: (bi, i, 0)), vec, full(w_q), mem, mem,
                  full(w_xo), vec, full(w_a), full(w_b), full(w_o), vec],
        out_specs=pl.BlockSpec((1, tm, d), lambda bi, i: (bi, i, 0)),
        out_shape=jax.ShapeDtypeStruct((b, s, d), F32),
        compiler_params=_params(("parallel", "parallel"), 56),
        name="xattn_ffn",
    )(x, gx.reshape(1, d), w_q, k, v, w_xo, gf.reshape(1, d), w_a, w_b, w_o, final_g.reshape(1, d))


def _ret_feature_order():
    quarter = RET_KEY_DIM // 2
    first = [h * RET_KEY_DIM + f for h in range(RET_HEADS) for f in range(quarter)]
    return np.array(first + [c + quarter for c in first])


def _layer(x, mem, p, lam_init, final_g, final_norm):
    b, s, d = x.shape
    t = b * s
    x2d = x.reshape(t, d)
    a_w = DIL_HEADS * DIL_HEAD_DIM
    b_qk, b_vg = RET_HEADS * RET_KEY_DIM, RET_HEADS * RET_VAL_DIM
    c_w = DIFF_HEADS * 2 * DIFF_HEAD_DIM
    d_xbc = SSM_D_INNER + 2 * SSM_GROUPS * SSM_STATE
    sizes = (a_w, a_w, a_w, b_qk, b_qk, b_vg, b_vg, c_w, c_w, c_w, SSM_D_INNER, d_xbc, SSM_HEADS,
             N_BRANCH * d)
    offs = np.concatenate([[0], np.cumsum(sizes)])
    w_in = p['w_in']
    col = lambda i: w_in[:, offs[i]:offs[i + 1]]
    order = _ret_feature_order()
    w_dt = jnp.pad(col(12), ((0, 0), (0, LANES - SSM_HEADS)))
    bf = lambda w: w.astype(BF16)

    dils = [dil for _, dil in DIL_PATTERNS]
    nd = len(dils)
    proj = _proj_views(x2d, p['norm_mix_g'], [bf(col(i)) for i in (0, 1, 2)],
                       [DIL_HEAD_DIM ** -0.5, 1.0, 1.0], [bf(col(i)) for i in (7, 8, 9)],
                       [DIFF_HEAD_DIM ** -0.5, 1.0, 1.0], dils)
    aq, ak, av = proj[0:nd], proj[nd:2 * nd], proj[2 * nd:3 * nd]
    cq, ck, cv = proj[3 * nd:]
    bq, bk, bv, bg, dz, dxbc, ddt = _norm_matmul(
        x2d, p['norm_mix_g'],
        [bf(col(3)[:, order]), bf(col(4)[:, order]), bf(col(5)), bf(col(6)), bf(col(10)), bf(col(11)),
         bf(w_dt)],
        [F32, F32, BF16, BF16, BF16, F32, F32], [1.0] * 7, tm=512)
    (gates,) = _norm_matmul(x2d, p['norm_mix_g'], [bf(col(13))], [BF16], [1.0], tm=512)

    r3 = lambda y: y.reshape(b, s, y.shape[-1])
    o_parts, lse_parts = _band_attention(aq, ak, av, b, s)
    y_b = _retention(r3(bq), r3(bk), r3(bv), r3(bg), p['ret_gn_g'])
    y_c = _diff_attention(r3(cq), r3(ck), r3(cv), p['diff_lambda'], p['diff_subln_g'], lam_init)
    y_d = _ssd(r3(dz), r3(dxbc), r3(ddt), p['ssm_conv_w'], p['ssm_conv_b'], p['ssm_dt_bias'],
               p['ssm_A_log'], p['ssm_D'], p['ssm_norm_g'])
    x2d = _merge(x2d, o_parts, lse_parts, dils, y_b.reshape(t, -1), y_c.reshape(t, -1),
                 y_d.reshape(t, -1), gates, bf(p['w_branch']), bf(p['w_mix_out']))

    xw = X_HEADS * X_HEAD_DIM
    m2d = mem.reshape(-1, d)
    mk, mv = _norm_matmul(m2d, p['norm_mem_g'], [bf(p['w_xkv'][:, :xw]), bf(p['w_xkv'][:, xw:])],
                          [BF16, BF16], [1.0, 1.0], tm=256)
    mlen = mem.shape[1]
    hid = p['w_ffn_out'].shape[0]
    return _xattn_ffn(x2d.reshape(b, s, d), p['norm_x_g'], bf(p['w_xq']), mk.reshape(b, mlen, xw),
                      mv.reshape(b, mlen, xw), bf(p['w_xo']), p['norm_ffn_g'], bf(p['w_ffn_in'][:, :hid]),
                      bf(p['w_ffn_in'][:, hid:]), bf(p['w_ffn_out']), final_g, final_norm)


def kernel(x, mem, norm_mix_g, w_in, ret_gn_g, diff_lambda, diff_subln_g, ssm_conv_w, ssm_conv_b,
           ssm_dt_bias, ssm_A_log, ssm_D, ssm_norm_g, w_branch, w_mix_out, norm_x_g, norm_mem_g, w_xq,
           w_xkv, w_xo, norm_ffn_g, w_ffn_in, w_ffn_out, norm_f_g):
    stacked = dict(norm_mix_g=norm_mix_g, w_in=w_in, ret_gn_g=ret_gn_g, diff_lambda=diff_lambda,
                   diff_subln_g=diff_subln_g, ssm_conv_w=ssm_conv_w, ssm_conv_b=ssm_conv_b,
                   ssm_dt_bias=ssm_dt_bias, ssm_A_log=ssm_A_log, ssm_D=ssm_D, ssm_norm_g=ssm_norm_g,
                   w_branch=w_branch, w_mix_out=w_mix_out, norm_x_g=norm_x_g, norm_mem_g=norm_mem_g,
                   w_xq=w_xq, w_xkv=w_xkv, w_xo=w_xo, norm_ffn_g=norm_ffn_g, w_ffn_in=w_ffn_in,
                   w_ffn_out=w_ffn_out)
    depth = w_in.shape[0]
    for l in range(depth):
        lam_init = 0.8 - 0.6 * math.exp(-0.3 * l)
        p = {name: val[l] for name, val in stacked.items()}
        x = _layer(x, mem, p, lam_init, norm_f_g, final_norm=(l == depth - 1))
    return x
```

```python
import functools
import math

import jax
import jax.numpy as jnp
import numpy as np
from jax import lax
from jax.experimental import pallas as pl
from jax.experimental.pallas import tpu as pltpu

F32 = jnp.float32
BF16 = jnp.bfloat16

EPS = 1e-6
NEG = -1e30
LOG2E = math.log2(math.e)
BLK = 128

DIL_HEADS = 8
DIL_HEAD_DIM = 64
DIL_PATTERNS = ((128, 1), (512, 4), (2048, 16))
RET_HEADS = 4
RET_KEY_DIM = 64
RET_VAL_DIM = 128
ROPE_BASE = 10000.0
DIFF_HEADS = 4
DIFF_HEAD_DIM = 64
SSM_D_INNER = 512
SSM_HEAD_DIM = 64
SSM_HEADS = 8
SSM_GROUPS = 2
SSM_STATE = 128
SSM_CONV = 4
N_BRANCH = 4
BRANCH_WIDTH = 512
X_HEADS = 4
X_HEAD_DIM = 128
LANES = 128

MIB = 1024 * 1024


def _params(semantics, vmem_mib):
    return pltpu.CompilerParams(dimension_semantics=semantics,
                                vmem_limit_bytes=vmem_mib * MIB)


def _dot(a, b):
    return jnp.dot(a, b, preferred_element_type=F32)


def _dot_nt(a, b):
    return lax.dot_general(a, b, (((1,), (1,)), ((), ())), preferred_element_type=F32)


def _split3(x):
    x1 = x.astype(BF16)
    r1 = x - x1.astype(F32)
    x2 = r1.astype(BF16)
    x3 = (r1 - x2.astype(F32)).astype(BF16)
    return x1, x2, x3


def _dot_sel(x, sel):
    x1, x2, x3 = _split3(x)
    return _dot(x1, sel) + _dot(x2, sel) + _dot(x3, sel)


def _sel_dot(sel, x):
    x1, x2, x3 = _split3(x)
    return _dot(sel, x1) + _dot(sel, x2) + _dot(sel, x3)


def _sigmoid(x):
    return 0.5 * jnp.tanh(0.5 * x) + 0.5


def _silu(x):
    return x * _sigmoid(x)


def _rms(x, g):
    ms = jnp.mean(x * x, axis=-1, keepdims=True)
    return x * lax.rsqrt(ms + EPS) * g


def _norm_matmul_kernel(x_ref, g_ref, *refs, n_out, acts):
    w_refs, o_refs = refs[:n_out], refs[n_out:]
    h = _rms(x_ref[...], g_ref[...]).astype(BF16)
    for w_ref, o_ref, act in zip(w_refs, o_refs, acts):
        y = _dot(h, w_ref[...])
        if act is not None:
            y = act(y)
        o_ref[...] = y.astype(o_ref.dtype)


def _norm_matmul(x2d, g, ws, out_dtypes, acts, tm, vmem_mib=48):
    t, d = x2d.shape
    in_specs = [pl.BlockSpec((tm, d), lambda i: (i, 0)),
                pl.BlockSpec((1, d), lambda i: (0, 0))]
    in_specs += [pl.BlockSpec(w.shape, lambda i: (0, 0)) for w in ws]
    out_specs = [pl.BlockSpec((tm, w.shape[1]), lambda i: (i, 0)) for w in ws]
    out_shape = [jax.ShapeDtypeStruct((t, w.shape[1]), dt) for w, dt in zip(ws, out_dtypes)]
    return pl.pallas_call(
        functools.partial(_norm_matmul_kernel, n_out=len(ws), acts=tuple(acts)),
        grid=(t // tm,), in_specs=in_specs, out_specs=out_specs, out_shape=out_shape,
        compiler_params=_params(("parallel",), vmem_mib),
        name="norm_matmul",
    )(x2d, g.reshape(1, d), *ws)


def _proj_views_kernel(x_ref, g_ref, *refs, n_view, n_plain, dils, view_scales, plain_scales):
    nd = len(dils)
    n_w = n_view + n_plain
    w_refs = refs[:n_w]
    view_refs = refs[n_w:n_w + n_view * nd]
    plain_refs = refs[n_w + n_view * nd:n_w + n_view * nd + n_plain]
    slab_ref = refs[-1]
    tm = x_ref.shape[0]
    h = _rms(x_ref[...], g_ref[...]).astype(BF16)
    for i in range(n_view):
        y = _dot(h, w_refs[i][...])
        if view_scales[i] != 1.0:
            y = y * view_scales[i]
        width = y.shape[1]
        tiles = width // LANES
        for t in range(tiles):
            slab_ref[t] = y[:, t * LANES:(t + 1) * LANES]
        for di, dil in enumerate(dils):
            o_ref = view_refs[i * nd + di]
            if dil == 1:
                o_ref[...] = y.astype(o_ref.dtype)
                continue
            for r in range(dil):
                for t in range(tiles):
                    col = r * width + t * LANES
                    o_ref[:, col:col + LANES] = slab_ref[t, pl.ds(r, tm // dil, stride=dil), :].astype(o_ref.dtype)
    for i in range(n_plain):
        y = _dot(h, w_refs[n_view + i][...])
        if plain_scales[i] != 1.0:
            y = y * plain_scales[i]
        plain_refs[i][...] = y.astype(plain_refs[i].dtype)


def _proj_views(x2d, g, view_ws, view_scales, plain_ws, plain_scales, dils, tm=512):
    t, d = x2d.shape
    ws = list(view_ws) + list(plain_ws)
    full = lambda w: pl.BlockSpec(w.shape, lambda i: (0, 0))
    out_specs, out_shape = [], []
    for w in view_ws:
        n = w.shape[1]
        for dil in dils:
            out_specs.append(pl.BlockSpec((tm // dil, dil * n), lambda i: (i, 0)))
            out_shape.append(jax.ShapeDtypeStruct((t // dil, dil * n), BF16))
    for w in plain_ws:
        out_specs.append(pl.BlockSpec((tm, w.shape[1]), lambda i: (i, 0)))
        out_shape.append(jax.ShapeDtypeStruct((t, w.shape[1]), BF16))
    width = view_ws[0].shape[1]
    return pl.pallas_call(
        functools.partial(_proj_views_kernel, n_view=len(view_ws), n_plain=len(plain_ws), dils=tuple(dils),
                          view_scales=tuple(view_scales), plain_scales=tuple(plain_scales)),
        grid=(t // tm,),
        in_specs=[pl.BlockSpec((tm, d), lambda i: (i, 0)), pl.BlockSpec((1, d), lambda i: (0, 0))]
                 + [full(w) for w in ws],
        out_specs=out_specs, out_shape=out_shape,
        scratch_shapes=[pltpu.VMEM((width // LANES, tm, LANES), F32)],
        compiler_params=_params(("parallel",), 48),
        name="proj_views",
    )(x2d, g.reshape(1, d), *ws)


def _band_attn_kernel(*refs, steps_per_residue):
    n_pat = len(steps_per_residue)
    ins, outs = refs[:6 * n_pat], refs[6 * n_pat:]
    step = pl.program_id(1)
    for i in range(n_pat):
        bias_ref, q_ref, kp_ref, kc_ref, vp_ref, vc_ref = ins[6 * i:6 * i + 6]
        o_ref, lse_ref = outs[2 * i:2 * i + 2]
        first = jnp.minimum(step % steps_per_residue[i], 1)
        lo, hi = slice(0, BLK), slice(BLK, 2 * BLK)
        _band_block(bias_ref[first], q_ref.at[0, lo], kp_ref.at[0], kc_ref.at[0, lo], vp_ref.at[0],
                    vc_ref.at[0, lo], o_ref.at[0, lo], lse_ref.at[0, lo])
        _band_block(bias_ref[1], q_ref.at[0, hi], kc_ref.at[0, lo], kc_ref.at[0, hi], vc_ref.at[0, lo],
                    vc_ref.at[0, hi], o_ref.at[0, hi], lse_ref.at[0, hi])


def _band_block(bias, q_ref, kp_ref, kc_ref, vp_ref, vc_ref, o_ref, lse_ref):
    hd = DIL_HEAD_DIM
    lane = lax.broadcasted_iota(jnp.int32, (1, LANES), 1)
    ones = jnp.ones((2 * BLK, LANES), BF16)
    lse_tile = jnp.zeros((BLK, LANES), F32)
    for t in range(DIL_HEADS * hd // LANES):
        sl = slice(t * LANES, (t + 1) * LANES)
        q = q_ref[:, sl]
        k2 = jnp.concatenate([kp_ref[:, sl], kc_ref[:, sl]], axis=0)
        v2 = jnp.concatenate([vp_ref[:, sl], vc_ref[:, sl]], axis=0)
        v_ext = jnp.concatenate([v2, ones], axis=-1)
        pair = []
        for half in range(LANES // hd):
            in_head = (lane // hd) == half
            s = _dot_nt(jnp.where(in_head, q, jnp.zeros_like(q)), k2) + bias
            m = jnp.max(jnp.maximum(s[:, :BLK], s[:, BLK:]), axis=-1, keepdims=True)
            p = jnp.exp(s - m).astype(BF16)
            pv = _dot(p, v_ext)
            den = pv[:, LANES:]
            pair.append(pv[:, :LANES] / den)
            h = t * (LANES // hd) + half
            per_head = LANES // DIL_HEADS
            lse_tile = jnp.where((lane // per_head) == h, m + jnp.log(den), lse_tile)
        o_ref[:, sl] = jnp.where((lane // hd) == 0, pair[0], pair[1]).astype(o_ref.dtype)
    lse_ref[...] = lse_tile


def _band_bias(window):
    i = jnp.arange(BLK)[:, None]
    j = jnp.arange(2 * BLK)[None, :]
    dist = BLK + i - j
    band = (dist >= 0) & (dist <= window)
    first = band & (j >= BLK)
    return jnp.where(jnp.stack([first, band]), 0.0, NEG).astype(F32)


def _band_attention(qs, ks, vs, batch, seq):
    w = DIL_HEADS * DIL_HEAD_DIM
    steps = seq // (2 * BLK)
    args, in_specs, out_specs, out_shape, per_residue = [], [], [], [], []
    for (window, dil), q, k, v in zip(DIL_PATTERNS, qs, ks, vs):
        l = seq // dil
        ns = l // (2 * BLK)
        per_residue.append(ns)
        view = lambda t, l=l, dil=dil: t.reshape(batch, l, dil * w)
        cur = lambda wd, ns=ns: pl.BlockSpec((1, 2 * BLK, wd), lambda bi, n: (bi, n % ns, n // ns))
        prev = pl.BlockSpec((1, BLK, w),
                            lambda bi, n, ns=ns: (bi, jnp.maximum(2 * (n % ns) - 1, 0), n // ns))
        bias = pl.BlockSpec((2, BLK, 2 * BLK), lambda bi, n: (0, 0, 0))
        args += [_band_bias(window // dil), view(q), view(k), view(k), view(v), view(v)]
        in_specs += [bias, cur(w), prev, cur(w), prev, cur(w)]
        out_specs += [cur(w), cur(LANES)]
        out_shape += [jax.ShapeDtypeStruct((batch, l, dil * w), BF16),
                      jax.ShapeDtypeStruct((batch, l, dil * LANES), F32)]
    outs = pl.pallas_call(
        functools.partial(_band_attn_kernel, steps_per_residue=tuple(per_residue)),
        grid=(batch, steps),
        in_specs=in_specs, out_specs=out_specs, out_shape=out_shape,
        compiler_params=_params(("parallel", "parallel"), 32),
        name="band_attention",
    )(*args)
    flat = [o.reshape(-1, o.shape[-1]) for o in outs]
    return flat[0::2], flat[1::2]


def _retention_kernel(q_ref, k_ref, v_ref, g_ref, cos_ref, sin_ref, dq_ref, dk_ref, dec_ref,
                      cdec_ref, gn_ref, o_ref, state_ref):
    @pl.when(pl.program_id(0) == 0)
    def _():
        state_ref[...] = jnp.zeros_like(state_ref)

    cos, sin = cos_ref[...], sin_ref[...]
    half = RET_HEADS * RET_KEY_DIM // 2

    def rot(t):
        t1, t2 = t[:, :half], t[:, half:]
        return jnp.concatenate([t1 * cos - t2 * sin, t1 * sin + t2 * cos], axis=-1)

    lane_head = (lax.broadcasted_iota(jnp.int32, (1, 2 * half), 1) % half) // (RET_KEY_DIM // 2)
    for bi in range(q_ref.shape[0]):
        rq = rot(q_ref[bi])
        rk = rot(k_ref[bi]) * (RET_KEY_DIM ** -0.5)
        v = v_ref[bi]
        rkb = rk.astype(BF16)
        state = state_ref[bi]
        o_cross = _dot((rq * dq_ref[...]).astype(BF16), state.astype(BF16))
        outs = []
        for h in range(RET_HEADS):
            qh = jnp.where(lane_head == h, rq, 0.0).astype(BF16)
            s = _dot_nt(qh, rkb) * dec_ref[h]
            outs.append(_dot(s.astype(BF16), v[:, h * RET_VAL_DIM:(h + 1) * RET_VAL_DIM]))
        o = jnp.concatenate(outs, axis=-1) + o_cross
        k_end_t = (rk * dk_ref[...]).T.astype(BF16)
        kv = _dot(k_end_t, v)
        row_head = (lax.broadcasted_iota(jnp.int32, kv.shape, 0) % half) // (RET_KEY_DIM // 2)
        col_head = lax.broadcasted_iota(jnp.int32, kv.shape, 1) // RET_VAL_DIM
        state_ref[bi] = state * cdec_ref[...] + jnp.where(row_head == col_head, kv, 0.0)
        normed = []
        for h in range(RET_HEADS):
            oh = o[:, h * RET_VAL_DIM:(h + 1) * RET_VAL_DIM]
            mu = jnp.mean(oh, axis=-1, keepdims=True)
            cen = oh - mu
            var = jnp.mean(cen * cen, axis=-1, keepdims=True)
            normed.append(cen * lax.rsqrt(var + EPS))
        y = jnp.concatenate(normed, axis=-1) * gn_ref[...]
        o_ref[bi] = (g_ref[bi].astype(F32) * y).astype(o_ref.dtype)


def _retention_tables(s):
    h, dk, dv = RET_HEADS, RET_KEY_DIM, RET_VAL_DIM
    quarter = dk // 2
    pos = jnp.arange(s, dtype=F32)
    inv_freq = ROPE_BASE ** (-jnp.arange(quarter, dtype=F32) / quarter)
    ang = pos[:, None] * jnp.tile(inv_freq, h)[None, :]
    log_gamma = jnp.log1p(-jnp.exp2(-5.0 - jnp.arange(h, dtype=F32)))
    idx = jnp.arange(BLK, dtype=F32)
    rel = idx[:, None] - idx[None, :]
    decay = jnp.where(rel >= 0, jnp.exp(log_gamma[:, None, None] * jnp.maximum(rel, 0.0)), 0.0)
    lane_gamma = jnp.tile(jnp.repeat(log_gamma, quarter), 2)
    dq = jnp.exp((idx + 1.0)[:, None] * lane_gamma[None, :])
    dkt = jnp.exp((BLK - 1 - idx)[:, None] * lane_gamma[None, :])
    cdec = jnp.repeat(jnp.exp(log_gamma * BLK), dv)[None, :]
    return jnp.cos(ang), jnp.sin(ang), dq, dkt, decay, cdec


def _retention(q, k, v, g, gn_g):
    b, s, wk = q.shape
    wv = v.shape[-1]
    cos, sin, dq, dkt, decay, cdec = _retention_tables(s)
    tok = lambda w: pl.BlockSpec((b, BLK, w), lambda c: (0, c, 0))
    const = lambda shape: pl.BlockSpec(shape, lambda c: (0,) * len(shape))
    return pl.pallas_call(
        _retention_kernel,
        grid=(s // BLK,),
        in_specs=[tok(wk), tok(wk), tok(wv), tok(wv),
                  pl.BlockSpec((BLK, wk // 2), lambda c: (c, 0)),
                  pl.BlockSpec((BLK, wk // 2), lambda c: (c, 0)),
                  const((BLK, wk)), const((BLK, wk)), const((RET_HEADS, BLK, BLK)),
                  const((1, wv)), const((1, wv))],
        out_specs=tok(wv),
        out_shape=jax.ShapeDtypeStruct((b, s, wv), BF16),
        scratch_shapes=[pltpu.VMEM((b, wk, wv), F32)],
        compiler_params=_params(("arbitrary",), 32),
        name="retention",
    )(q, k, v, g, cos, sin, dq, dkt, decay, cdec, gn_g.reshape(1, wv))


def _diff_attn_kernel(lam_ref, g_ref, q_ref, k_ref, v_ref, o_ref, s_ref, m_ref, vext_ref, acc_ref, *,
                      tq, lam_init):
    d = DIFF_HEAD_DIM
    hw = 2 * d
    nl = tq // LANES
    qi = pl.program_id(2)
    q = q_ref[0]
    qs = (q[:, :d], q[:, d:])

    @pl.when(qi == 0)
    def _():
        vext_ref[:, :hw] = v_ref[0]
        vext_ref[:, hw:] = jnp.ones((vext_ref.shape[0], hw), BF16)

    def fold_max(x):
        r = x[:, :LANES]
        for c in range(1, nl):
            r = jnp.maximum(r, x[:, c * LANES:(c + 1) * LANES])
        return r

    m_ref[...] = jnp.full(m_ref.shape, NEG, F32)

    def scores(j, diagonal):
        k = k_ref[0, pl.ds(pl.multiple_of(j * tq, tq), tq), :]
        for c in range(2):
            s = _dot_nt(qs[c], k[:, c * d:(c + 1) * d]) * LOG2E
            if diagonal:
                row = lax.broadcasted_iota(jnp.int32, (tq, tq), 0)
                col = lax.broadcasted_iota(jnp.int32, (tq, tq), 1)
                s = jnp.where(col <= row, s, NEG)
            s_ref[c, j] = s
            m_ref[c] = jnp.maximum(m_ref[c], fold_max(s))

    def for_each_block(n, body):
        def quad(jj, carry):
            for u in range(4):
                body(4 * jj + u)
            return carry

        lax.fori_loop(0, lax.shift_right_logical(n, 2), quad, 0)
        done = n & ~3

        @pl.when((n & 2) == 2)
        def _():
            body(done)
            body(done + 1)

        @pl.when((n & 1) == 1)
        def _():
            body(n - 1)

    for_each_block(qi, lambda j: scores(j, False))
    scores(qi, True)

    for c in range(2):
        m_ref[c] = jnp.broadcast_to(jnp.max(m_ref[c], axis=-1, keepdims=True), (tq, LANES))
    acc_ref[...] = jnp.zeros(acc_ref.shape, F32)

    def accumulate(j):
        v = vext_ref[pl.ds(pl.multiple_of(j * tq, tq), tq), :]
        for c in range(2):
            m = m_ref[c]
            s = s_ref[c, j]
            p = jnp.concatenate([jnp.exp2(s[:, t * LANES:(t + 1) * LANES] - m) for t in range(nl)], axis=-1)
            acc_ref[c] += _dot(p.astype(BF16), v)

    for_each_block(qi + 1, accumulate)

    lp = lam_ref[...]
    lam = (jnp.exp(jnp.sum(lp[0:1] * lp[1:2], keepdims=True))
           - jnp.exp(jnp.sum(lp[2:3] * lp[3:4], keepdims=True)) + lam_init)
    o1 = acc_ref[0, :, :hw] / acc_ref[0, :, hw:hw + 1]
    o2 = acc_ref[1, :, :hw] / acc_ref[1, :, hw:hw + 1]
    o_ref[0] = (_rms(o1 - lam * o2, g_ref[...]) * (1.0 - lam_init)).astype(o_ref.dtype)


def _diff_attention(q, k, v, lam_params, subln_g, lam_init, tq=512):
    b, s, w = q.shape
    hw = 2 * DIFF_HEAD_DIM
    return pl.pallas_call(
        functools.partial(_diff_attn_kernel, tq=tq, lam_init=lam_init),
        grid=(b, DIFF_HEADS, s // tq),
        in_specs=[pl.BlockSpec(lam_params.shape, lambda bi, h, i: (0, 0)),
                  pl.BlockSpec((1, hw), lambda bi, h, i: (0, 0)),
                  pl.BlockSpec((1, tq, hw), lambda bi, h, i: (bi, i, h)),
                  pl.BlockSpec((1, s, hw), lambda bi, h, i: (bi, 0, h)),
                  pl.BlockSpec((1, s, hw), lambda bi, h, i: (bi, 0, h))],
        out_specs=pl.BlockSpec((1, tq, hw), lambda bi, h, i: (bi, i, h)),
        out_shape=jax.ShapeDtypeStruct((b, s, w), BF16),
        scratch_shapes=[pltpu.VMEM((2, s // tq, tq, tq), F32), pltpu.VMEM((2, tq, LANES), F32),
                        pltpu.VMEM((s, 2 * hw), BF16), pltpu.VMEM((2, tq, 2 * hw), F32)],
        compiler_params=_params(("parallel", "parallel", "arbitrary"), 40),
        name="diff_attention",
    )(lam_params, subln_g.reshape(1, hw), q, k, v)


def _ssd_kernel(z_ref, xbc_ref, dt_ref, cw_ref, cb_ref, dtb_ref, alog_ref, dskip_ref, ng_ref,
                tri_ref, exp_ref, o_ref, carry_ref, ext_ref, state_ref):
    gn = SSM_GROUPS * SSM_STATE
    hpg = SSM_HEADS // SSM_GROUPS
    gw = hpg * SSM_HEAD_DIM
    tail = 8

    @pl.when(pl.program_id(0) == 0)
    def _():
        carry_ref[...] = jnp.zeros_like(carry_ref)
        state_ref[...] = jnp.zeros_like(state_ref)

    cw = cw_ref[...]
    lane = lax.broadcasted_iota(jnp.int32, (1, LANES), 1)
    a_neg = jnp.where(lane < SSM_HEADS, -jnp.exp(alog_ref[...]), 0.0)
    expand = exp_ref[...]
    row = lax.broadcasted_iota(jnp.int32, (BLK, BLK), 0)
    col = lax.broadcasted_iota(jnp.int32, (BLK, BLK), 1)
    col_head = lax.broadcasted_iota(jnp.int32, (1, gw), 1) // SSM_HEAD_DIM
    for bi in range(z_ref.shape[0]):
        xbc = xbc_ref[bi]
        ext_ref[bi, 0:tail, :] = carry_ref[bi]
        ext_ref[bi, tail:tail + BLK, :] = xbc
        carry_ref[bi] = xbc[BLK - tail:, :]
        conv = cb_ref[...] + cw[SSM_CONV - 1:SSM_CONV, :] * xbc
        for sft in range(1, SSM_CONV):
            conv = conv + (cw[SSM_CONV - 1 - sft:SSM_CONV - sft, :]
                           * ext_ref[bi, tail - sft:tail - sft + BLK, :])
        xc = _silu(conv)
        xs, bm, cm = xc[:, :SSM_D_INNER], xc[:, SSM_D_INNER:SSM_D_INNER + gn], xc[:, SSM_D_INNER + gn:]

        pre = dt_ref[bi] + dtb_ref[...]
        dt = jnp.maximum(pre, 0.0) + jnp.log1p(jnp.exp(-jnp.abs(pre)))
        a_col = _sel_dot(tri_ref[...], dt * a_neg)
        a_row = a_col.T
        a_exp = _dot_sel(a_col, expand)
        dt_exp = _dot_sel(dt, expand)
        total = a_exp[BLK - 1:BLK, :]
        xdt = xs * dt_exp
        x_end = (xdt * jnp.exp(total - a_exp)).astype(BF16)
        xdt_b = xdt.astype(BF16)
        ys = []
        for g in range(SSM_GROUPS):
            b_g = bm[:, g * SSM_STATE:(g + 1) * SSM_STATE]
            c_g = cm[:, g * SSM_STATE:(g + 1) * SSM_STATE].astype(BF16)
            cb = _dot_nt(c_g, b_g.astype(BF16))
            xg = xdt_b[:, g * gw:(g + 1) * gw]
            y_g = jnp.zeros((BLK, gw), F32)
            for hh in range(hpg):
                h = g * hpg + hh
                seg = a_col[:, h:h + 1] - a_row[h:h + 1, :]
                m_h = (cb * jnp.exp(jnp.where(col <= row, seg, NEG))).astype(BF16)
                y_g = y_g + _dot(m_h, jnp.where(col_head == hh, xg, jnp.zeros_like(xg)))
            state = state_ref[bi, g]
            y_g = y_g + _dot(c_g, state.astype(BF16)) * jnp.exp(a_exp[:, g * gw:(g + 1) * gw])
            new = _dot(b_g.T.astype(BF16), x_end[:, g * gw:(g + 1) * gw])
            state_ref[bi, g] = state * jnp.exp(total[:, g * gw:(g + 1) * gw]) + new
            ys.append(y_g)
        y = jnp.concatenate(ys, axis=-1) + xs * dskip_ref[...]
        y = y * z_ref[bi].astype(F32)
        o_ref[bi] = _rms(y, ng_ref[...]).astype(o_ref.dtype)


def _ssd(z, xbc, dt, conv_w, conv_b, dt_bias, a_log, d_skip, norm_g):
    b, s, c = xbc.shape
    di = SSM_D_INNER
    pad = lambda t: jnp.pad(t.astype(F32), (0, LANES - t.shape[0])).reshape(1, LANES)
    tri = jnp.tril(jnp.ones((BLK, BLK), BF16))
    expand = (jnp.arange(LANES)[:, None] == (jnp.arange(di) // SSM_HEAD_DIM)[None, :]).astype(BF16)
    tok = lambda w: pl.BlockSpec((b, BLK, w), lambda ci: (0, ci, 0))
    const = lambda shape: pl.BlockSpec(shape, lambda ci: (0,) * len(shape))
    return pl.pallas_call(
        _ssd_kernel,
        grid=(s // BLK,),
        in_specs=[tok(di), tok(c), tok(LANES), const((SSM_CONV, c)), const((1, c)),
                  const((1, LANES)), const((1, LANES)), const((1, di)), const((1, di)),
                  const((BLK, BLK)), const((LANES, di))],
        out_specs=tok(di),
        out_shape=jax.ShapeDtypeStruct((b, s, di), BF16),
        scratch_shapes=[pltpu.VMEM((b, 8, c), F32), pltpu.VMEM((b, 8 + BLK, c), F32),
                        pltpu.VMEM((b, SSM_GROUPS, SSM_STATE, di // SSM_GROUPS), F32)],
        compiler_params=_params(("arbitrary",), 32),
        name="ssd",
    )(z, xbc, dt, conv_w, conv_b.reshape(1, c), pad(dt_bias), pad(a_log),
      jnp.repeat(d_skip.astype(F32), SSM_HEAD_DIM).reshape(1, di), norm_g.reshape(1, di), tri, expand)


def _from_residue_view(src_ref, slab_ref, dil, width):
    if dil == 1:
        return src_ref[...].astype(F32)
    rows = src_ref.shape[0]
    tiles = width // LANES
    for r in range(dil):
        for t in range(tiles):
            col = r * width + t * LANES
            slab_ref[t, pl.ds(r, rows, stride=dil), :] = src_ref[:, col:col + LANES].astype(F32)
    return jnp.concatenate([slab_ref[t] for t in range(tiles)], axis=-1)


def _merge_kernel(x_ref, *refs, dils):
    n = len(dils)
    o_refs, l_refs = refs[:n], refs[n:2 * n]
    yb_ref, yc_ref, yd_ref, gate_ref, wb_ref, wo_ref, hexp_ref, out_ref = refs[2 * n:2 * n + 8]
    o_slabs, l_slabs = refs[2 * n + 8:3 * n + 8], refs[3 * n + 8:]
    d = x_ref.shape[-1]
    bw = yb_ref.shape[-1]
    lses = [_from_residue_view(l_refs[i], l_slabs[i], dils[i], LANES) for i in range(n)]
    m = functools.reduce(jnp.maximum, lses)
    es = [jnp.exp(l - m) for l in lses]
    inv = 1.0 / functools.reduce(jnp.add, es)
    ya = jnp.zeros((x_ref.shape[0], bw), F32)
    for i in range(n):
        wt = _dot((es[i] * inv).astype(BF16), hexp_ref[...])
        ya = ya + wt * _from_residue_view(o_refs[i], o_slabs[i], dils[i], bw)
    branches = (ya.astype(BF16), yb_ref[...], yc_ref[...], yd_ref[...])
    merged = jnp.zeros((x_ref.shape[0], d), F32)
    for i, y in enumerate(branches):
        gate = gate_ref[:, i * d:(i + 1) * d].astype(F32)
        merged = merged + gate * _dot(y, wb_ref[i])
    out_ref[...] = x_ref[...] + _dot(merged.astype(BF16), wo_ref[...])


def _merge(x2d, o_parts, lse_parts, dils, yb, yc, yd, gates, w_branch, w_out, tm=256):
    t, d = x2d.shape
    bw = yb.shape[-1]
    row = lambda w: pl.BlockSpec((tm, w), lambda i: (i, 0))
    view = lambda w, dil: pl.BlockSpec((tm // dil, dil * w), lambda i: (i, 0))
    per_head = LANES // DIL_HEADS
    hexp = (jnp.arange(LANES)[:, None]
            == (jnp.arange(bw) // DIL_HEAD_DIM * per_head)[None, :]).astype(BF16)
    return pl.pallas_call(
        functools.partial(_merge_kernel, dils=tuple(dils)),
        grid=(t // tm,),
        in_specs=[row(d)] + [view(bw, dil) for dil in dils] + [view(LANES, dil) for dil in dils]
                 + [row(bw)] * 3 + [row(N_BRANCH * d),
                                    pl.BlockSpec(w_branch.shape, lambda i: (0, 0, 0)),
                                    pl.BlockSpec(w_out.shape, lambda i: (0, 0)),
                                    pl.BlockSpec(hexp.shape, lambda i: (0, 0))],
        out_specs=row(d),
        out_shape=jax.ShapeDtypeStruct((t, d), F32),
        scratch_shapes=[pltpu.VMEM((bw // LANES, tm, LANES), F32) for _ in dils]
                       + [pltpu.VMEM((1, tm, LANES), F32) for _ in dils],
        compiler_params=_params(("parallel",), 48),
        name="merge",
    )(x2d, *o_parts, *lse_parts, yb, yc, yd, gates, w_branch, w_out, hexp)


def _xattn_ffn_kernel(x_ref, gx_ref, wq_ref, k_ref, v_ref, wxo_ref, gf_ref, wa_ref, wb_ref, wo_ref,
                      fg_ref, out_ref, *, final_norm):
    x = x_ref[0]
    q = _dot(_rms(x, gx_ref[...]).astype(BF16), wq_ref[...]).astype(BF16)
    k, v = k_ref[0], v_ref[0]
    outs = []
    for h in range(X_HEADS):
        sl = slice(h * X_HEAD_DIM, (h + 1) * X_HEAD_DIM)
        s = _dot_nt(q[:, sl], k[:, sl]) * (X_HEAD_DIM ** -0.5)
        p = jnp.exp(s - jnp.max(s, axis=-1, keepdims=True))
        a = p / jnp.sum(p, axis=-1, keepdims=True)
        outs.append(_dot(a.astype(BF16), v[:, sl]))
    x = x + _dot(jnp.concatenate(outs, axis=-1).astype(BF16), wxo_ref[...])
    h = _rms(x, gf_ref[...]).astype(BF16)
    a = _dot(h, wa_ref[...])
    b = _dot(h, wb_ref[...])
    y = x + _dot((_silu(a) * b).astype(BF16), wo_ref[...])
    if final_norm:
        y = _rms(y, fg_ref[...])
    out_ref[0] = y


def _xattn_ffn(x, gx, w_q, k, v, w_xo, gf, w_a, w_b, w_o, final_g, final_norm, tm=512):
    b, s, d = x.shape
    m, w = k.shape[1], k.shape[2]
    full = lambda wt: pl.BlockSpec(wt.shape, lambda bi, i: (0, 0), pipeline_mode=pl.Buffered(1))
    vec = pl.BlockSpec((1, d), lambda bi, i: (0, 0))
    mem = pl.BlockSpec((1, m, w), lambda bi, i: (bi, 0, 0))
    return pl.pallas_call(
        functools.partial(_xattn_ffn_kernel, final_norm=final_norm),
        grid=(b, s // tm),
        in_specs=[pl.BlockSpec((1, tm, d), lambda bi, i: (bi, i, 0)), vec, full(w_q), mem, mem,
                  full(w_xo), vec, full(w_a), full(w_b), full(w_o), vec],
        out_specs=pl.BlockSpec((1, tm, d), lambda bi, i: (bi, i, 0)),
        out_shape=jax.ShapeDtypeStruct((b, s, d), F32),
        compiler_params=_params(("parallel", "parallel"), 56),
        name="xattn_ffn",
    )(x, gx.reshape(1, d), w_q, k, v, w_xo, gf.reshape(1, d), w_a, w_b, w_o, final_g.reshape(1, d))


def _ret_feature_order():
    quarter = RET_KEY_DIM // 2
    first = [h * RET_KEY_DIM + f for h in range(RET_HEADS) for f in range(quarter)]
    return np.array(first + [c + quarter for c in first])


def _layer(x, mem, p, lam_init, final_g, final_norm):
    b, s, d = x.shape
    t = b * s
    x2d = x.reshape(t, d)
    a_w = DIL_HEADS * DIL_HEAD_DIM
    b_qk, b_vg = RET_HEADS * RET_KEY_DIM, RET_HEADS * RET_VAL_DIM
    c_w = DIFF_HEADS * 2 * DIFF_HEAD_DIM
    d_xbc = SSM_D_INNER + 2 * SSM_GROUPS * SSM_STATE
    sizes = (a_w, a_w, a_w, b_qk, b_qk, b_vg, b_vg, c_w, c_w, c_w, SSM_D_INNER, d_xbc, SSM_HEADS,
             N_BRANCH * d)
    offs = np.concatenate([[0], np.cumsum(sizes)])
    w_in = p['w_in']
    col = lambda i: w_in[:, offs[i]:offs[i + 1]]
    order = _ret_feature_order()
    w_dt = jnp.pad(col(12), ((0, 0), (0, LANES - SSM_HEADS)))
    bf = lambda w: w.astype(BF16)

    dils = [dil for _, dil in DIL_PATTERNS]
    nd = len(dils)
    proj = _proj_views(x2d, p['norm_mix_g'], [bf(col(i)) for i in (0, 1, 2)],
                       [DIL_HEAD_DIM ** -0.5, 1.0, 1.0], [bf(col(i)) for i in (7, 8, 9)],
                       [DIFF_HEAD_DIM ** -0.5, 1.0, 1.0], dils)
    aq, ak, av = proj[0:nd], proj[nd:2 * nd], proj[2 * nd:3 * nd]
    cq, ck, cv = proj[3 * nd:]
    bq, bk, bv, bg, dz, dxbc, ddt = _norm_matmul(
        x2d, p['norm_mix_g'],
        [bf(col(3)[:, order]), bf(col(4)[:, order]), bf(col(5)), bf(col(6)), bf(col(10)), bf(col(11)),
         bf(w_dt)],
        [F32, F32, BF16, BF16, BF16, F32, F32], [None, None, None, _silu, _silu, None, None], tm=512)
    (gates,) = _norm_matmul(x2d, p['norm_mix_g'], [bf(col(13))], [BF16], [_sigmoid], tm=512)

    r3 = lambda y: y.reshape(b, s, y.shape[-1])
    o_parts, lse_parts = _band_attention(aq, ak, av, b, s)
    y_b = _retention(r3(bq), r3(bk), r3(bv), r3(bg), p['ret_gn_g'])
    y_c = _diff_attention(r3(cq), r3(ck), r3(cv), p['diff_lambda'], p['diff_subln_g'], lam_init)
    y_d = _ssd(r3(dz), r3(dxbc), r3(ddt), p['ssm_conv_w'], p['ssm_conv_b'], p['ssm_dt_bias'],
               p['ssm_A_log'], p['ssm_D'], p['ssm_norm_g'])
    x2d = _merge(x2d, o_parts, lse_parts, dils, y_b.reshape(t, -1), y_c.reshape(t, -1),
                 y_d.reshape(t, -1), gates, bf(p['w_branch']), bf(p['w_mix_out']))

    xw = X_HEADS * X_HEAD_DIM
    m2d = mem.reshape(-1, d)
    mk, mv = _norm_matmul(m2d, p['norm_mem_g'], [bf(p['w_xkv'][:, :xw]), bf(p['w_xkv'][:, xw:])],
                          [BF16, BF16], [None, None], tm=256)
    mlen = mem.shape[1]
    hid = p['w_ffn_out'].shape[0]
    return _xattn_ffn(x2d.reshape(b, s, d), p['norm_x_g'], bf(p['w_xq']), mk.reshape(b, mlen, xw),
                      mv.reshape(b, mlen, xw), bf(p['w_xo']), p['norm_ffn_g'], bf(p['w_ffn_in'][:, :hid]),
                      bf(p['w_ffn_in'][:, hid:]), bf(p['w_ffn_out']), final_g, final_norm)


def kernel(x, mem, norm_mix_g, w_in, ret_gn_g, diff_lambda, diff_subln_g, ssm_conv_w, ssm_conv_b,
           ssm_dt_bias, ssm_A_log, ssm_D, ssm_norm_g, w_branch, w_mix_out, norm_x_g, norm_mem_g, w_xq,
           w_xkv, w_xo, norm_ffn_g, w_ffn_in, w_ffn_out, norm_f_g):
    stacked = dict(norm_mix_g=norm_mix_g, w_in=w_in, ret_gn_g=ret_gn_g, diff_lambda=diff_lambda,
                   diff_subln_g=diff_subln_g, ssm_conv_w=ssm_conv_w, ssm_conv_b=ssm_conv_b,
                   ssm_dt_bias=ssm_dt_bias, ssm_A_log=ssm_A_log, ssm_D=ssm_D, ssm_norm_g=ssm_norm_g,
                   w_branch=w_branch, w_mix_out=w_mix_out, norm_x_g=norm_x_g, norm_mem_g=norm_mem_g,
                   w_xq=w_xq, w_xkv=w_xkv, w_xo=w_xo, norm_ffn_g=norm_ffn_g, w_ffn_in=w_ffn_in,
                   w_ffn_out=w_ffn_out)
    depth = w_in.shape[0]
    for l in range(depth):
        lam_init = 0.8 - 0.6 * math.exp(-0.3 * l)
        p = {name: val[l] for name, val in stacked.items()}
        x = _layer(x, mem, p, lam_init, norm_f_g, final_norm=(l == depth - 1))
    return x
```

```python
import functools
import math

import jax
import jax.numpy as jnp
import numpy as np
from jax import lax
from jax.experimental import pallas as pl
from jax.experimental.pallas import tpu as pltpu

F32 = jnp.float32
BF16 = jnp.bfloat16

EPS = 1e-6
NEG = -1e30
LOG2E = math.log2(math.e)
BLK = 128

DIL_HEADS = 8
DIL_HEAD_DIM = 64
DIL_PATTERNS = ((128, 1), (512, 4), (2048, 16))
RET_HEADS = 4
RET_KEY_DIM = 64
RET_VAL_DIM = 128
ROPE_BASE = 10000.0
DIFF_HEADS = 4
DIFF_HEAD_DIM = 64
SSM_D_INNER = 512
SSM_HEAD_DIM = 64
SSM_HEADS = 8
SSM_GROUPS = 2
SSM_STATE = 128
SSM_CONV = 4
N_BRANCH = 4
BRANCH_WIDTH = 512
X_HEADS = 4
X_HEAD_DIM = 128
LANES = 128

MIB = 1024 * 1024


def _params(semantics, vmem_mib):
    return pltpu.CompilerParams(dimension_semantics=semantics,
                                vmem_limit_bytes=vmem_mib * MIB)


def _dot(a, b):
    return jnp.dot(a, b, preferred_element_type=F32)


def _dot_nt(a, b):
    return lax.dot_general(a, b, (((1,), (1,)), ((), ())), preferred_element_type=F32)


def _split3(x):
    x1 = x.astype(BF16)
    r1 = x - x1.astype(F32)
    x2 = r1.astype(BF16)
    x3 = (r1 - x2.astype(F32)).astype(BF16)
    return x1, x2, x3


def _dot_sel(x, sel):
    x1, x2, x3 = _split3(x)
    return _dot(x1, sel) + _dot(x2, sel) + _dot(x3, sel)


def _sel_dot(sel, x):
    x1, x2, x3 = _split3(x)
    return _dot(sel, x1) + _dot(sel, x2) + _dot(sel, x3)


def _sigmoid(x):
    return 0.5 * jnp.tanh(0.5 * x) + 0.5


def _silu(x):
    return x * _sigmoid(x)


def _rms(x, g):
    ms = jnp.mean(x * x, axis=-1, keepdims=True)
    return x * lax.rsqrt(ms + EPS) * g


def _norm_matmul_kernel(x_ref, g_ref, *refs, n_out, acts):
    w_refs, o_refs = refs[:n_out], refs[n_out:]
    h = _rms(x_ref[...], g_ref[...]).astype(BF16)
    for w_ref, o_ref, act in zip(w_refs, o_refs, acts):
        y = _dot(h, w_ref[...])
        if act is not None:
            y = act(y)
        o_ref[...] = y.astype(o_ref.dtype)


def _norm_matmul(x2d, g, ws, out_dtypes, acts, tm, vmem_mib=48):
    t, d = x2d.shape
    in_specs = [pl.BlockSpec((tm, d), lambda i: (i, 0)),
                pl.BlockSpec((1, d), lambda i: (0, 0))]
    in_specs += [pl.BlockSpec(w.shape, lambda i: (0, 0)) for w in ws]
    out_specs = [pl.BlockSpec((tm, w.shape[1]), lambda i: (i, 0)) for w in ws]
    out_shape = [jax.ShapeDtypeStruct((t, w.shape[1]), dt) for w, dt in zip(ws, out_dtypes)]
    return pl.pallas_call(
        functools.partial(_norm_matmul_kernel, n_out=len(ws), acts=tuple(acts)),
        grid=(t // tm,), in_specs=in_specs, out_specs=out_specs, out_shape=out_shape,
        compiler_params=_params(("parallel",), vmem_mib),
        name="norm_matmul",
    )(x2d, g.reshape(1, d), *ws)


def _proj_views_kernel(x_ref, g_ref, *refs, n_view, n_plain, dils, view_scales, plain_scales):
    nd = len(dils)
    n_w = n_view + n_plain
    w_refs = refs[:n_w]
    view_refs = refs[n_w:n_w + n_view * nd]
    plain_refs = refs[n_w + n_view * nd:n_w + n_view * nd + n_plain]
    slab_ref = refs[-1]
    tm = x_ref.shape[0]
    h = _rms(x_ref[...], g_ref[...]).astype(BF16)
    for i in range(n_view):
        y = _dot(h, w_refs[i][...])
        if view_scales[i] != 1.0:
            y = y * view_scales[i]
        width = y.shape[1]
        tiles = width // LANES
        for t in range(tiles):
            slab_ref[t] = y[:, t * LANES:(t + 1) * LANES]
        for di, dil in enumerate(dils):
            o_ref = view_refs[i * nd + di]
            if dil == 1:
                o_ref[...] = y.astype(o_ref.dtype)
                continue
            for r in range(dil):
                for t in range(tiles):
                    col = r * width + t * LANES
                    o_ref[:, col:col + LANES] = slab_ref[t, pl.ds(r, tm // dil, stride=dil), :].astype(o_ref.dtype)
    for i in range(n_plain):
        y = _dot(h, w_refs[n_view + i][...])
        if plain_scales[i] != 1.0:
            y = y * plain_scales[i]
        plain_refs[i][...] = y.astype(plain_refs[i].dtype)


def _proj_views(x2d, g, view_ws, view_scales, plain_ws, plain_scales, dils, tm=512):
    t, d = x2d.shape
    ws = list(view_ws) + list(plain_ws)
    full = lambda w: pl.BlockSpec(w.shape, lambda i: (0, 0))
    out_specs, out_shape = [], []
    for w in view_ws:
        n = w.shape[1]
        for dil in dils:
            out_specs.append(pl.BlockSpec((tm // dil, dil * n), lambda i: (i, 0)))
            out_shape.append(jax.ShapeDtypeStruct((t // dil, dil * n), BF16))
    for w in plain_ws:
        out_specs.append(pl.BlockSpec((tm, w.shape[1]), lambda i: (i, 0)))
        out_shape.append(jax.ShapeDtypeStruct((t, w.shape[1]), BF16))
    width = view_ws[0].shape[1]
    return pl.pallas_call(
        functools.partial(_proj_views_kernel, n_view=len(view_ws), n_plain=len(plain_ws), dils=tuple(dils),
                          view_scales=tuple(view_scales), plain_scales=tuple(plain_scales)),
        grid=(t // tm,),
        in_specs=[pl.BlockSpec((tm, d), lambda i: (i, 0)), pl.BlockSpec((1, d), lambda i: (0, 0))]
                 + [full(w) for w in ws],
        out_specs=out_specs, out_shape=out_shape,
        scratch_shapes=[pltpu.VMEM((width // LANES, tm, LANES), F32)],
        compiler_params=_params(("parallel",), 48),
        name="proj_views",
    )(x2d, g.reshape(1, d), *ws)


def _band_attn_kernel(*refs, steps_per_residue):
    n_pat = len(steps_per_residue)
    ins, outs = refs[:6 * n_pat], refs[6 * n_pat:]
    step = pl.program_id(1)
    for i in range(n_pat):
        bias_ref, q_ref, kp_ref, kc_ref, vp_ref, vc_ref = ins[6 * i:6 * i + 6]
        o_ref, lse_ref = outs[2 * i:2 * i + 2]
        first = jnp.minimum(step % steps_per_residue[i], 1)
        lo, hi = slice(0, BLK), slice(BLK, 2 * BLK)
        _band_block(bias_ref[first], q_ref.at[0, lo], kp_ref.at[0], kc_ref.at[0, lo], vp_ref.at[0],
                    vc_ref.at[0, lo], o_ref.at[0, lo], lse_ref.at[0, lo])
        _band_block(bias_ref[1], q_ref.at[0, hi], kc_ref.at[0, lo], kc_ref.at[0, hi], vc_ref.at[0, lo],
                    vc_ref.at[0, hi], o_ref.at[0, hi], lse_ref.at[0, hi])


def _band_block(bias, q_ref, kp_ref, kc_ref, vp_ref, vc_ref, o_ref, lse_ref):
    hd = DIL_HEAD_DIM
    lane = lax.broadcasted_iota(jnp.int32, (1, LANES), 1)
    ones = jnp.ones((2 * BLK, LANES), BF16)
    lse_tile = jnp.zeros((BLK, LANES), F32)
    for t in range(DIL_HEADS * hd // LANES):
        sl = slice(t * LANES, (t + 1) * LANES)
        q = q_ref[:, sl]
        k2 = jnp.concatenate([kp_ref[:, sl], kc_ref[:, sl]], axis=0)
        v2 = jnp.concatenate([vp_ref[:, sl], vc_ref[:, sl]], axis=0)
        v_ext = jnp.concatenate([v2, ones], axis=-1)
        pair = []
        for half in range(LANES // hd):
            in_head = (lane // hd) == half
            s = _dot_nt(jnp.where(in_head, q, jnp.zeros_like(q)), k2) + bias
            m = jnp.max(jnp.maximum(s[:, :BLK], s[:, BLK:]), axis=-1, keepdims=True)
            p = jnp.exp(s - m).astype(BF16)
            pv = _dot(p, v_ext)
            den = pv[:, LANES:]
            pair.append(pv[:, :LANES] / den)
            h = t * (LANES // hd) + half
            per_head = LANES // DIL_HEADS
            lse_tile = jnp.where((lane // per_head) == h, m + jnp.log(den), lse_tile)
        o_ref[:, sl] = jnp.where((lane // hd) == 0, pair[0], pair[1]).astype(o_ref.dtype)
    lse_ref[...] = lse_tile


def _band_bias(window):
    i = np.arange(BLK)[:, None]
    j = np.arange(2 * BLK)[None, :]
    dist = BLK + i - j
    band = (dist >= 0) & (dist <= window)
    first = band & (j >= BLK)
    return np.where(np.stack([first, band]), 0.0, NEG).astype(np.float32)


def _band_attention(qs, ks, vs, batch, seq):
    w = DIL_HEADS * DIL_HEAD_DIM
    steps = seq // (2 * BLK)
    args, in_specs, out_specs, out_shape, per_residue = [], [], [], [], []
    for (window, dil), q, k, v in zip(DIL_PATTERNS, qs, ks, vs):
        l = seq // dil
        ns = l // (2 * BLK)
        per_residue.append(ns)
        view = lambda t, l=l, dil=dil: t.reshape(batch, l, dil * w)
        cur = lambda wd, ns=ns: pl.BlockSpec((1, 2 * BLK, wd), lambda bi, n: (bi, n % ns, n // ns))
        prev = pl.BlockSpec((1, BLK, w),
                            lambda bi, n, ns=ns: (bi, jnp.maximum(2 * (n % ns) - 1, 0), n // ns))
        bias = pl.BlockSpec((2, BLK, 2 * BLK), lambda bi, n: (0, 0, 0))
        args += [_band_bias(window // dil), view(q), view(k), view(k), view(v), view(v)]
        in_specs += [bias, cur(w), prev, cur(w), prev, cur(w)]
        out_specs += [cur(w), cur(LANES)]
        out_shape += [jax.ShapeDtypeStruct((batch, l, dil * w), BF16),
                      jax.ShapeDtypeStruct((batch, l, dil * LANES), F32)]
    outs = pl.pallas_call(
        functools.partial(_band_attn_kernel, steps_per_residue=tuple(per_residue)),
        grid=(batch, steps),
        in_specs=in_specs, out_specs=out_specs, out_shape=out_shape,
        compiler_params=_params(("parallel", "parallel"), 32),
        name="band_attention",
    )(*args)
    flat = [o.reshape(-1, o.shape[-1]) for o in outs]
    return flat[0::2], flat[1::2]


def _retention_kernel(q_ref, k_ref, v_ref, g_ref, cos_ref, sin_ref, dq_ref, dk_ref, dec_ref,
                      cdec_ref, gn_ref, o_ref, state_ref):
    @pl.when(pl.program_id(0) == 0)
    def _():
        state_ref[...] = jnp.zeros_like(state_ref)

    cos, sin = cos_ref[...], sin_ref[...]
    half = RET_HEADS * RET_KEY_DIM // 2

    def rot(t):
        t1, t2 = t[:, :half], t[:, half:]
        return jnp.concatenate([t1 * cos - t2 * sin, t1 * sin + t2 * cos], axis=-1)

    lane_head = (lax.broadcasted_iota(jnp.int32, (1, 2 * half), 1) % half) // (RET_KEY_DIM // 2)
    for bi in range(q_ref.shape[0]):
        rq = rot(q_ref[bi])
        rk = rot(k_ref[bi]) * (RET_KEY_DIM ** -0.5)
        v = v_ref[bi]
        rkb = rk.astype(BF16)
        state = state_ref[bi]
        o_cross = _dot((rq * dq_ref[...]).astype(BF16), state.astype(BF16))
        outs = []
        for h in range(RET_HEADS):
            qh = jnp.where(lane_head == h, rq, 0.0).astype(BF16)
            s = _dot_nt(qh, rkb) * dec_ref[h]
            outs.append(_dot(s.astype(BF16), v[:, h * RET_VAL_DIM:(h + 1) * RET_VAL_DIM]))
        o = jnp.concatenate(outs, axis=-1) + o_cross
        k_end_t = (rk * dk_ref[...]).T.astype(BF16)
        kv = _dot(k_end_t, v)
        row_head = (lax.broadcasted_iota(jnp.int32, kv.shape, 0) % half) // (RET_KEY_DIM // 2)
        col_head = lax.broadcasted_iota(jnp.int32, kv.shape, 1) // RET_VAL_DIM
        state_ref[bi] = state * cdec_ref[...] + jnp.where(row_head == col_head, kv, 0.0)
        normed = []
        for h in range(RET_HEADS):
            oh = o[:, h * RET_VAL_DIM:(h + 1) * RET_VAL_DIM]
            mu = jnp.mean(oh, axis=-1, keepdims=True)
            cen = oh - mu
            var = jnp.mean(cen * cen, axis=-1, keepdims=True)
            normed.append(cen * lax.rsqrt(var + EPS))
        y = jnp.concatenate(normed, axis=-1) * gn_ref[...]
        o_ref[bi] = (g_ref[bi].astype(F32) * y).astype(o_ref.dtype)


def _retention_tables(s):
    h, dk, dv = RET_HEADS, RET_KEY_DIM, RET_VAL_DIM
    quarter = dk // 2
    pos = np.arange(s, dtype=np.float64)
    inv_freq = ROPE_BASE ** (-np.arange(quarter, dtype=np.float64) / quarter)
    ang = pos[:, None] * np.tile(inv_freq, h)[None, :]
    log_gamma = np.log1p(-np.exp2(-5.0 - np.arange(h, dtype=np.float64)))
    idx = np.arange(BLK, dtype=np.float64)
    rel = idx[:, None] - idx[None, :]
    decay = np.where(rel >= 0, np.exp(log_gamma[:, None, None] * np.maximum(rel, 0.0)), 0.0)
    lane_gamma = np.tile(np.repeat(log_gamma, quarter), 2)
    dq = np.exp((idx + 1.0)[:, None] * lane_gamma[None, :])
    dkt = np.exp((BLK - 1 - idx)[:, None] * lane_gamma[None, :])
    cdec = np.repeat(np.exp(log_gamma * BLK), dv)[None, :]
    return tuple(t.astype(np.float32) for t in (np.cos(ang), np.sin(ang), dq, dkt, decay, cdec))


def _retention_call(q, k, v, g, gn_g):
    b, s, wk = q.shape
    wv = v.shape[-1]
    cos, sin, dq, dkt, decay, cdec = _retention_tables(s)
    tok = lambda w: pl.BlockSpec((b, BLK, w), lambda c: (0, c, 0))
    const = lambda shape: pl.BlockSpec(shape, lambda c: (0,) * len(shape))
    in_specs = [tok(wk), tok(wk), tok(wv), tok(wv),
                pl.BlockSpec((BLK, wk // 2), lambda c: (c, 0)),
                pl.BlockSpec((BLK, wk // 2), lambda c: (c, 0)),
                const((BLK, wk)), const((BLK, wk)), const((RET_HEADS, BLK, BLK)),
                const((1, wv)), const((1, wv))]
    args = [q, k, v, g, cos, sin, dq, dkt, decay, cdec, gn_g.reshape(1, wv)]
    return (args, in_specs, tok(wv), jax.ShapeDtypeStruct((b, s, wv), BF16),
            [pltpu.VMEM((b, wk, wv), F32)])


def _recurrent_mixers_kernel(*refs, n_ret_in, n_ssd_in):
    ret_in, ssd_in = refs[:n_ret_in], refs[n_ret_in:n_ret_in + n_ssd_in]
    ret_out, ssd_out = refs[n_ret_in + n_ssd_in:n_ret_in + n_ssd_in + 2]
    scratch = refs[n_ret_in + n_ssd_in + 2:]
    _retention_kernel(*ret_in, ret_out, scratch[0])
    _ssd_kernel(*ssd_in, ssd_out, *scratch[1:])


def _recurrent_mixers(ret_call, ssd_call, steps):
    (ra, rs, ro, rshape, rscr), (sa, ss, so, sshape, sscr) = ret_call, ssd_call
    return pl.pallas_call(
        functools.partial(_recurrent_mixers_kernel, n_ret_in=len(ra), n_ssd_in=len(sa)),
        grid=(steps,),
        in_specs=rs + ss, out_specs=[ro, so], out_shape=[rshape, sshape],
        scratch_shapes=rscr + sscr,
        compiler_params=_params(("arbitrary",), 40),
        name="recurrent_mixers",
    )(*ra, *sa)


def _diff_attn_kernel(lam_ref, g_ref, q_ref, k_ref, v_ref, o_ref, s_ref, m_ref, vext_ref, acc_ref, *,
                      tq, lam_init):
    d = DIFF_HEAD_DIM
    hw = 2 * d
    nl = tq // LANES
    qi = pl.program_id(2)
    q = q_ref[0]
    qs = (q[:, :d], q[:, d:])

    @pl.when(qi == 0)
    def _():
        vext_ref[:, :hw] = v_ref[0]
        vext_ref[:, hw:] = jnp.ones((vext_ref.shape[0], hw), BF16)

    def fold_max(x):
        r = x[:, :LANES]
        for c in range(1, nl):
            r = jnp.maximum(r, x[:, c * LANES:(c + 1) * LANES])
        return r

    m_ref[...] = jnp.full(m_ref.shape, NEG, F32)

    def scores(j, diagonal):
        k = k_ref[0, pl.ds(pl.multiple_of(j * tq, tq), tq), :]
        for c in range(2):
            s = _dot_nt(qs[c], k[:, c * d:(c + 1) * d]) * LOG2E
            if diagonal:
                row = lax.broadcasted_iota(jnp.int32, (tq, tq), 0)
                col = lax.broadcasted_iota(jnp.int32, (tq, tq), 1)
                s = jnp.where(col <= row, s, NEG)
            s_ref[c, j] = s
            m_ref[c] = jnp.maximum(m_ref[c], fold_max(s))

    def for_each_block(n, body):
        def quad(jj, carry):
            for u in range(4):
                body(4 * jj + u)
            return carry

        lax.fori_loop(0, lax.shift_right_logical(n, 2), quad, 0)
        done = n & ~3

        @pl.when((n & 2) == 2)
        def _():
            body(done)
            body(done + 1)

        @pl.when((n & 1) == 1)
        def _():
            body(n - 1)

    for_each_block(qi, lambda j: scores(j, False))
    scores(qi, True)

    for c in range(2):
        m_ref[c] = jnp.broadcast_to(jnp.max(m_ref[c], axis=-1, keepdims=True), (tq, LANES))
    acc_ref[...] = jnp.zeros(acc_ref.shape, F32)

    def accumulate(j):
        v = vext_ref[pl.ds(pl.multiple_of(j * tq, tq), tq), :]
        for c in range(2):
            m = m_ref[c]
            s = s_ref[c, j]
            p = jnp.concatenate([jnp.exp2(s[:, t * LANES:(t + 1) * LANES] - m) for t in range(nl)], axis=-1)
            acc_ref[c] += _dot(p.astype(BF16), v)

    for_each_block(qi + 1, accumulate)

    lp = lam_ref[...]
    lam = (jnp.exp(jnp.sum(lp[0:1] * lp[1:2], keepdims=True))
           - jnp.exp(jnp.sum(lp[2:3] * lp[3:4], keepdims=True)) + lam_init)
    o1 = acc_ref[0, :, :hw] / acc_ref[0, :, hw:hw + 1]
    o2 = acc_ref[1, :, :hw] / acc_ref[1, :, hw:hw + 1]
    o_ref[0] = (_rms(o1 - lam * o2, g_ref[...]) * (1.0 - lam_init)).astype(o_ref.dtype)


def _diff_attention(q, k, v, lam_params, subln_g, lam_init, tq=512):
    b, s, w = q.shape
    hw = 2 * DIFF_HEAD_DIM
    return pl.pallas_call(
        functools.partial(_diff_attn_kernel, tq=tq, lam_init=lam_init),
        grid=(b, DIFF_HEADS, s // tq),
        in_specs=[pl.BlockSpec(lam_params.shape, lambda bi, h, i: (0, 0)),
                  pl.BlockSpec((1, hw), lambda bi, h, i: (0, 0)),
                  pl.BlockSpec((1, tq, hw), lambda bi, h, i: (bi, i, h)),
                  pl.BlockSpec((1, s, hw), lambda bi, h, i: (bi, 0, h)),
                  pl.BlockSpec((1, s, hw), lambda bi, h, i: (bi, 0, h))],
        out_specs=pl.BlockSpec((1, tq, hw), lambda bi, h, i: (bi, i, h)),
        out_shape=jax.ShapeDtypeStruct((b, s, w), BF16),
        scratch_shapes=[pltpu.VMEM((2, s // tq, tq, tq), F32), pltpu.VMEM((2, tq, LANES), F32),
                        pltpu.VMEM((s, 2 * hw), BF16), pltpu.VMEM((2, tq, 2 * hw), F32)],
        compiler_params=_params(("parallel", "parallel", "arbitrary"), 40),
        name="diff_attention",
    )(lam_params, subln_g.reshape(1, hw), q, k, v)


def _ssd_kernel(z_ref, xbc_ref, dt_ref, cw_ref, cb_ref, dtb_ref, alog_ref, dskip_ref, ng_ref,
                tri_ref, exp_ref, o_ref, carry_ref, ext_ref, state_ref):
    gn = SSM_GROUPS * SSM_STATE
    hpg = SSM_HEADS // SSM_GROUPS
    gw = hpg * SSM_HEAD_DIM
    tail = 8

    @pl.when(pl.program_id(0) == 0)
    def _():
        carry_ref[...] = jnp.zeros_like(carry_ref)
        state_ref[...] = jnp.zeros_like(state_ref)

    cw = cw_ref[...]
    lane = lax.broadcasted_iota(jnp.int32, (1, LANES), 1)
    a_neg = jnp.where(lane < SSM_HEADS, -jnp.exp(alog_ref[...]), 0.0)
    expand = exp_ref[...]
    row = lax.broadcasted_iota(jnp.int32, (BLK, BLK), 0)
    col = lax.broadcasted_iota(jnp.int32, (BLK, BLK), 1)
    col_head = lax.broadcasted_iota(jnp.int32, (1, gw), 1) // SSM_HEAD_DIM
    for bi in range(z_ref.shape[0]):
        xbc = xbc_ref[bi]
        ext_ref[bi, 0:tail, :] = carry_ref[bi]
        ext_ref[bi, tail:tail + BLK, :] = xbc
        carry_ref[bi] = xbc[BLK - tail:, :]
        conv = cb_ref[...] + cw[SSM_CONV - 1:SSM_CONV, :] * xbc
        for sft in range(1, SSM_CONV):
            conv = conv + (cw[SSM_CONV - 1 - sft:SSM_CONV - sft, :]
                           * ext_ref[bi, tail - sft:tail - sft + BLK, :])
        xc = _silu(conv)
        xs, bm, cm = xc[:, :SSM_D_INNER], xc[:, SSM_D_INNER:SSM_D_INNER + gn], xc[:, SSM_D_INNER + gn:]

        pre = dt_ref[bi] + dtb_ref[...]
        dt = jnp.maximum(pre, 0.0) + jnp.log1p(jnp.exp(-jnp.abs(pre)))
        a_col = _sel_dot(tri_ref[...], dt * a_neg)
        a_row = a_col.T
        a_exp = _dot_sel(a_col, expand)
        dt_exp = _dot_sel(dt, expand)
        total = a_exp[BLK - 1:BLK, :]
        xdt = xs * dt_exp
        x_end = (xdt * jnp.exp(total - a_exp)).astype(BF16)
        xdt_b = xdt.astype(BF16)
        ys = []
        for g in range(SSM_GROUPS):
            b_g = bm[:, g * SSM_STATE:(g + 1) * SSM_STATE]
            c_g = cm[:, g * SSM_STATE:(g + 1) * SSM_STATE].astype(BF16)
            cb = _dot_nt(c_g, b_g.astype(BF16))
            xg = xdt_b[:, g * gw:(g + 1) * gw]
            y_g = jnp.zeros((BLK, gw), F32)
            for hh in range(hpg):
                h = g * hpg + hh
                seg = a_col[:, h:h + 1] - a_row[h:h + 1, :]
                m_h = (cb * jnp.exp(jnp.where(col <= row, seg, NEG))).astype(BF16)
                y_g = y_g + _dot(m_h, jnp.where(col_head == hh, xg, jnp.zeros_like(xg)))
            state = state_ref[bi, g]
            y_g = y_g + _dot(c_g, state.astype(BF16)) * jnp.exp(a_exp[:, g * gw:(g + 1) * gw])
            new = _dot(b_g.T.astype(BF16), x_end[:, g * gw:(g + 1) * gw])
            state_ref[bi, g] = state * jnp.exp(total[:, g * gw:(g + 1) * gw]) + new
            ys.append(y_g)
        y = jnp.concatenate(ys, axis=-1) + xs * dskip_ref[...]
        y = y * z_ref[bi].astype(F32)
        o_ref[bi] = _rms(y, ng_ref[...]).astype(o_ref.dtype)


def _ssd_call(z, xbc, dt, conv_w, conv_b, dt_bias, a_log, d_skip, norm_g):
    b, s, c = xbc.shape
    di = SSM_D_INNER
    pad = lambda t: jnp.pad(t.astype(F32), (0, LANES - t.shape[0])).reshape(1, LANES)
    tri = jnp.asarray(np.tril(np.ones((BLK, BLK), np.float32)), BF16)
    expand = jnp.asarray(np.arange(LANES)[:, None] == (np.arange(di) // SSM_HEAD_DIM)[None, :], BF16)
    tok = lambda w: pl.BlockSpec((b, BLK, w), lambda ci: (0, ci, 0))
    const = lambda shape: pl.BlockSpec(shape, lambda ci: (0,) * len(shape))
    in_specs = [tok(di), tok(c), tok(LANES), const((SSM_CONV, c)), const((1, c)),
                const((1, LANES)), const((1, LANES)), const((1, di)), const((1, di)),
                const((BLK, BLK)), const((LANES, di))]
    args = [z, xbc, dt, conv_w, conv_b.reshape(1, c), pad(dt_bias), pad(a_log),
            jnp.repeat(d_skip.astype(F32), SSM_HEAD_DIM).reshape(1, di), norm_g.reshape(1, di), tri, expand]
    scratch = [pltpu.VMEM((b, 8, c), F32), pltpu.VMEM((b, 8 + BLK, c), F32),
               pltpu.VMEM((b, SSM_GROUPS, SSM_STATE, di // SSM_GROUPS), F32)]
    return args, in_specs, tok(di), jax.ShapeDtypeStruct((b, s, di), BF16), scratch


def _from_residue_view(src_ref, slab_ref, dil, width):
    if dil == 1:
        return src_ref[...].astype(F32)
    rows = src_ref.shape[0]
    tiles = width // LANES
    for r in range(dil):
        for t in range(tiles):
            col = r * width + t * LANES
            slab_ref[t, pl.ds(r, rows, stride=dil), :] = src_ref[:, col:col + LANES].astype(F32)
    return jnp.concatenate([slab_ref[t] for t in range(tiles)], axis=-1)


def _merge_kernel(x_ref, *refs, dils):
    n = len(dils)
    o_refs, l_refs = refs[:n], refs[n:2 * n]
    yb_ref, yc_ref, yd_ref, gate_ref, wb_ref, wo_ref, hexp_ref, out_ref = refs[2 * n:2 * n + 8]
    o_slabs, l_slabs = refs[2 * n + 8:3 * n + 8], refs[3 * n + 8:]
    d = x_ref.shape[-1]
    bw = yb_ref.shape[-1]
    lses = [_from_residue_view(l_refs[i], l_slabs[i], dils[i], LANES) for i in range(n)]
    m = functools.reduce(jnp.maximum, lses)
    es = [jnp.exp(l - m) for l in lses]
    inv = 1.0 / functools.reduce(jnp.add, es)
    ya = jnp.zeros((x_ref.shape[0], bw), F32)
    for i in range(n):
        wt = _dot((es[i] * inv).astype(BF16), hexp_ref[...])
        ya = ya + wt * _from_residue_view(o_refs[i], o_slabs[i], dils[i], bw)
    branches = (ya.astype(BF16), yb_ref[...], yc_ref[...], yd_ref[...])
    merged = jnp.zeros((x_ref.shape[0], d), F32)
    for i, y in enumerate(branches):
        gate = gate_ref[:, i * d:(i + 1) * d].astype(F32)
        merged = merged + gate * _dot(y, wb_ref[i])
    out_ref[...] = x_ref[...] + _dot(merged.astype(BF16), wo_ref[...])


def _merge(x2d, o_parts, lse_parts, dils, yb, yc, yd, gates, w_branch, w_out, tm=256):
    t, d = x2d.shape
    bw = yb.shape[-1]
    row = lambda w: pl.BlockSpec((tm, w), lambda i: (i, 0))
    view = lambda w, dil: pl.BlockSpec((tm // dil, dil * w), lambda i: (i, 0))
    per_head = LANES // DIL_HEADS
    hexp = jnp.asarray(np.arange(LANES)[:, None] == (np.arange(bw) // DIL_HEAD_DIM * per_head)[None, :],
                       BF16)
    return pl.pallas_call(
        functools.partial(_merge_kernel, dils=tuple(dils)),
        grid=(t // tm,),
        in_specs=[row(d)] + [view(bw, dil) for dil in dils] + [view(LANES, dil) for dil in dils]
                 + [row(bw)] * 3 + [row(N_BRANCH * d),
                                    pl.BlockSpec(w_branch.shape, lambda i: (0, 0, 0)),
                                    pl.BlockSpec(w_out.shape, lambda i: (0, 0)),
                                    pl.BlockSpec(hexp.shape, lambda i: (0, 0))],
        out_specs=row(d),
        out_shape=jax.ShapeDtypeStruct((t, d), F32),
        scratch_shapes=[pltpu.VMEM((bw // LANES, tm, LANES), F32) for _ in dils]
                       + [pltpu.VMEM((1, tm, LANES), F32) for _ in dils],
        compiler_params=_params(("parallel",), 48),
        name="merge",
    )(x2d, *o_parts, *lse_parts, yb, yc, yd, gates, w_branch, w_out, hexp)


def _xattn_ffn_kernel(x_ref, gx_ref, wq_ref, k_ref, v_ref, wxo_ref, gf_ref, wa_ref, wb_ref, wo_ref,
                      fg_ref, out_ref, *, final_norm):
    x = x_ref[0]
    q = _dot(_rms(x, gx_ref[...]).astype(BF16), wq_ref[...]).astype(BF16)
    k, v = k_ref[0], v_ref[0]
    outs = []
    for h in range(X_HEADS):
        sl = slice(h * X_HEAD_DIM, (h + 1) * X_HEAD_DIM)
        s = _dot_nt(q[:, sl], k[:, sl]) * (X_HEAD_DIM ** -0.5)
        p = jnp.exp(s - jnp.max(s, axis=-1, keepdims=True))
        a = p / jnp.sum(p, axis=-1, keepdims=True)
        outs.append(_dot(a.astype(BF16), v[:, sl]))
    x = x + _dot(jnp.concatenate(outs, axis=-1).astype(BF16), wxo_ref[...])
    h = _rms(x, gf_ref[...]).astype(BF16)
    a = _dot(h, wa_ref[...])
    b = _dot(h, wb_ref[...])
    y = x + _dot((_silu(a) * b).astype(BF16), wo_ref[...])
    if final_norm:
        y = _rms(y, fg_ref[...])
    out_ref[0] = y


def _xattn_ffn(x, gx, w_q, k, v, w_xo, gf, w_a, w_b, w_o, final_g, final_norm, tm=512):
    b, s, d = x.shape
    m, w = k.shape[1], k.shape[2]
    full = lambda wt: pl.BlockSpec(wt.shape, lambda bi, i: (0, 0), pipeline_mode=pl.Buffered(1))
    vec = pl.BlockSpec((1, d), lambda bi, i: (0, 0))
    mem = pl.BlockSpec((1, m, w), lambda bi, i: (bi, 0, 0))
    return pl.pallas_call(
        functools.partial(_xattn_ffn_kernel, final_norm=final_norm),
        grid=(b, s // tm),
        in_specs=[pl.BlockSpec((1, tm, d), lambda bi, i: (bi, i, 0)), vec, full(w_q), mem, mem,
                  full(w_xo), vec, full(w_a), full(w_b), full(w_o), vec],
        out_specs=pl.BlockSpec((1, tm, d), lambda bi, i: (bi, i, 0)),
        out_shape=jax.ShapeDtypeStruct((b, s, d), F32),
        compiler_params=_params(("parallel", "parallel"), 56),
        name="xattn_ffn",
    )(x, gx.reshape(1, d), w_q, k, v, w_xo, gf.reshape(1, d), w_a, w_b, w_o, final_g.reshape(1, d))


def _ret_feature_order():
    quarter = RET_KEY_DIM // 2
    first = [h * RET_KEY_DIM + f for h in range(RET_HEADS) for f in range(quarter)]
    return np.array(first + [c + quarter for c in first])


def _layer(x, mem, p, lam_init, final_g, final_norm):
    b, s, d = x.shape
    t = b * s
    x2d = x.reshape(t, d)
    a_w = DIL_HEADS * DIL_HEAD_DIM
    b_qk, b_vg = RET_HEADS * RET_KEY_DIM, RET_HEADS * RET_VAL_DIM
    c_w = DIFF_HEADS * 2 * DIFF_HEAD_DIM
    d_xbc = SSM_D_INNER + 2 * SSM_GROUPS * SSM_STATE
    sizes = (a_w, a_w, a_w, b_qk, b_qk, b_vg, b_vg, c_w, c_w, c_w, SSM_D_INNER, d_xbc, SSM_HEADS,
             N_BRANCH * d)
    offs = np.concatenate([[0], np.cumsum(sizes)])
    w_in = p['w_in']
    col = lambda i: w_in[:, offs[i]:offs[i + 1]]
    order = _ret_feature_order()
    w_dt = jnp.pad(col(12), ((0, 0), (0, LANES - SSM_HEADS)))
    bf = lambda w: w.astype(BF16)

    dils = [dil for _, dil in DIL_PATTERNS]
    nd = len(dils)
    proj = _proj_views(x2d, p['norm_mix_g'], [bf(col(i)) for i in (0, 1, 2)],
                       [DIL_HEAD_DIM ** -0.5, 1.0, 1.0], [bf(col(i)) for i in (7, 8, 9)],
                       [DIFF_HEAD_DIM ** -0.5, 1.0, 1.0], dils)
    aq, ak, av = proj[0:nd], proj[nd:2 * nd], proj[2 * nd:3 * nd]
    cq, ck, cv = proj[3 * nd:]
    bq, bk, bv, bg, dz, dxbc, ddt = _norm_matmul(
        x2d, p['norm_mix_g'],
        [bf(col(3)[:, order]), bf(col(4)[:, order]), bf(col(5)), bf(col(6)), bf(col(10)), bf(col(11)),
         bf(w_dt)],
        [F32, F32, BF16, BF16, BF16, F32, F32], [None, None, None, _silu, _silu, None, None], tm=512)
    (gates,) = _norm_matmul(x2d, p['norm_mix_g'], [bf(col(13))], [BF16], [_sigmoid], tm=512)

    r3 = lambda y: y.reshape(b, s, y.shape[-1])
    o_parts, lse_parts = _band_attention(aq, ak, av, b, s)
    y_b, y_d = _recurrent_mixers(
        _retention_call(r3(bq), r3(bk), r3(bv), r3(bg), p['ret_gn_g']),
        _ssd_call(r3(dz), r3(dxbc), r3(ddt), p['ssm_conv_w'], p['ssm_conv_b'], p['ssm_dt_bias'],
                  p['ssm_A_log'], p['ssm_D'], p['ssm_norm_g']), s // BLK)
    y_c = _diff_attention(r3(cq), r3(ck), r3(cv), p['diff_lambda'], p['diff_subln_g'], lam_init)
    x2d = _merge(x2d, o_parts, lse_parts, dils, y_b.reshape(t, -1), y_c.reshape(t, -1),
                 y_d.reshape(t, -1), gates, bf(p['w_branch']), bf(p['w_mix_out']))

    xw = X_HEADS * X_HEAD_DIM
    m2d = mem.reshape(-1, d)
    mk, mv = _norm_matmul(m2d, p['norm_mem_g'], [bf(p['w_xkv'][:, :xw]), bf(p['w_xkv'][:, xw:])],
                          [BF16, BF16], [None, None], tm=256)
    mlen = mem.shape[1]
    hid = p['w_ffn_out'].shape[0]
    return _xattn_ffn(x2d.reshape(b, s, d), p['norm_x_g'], bf(p['w_xq']), mk.reshape(b, mlen, xw),
                      mv.reshape(b, mlen, xw), bf(p['w_xo']), p['norm_ffn_g'], bf(p['w_ffn_in'][:, :hid]),
                      bf(p['w_ffn_in'][:, hid:]), bf(p['w_ffn_out']), final_g, final_norm)


def kernel(x, mem, norm_mix_g, w_in, ret_gn_g, diff_lambda, diff_subln_g, ssm_conv_w, ssm_conv_b,
           ssm_dt_bias, ssm_A_log, ssm_D, ssm_norm_g, w_branch, w_mix_out, norm_x_g, norm_mem_g, w_xq,
           w_xkv, w_xo, norm_ffn_g, w_ffn_in, w_ffn_out, norm_f_g):
    stacked = dict(norm_mix_g=norm_mix_g, w_in=w_in, ret_gn_g=ret_gn_g, diff_lambda=diff_lambda,
                   diff_subln_g=diff_subln_g, ssm_conv_w=ssm_conv_w, ssm_conv_b=ssm_conv_b,
                   ssm_dt_bias=ssm_dt_bias, ssm_A_log=ssm_A_log, ssm_D=ssm_D, ssm_norm_g=ssm_norm_g,
                   w_branch=w_branch, w_mix_out=w_mix_out, norm_x_g=norm_x_g, norm_mem_g=norm_mem_g,
                   w_xq=w_xq, w_xkv=w_xkv, w_xo=w_xo, norm_ffn_g=norm_ffn_g, w_ffn_in=w_ffn_in,
                   w_ffn_out=w_ffn_out)
    depth = w_in.shape[0]
    for l in range(depth):
        lam_init = 0.8 - 0.6 * math.exp(-0.3 * l)
        p = {name: val[l] for name, val in stacked.items()}
        x = _layer(x, mem, p, lam_init, norm_f_g, final_norm=(l == depth - 1))
    return x
```

```python
import functools
import math

import jax
import jax.numpy as jnp
import numpy as np
from jax import lax
from jax.experimental import pallas as pl
from jax.experimental.pallas import tpu as pltpu

F32 = jnp.float32
BF16 = jnp.bfloat16

EPS = 1e-6
NEG = -1e30
LOG2E = math.log2(math.e)
BLK = 128

DIL_HEADS = 8
DIL_HEAD_DIM = 64
DIL_PATTERNS = ((128, 1), (512, 4), (2048, 16))
RET_HEADS = 4
RET_KEY_DIM = 64
RET_VAL_DIM = 128
ROPE_BASE = 10000.0
DIFF_HEADS = 4
DIFF_HEAD_DIM = 64
SSM_D_INNER = 512
SSM_HEAD_DIM = 64
SSM_HEADS = 8
SSM_GROUPS = 2
SSM_STATE = 128
SSM_CONV = 4
N_BRANCH = 4
BRANCH_WIDTH = 512
X_HEADS = 4
X_HEAD_DIM = 128
LANES = 128

MIB = 1024 * 1024


def _params(semantics, vmem_mib):
    return pltpu.CompilerParams(dimension_semantics=semantics,
                                vmem_limit_bytes=vmem_mib * MIB)


def _dot(a, b):
    return jnp.dot(a, b, preferred_element_type=F32)


def _dot_nt(a, b):
    return lax.dot_general(a, b, (((1,), (1,)), ((), ())), preferred_element_type=F32)


def _split3(x):
    x1 = x.astype(BF16)
    r1 = x - x1.astype(F32)
    x2 = r1.astype(BF16)
    x3 = (r1 - x2.astype(F32)).astype(BF16)
    return x1, x2, x3


def _dot_sel(x, sel):
    x1, x2, x3 = _split3(x)
    return _dot(x1, sel) + _dot(x2, sel) + _dot(x3, sel)


def _sel_dot(sel, x):
    x1, x2, x3 = _split3(x)
    return _dot(sel, x1) + _dot(sel, x2) + _dot(sel, x3)


def _sigmoid(x):
    return 0.5 * jnp.tanh(0.5 * x) + 0.5


def _silu(x):
    h = 0.5 * x
    return h * jnp.tanh(h) + h


def _rms(x, g):
    ms = jnp.mean(x * x, axis=-1, keepdims=True)
    return x * lax.rsqrt(ms + EPS) * g


def _norm_matmul_kernel(x_ref, g_ref, *refs, n_out, acts):
    w_refs, o_refs = refs[:n_out], refs[n_out:]
    h = _rms(x_ref[...], g_ref[...]).astype(BF16)
    for w_ref, o_ref, act in zip(w_refs, o_refs, acts):
        y = _dot(h, w_ref[...])
        if act is not None:
            y = act(y)
        o_ref[...] = y.astype(o_ref.dtype)


def _norm_matmul(x2d, g, ws, out_dtypes, acts, tm, vmem_mib=48):
    t, d = x2d.shape
    in_specs = [pl.BlockSpec((tm, d), lambda i: (i, 0)),
                pl.BlockSpec((1, d), lambda i: (0, 0))]
    in_specs += [pl.BlockSpec(w.shape, lambda i: (0, 0)) for w in ws]
    out_specs = [pl.BlockSpec((tm, w.shape[1]), lambda i: (i, 0)) for w in ws]
    out_shape = [jax.ShapeDtypeStruct((t, w.shape[1]), dt) for w, dt in zip(ws, out_dtypes)]
    return pl.pallas_call(
        functools.partial(_norm_matmul_kernel, n_out=len(ws), acts=tuple(acts)),
        grid=(t // tm,), in_specs=in_specs, out_specs=out_specs, out_shape=out_shape,
        compiler_params=_params(("parallel",), vmem_mib),
        name="norm_matmul",
    )(x2d, g.reshape(1, d), *ws)


def _proj_views_kernel(x_ref, g_ref, *refs, n_view, n_plain, dils, view_scales, plain_scales):
    nd = len(dils)
    n_w = n_view + n_plain
    w_refs = refs[:n_w]
    view_refs = refs[n_w:n_w + n_view * nd]
    plain_refs = refs[n_w + n_view * nd:n_w + n_view * nd + n_plain]
    slab_ref = refs[-1]
    tm = x_ref.shape[0]
    h = _rms(x_ref[...], g_ref[...]).astype(BF16)
    for i in range(n_view):
        y = _dot(h, w_refs[i][...])
        if view_scales[i] != 1.0:
            y = y * view_scales[i]
        width = y.shape[1]
        tiles = width // LANES
        for t in range(tiles):
            slab_ref[t] = y[:, t * LANES:(t + 1) * LANES]
        for di, dil in enumerate(dils):
            o_ref = view_refs[i * nd + di]
            if dil == 1:
                o_ref[...] = y.astype(o_ref.dtype)
                continue
            for r in range(dil):
                for t in range(tiles):
                    col = r * width + t * LANES
                    o_ref[:, col:col + LANES] = slab_ref[t, pl.ds(r, tm // dil, stride=dil), :].astype(o_ref.dtype)
    for i in range(n_plain):
        y = _dot(h, w_refs[n_view + i][...])
        if plain_scales[i] != 1.0:
            y = y * plain_scales[i]
        plain_refs[i][...] = y.astype(plain_refs[i].dtype)


def _proj_views(x2d, g, view_ws, view_scales, plain_ws, plain_scales, dils, tm=512):
    t, d = x2d.shape
    ws = list(view_ws) + list(plain_ws)
    full = lambda w: pl.BlockSpec(w.shape, lambda i: (0, 0))
    out_specs, out_shape = [], []
    for w in view_ws:
        n = w.shape[1]
        for dil in dils:
            out_specs.append(pl.BlockSpec((tm // dil, dil * n), lambda i: (i, 0)))
            out_shape.append(jax.ShapeDtypeStruct((t // dil, dil * n), BF16))
    for w in plain_ws:
        out_specs.append(pl.BlockSpec((tm, w.shape[1]), lambda i: (i, 0)))
        out_shape.append(jax.ShapeDtypeStruct((t, w.shape[1]), BF16))
    width = view_ws[0].shape[1]
    return pl.pallas_call(
        functools.partial(_proj_views_kernel, n_view=len(view_ws), n_plain=len(plain_ws), dils=tuple(dils),
                          view_scales=tuple(view_scales), plain_scales=tuple(plain_scales)),
        grid=(t // tm,),
        in_specs=[pl.BlockSpec((tm, d), lambda i: (i, 0)), pl.BlockSpec((1, d), lambda i: (0, 0))]
                 + [full(w) for w in ws],
        out_specs=out_specs, out_shape=out_shape,
        scratch_shapes=[pltpu.VMEM((width // LANES, tm, LANES), F32)],
        compiler_params=_params(("parallel",), 48),
        name="proj_views",
    )(x2d, g.reshape(1, d), *ws)


def _band_attn_kernel(*refs, steps_per_residue):
    n_pat = len(steps_per_residue)
    ins, outs = refs[:6 * n_pat], refs[6 * n_pat:]
    step = pl.program_id(1)
    for i in range(n_pat):
        bias_ref, q_ref, kp_ref, kc_ref, vp_ref, vc_ref = ins[6 * i:6 * i + 6]
        o_ref, lse_ref = outs[2 * i:2 * i + 2]
        first = jnp.minimum(step % steps_per_residue[i], 1)
        lo, hi = slice(0, BLK), slice(BLK, 2 * BLK)
        _band_block(bias_ref[first], q_ref.at[0, lo], kp_ref.at[0], kc_ref.at[0, lo], vp_ref.at[0],
                    vc_ref.at[0, lo], o_ref.at[0, lo], lse_ref.at[0, lo])
        _band_block(bias_ref[1], q_ref.at[0, hi], kc_ref.at[0, lo], kc_ref.at[0, hi], vc_ref.at[0, lo],
                    vc_ref.at[0, hi], o_ref.at[0, hi], lse_ref.at[0, hi])


def _band_block(bias, q_ref, kp_ref, kc_ref, vp_ref, vc_ref, o_ref, lse_ref):
    hd = DIL_HEAD_DIM
    lane = lax.broadcasted_iota(jnp.int32, (1, LANES), 1)
    ones = jnp.ones((2 * BLK, LANES), BF16)
    lse_tile = jnp.zeros((BLK, LANES), F32)
    for t in range(DIL_HEADS * hd // LANES):
        sl = slice(t * LANES, (t + 1) * LANES)
        q = q_ref[:, sl]
        k2 = jnp.concatenate([kp_ref[:, sl], kc_ref[:, sl]], axis=0)
        v2 = jnp.concatenate([vp_ref[:, sl], vc_ref[:, sl]], axis=0)
        v_ext = jnp.concatenate([v2, ones], axis=-1)
        pair = []
        for half in range(LANES // hd):
            in_head = (lane // hd) == half
            s = _dot_nt(jnp.where(in_head, q, jnp.zeros_like(q)), k2) + bias
            m = jnp.max(jnp.maximum(s[:, :BLK], s[:, BLK:]), axis=-1, keepdims=True)
            p = jnp.exp((s - m).astype(BF16))
            pv = _dot(p, v_ext)
            den = pv[:, LANES:]
            pair.append(pv[:, :LANES] / den)
            h = t * (LANES // hd) + half
            per_head = LANES // DIL_HEADS
            lse_tile = jnp.where((lane // per_head) == h, m + jnp.log(den), lse_tile)
        o_ref[:, sl] = jnp.where((lane // hd) == 0, pair[0], pair[1]).astype(o_ref.dtype)
    lse_ref[...] = lse_tile


def _band_bias(window):
    i = np.arange(BLK)[:, None]
    j = np.arange(2 * BLK)[None, :]
    dist = BLK + i - j
    band = (dist >= 0) & (dist <= window)
    first = band & (j >= BLK)
    return np.where(np.stack([first, band]), 0.0, NEG).astype(np.float32)


def _band_attention(qs, ks, vs, batch, seq):
    w = DIL_HEADS * DIL_HEAD_DIM
    steps = seq // (2 * BLK)
    args, in_specs, out_specs, out_shape, per_residue = [], [], [], [], []
    for (window, dil), q, k, v in zip(DIL_PATTERNS, qs, ks, vs):
        l = seq // dil
        ns = l // (2 * BLK)
        per_residue.append(ns)
        view = lambda t, l=l, dil=dil: t.reshape(batch, l, dil * w)
        cur = lambda wd, ns=ns: pl.BlockSpec((1, 2 * BLK, wd), lambda bi, n: (bi, n % ns, n // ns))
        prev = pl.BlockSpec((1, BLK, w),
                            lambda bi, n, ns=ns: (bi, jnp.maximum(2 * (n % ns) - 1, 0), n // ns))
        bias = pl.BlockSpec((2, BLK, 2 * BLK), lambda bi, n: (0, 0, 0))
        args += [_band_bias(window // dil), view(q), view(k), view(k), view(v), view(v)]
        in_specs += [bias, cur(w), prev, cur(w), prev, cur(w)]
        out_specs += [cur(w), cur(LANES)]
        out_shape += [jax.ShapeDtypeStruct((batch, l, dil * w), BF16),
                      jax.ShapeDtypeStruct((batch, l, dil * LANES), F32)]
    outs = pl.pallas_call(
        functools.partial(_band_attn_kernel, steps_per_residue=tuple(per_residue)),
        grid=(batch, steps),
        in_specs=in_specs, out_specs=out_specs, out_shape=out_shape,
        compiler_params=_params(("parallel", "parallel"), 32),
        name="band_attention",
    )(*args)
    flat = [o.reshape(-1, o.shape[-1]) for o in outs]
    return flat[0::2], flat[1::2]


def _retention_kernel(q_ref, k_ref, v_ref, g_ref, cos_ref, sin_ref, dq_ref, dk_ref, dec_ref,
                      cdec_ref, gn_ref, o_ref, state_ref):
    @pl.when(pl.program_id(0) == 0)
    def _():
        state_ref[...] = jnp.zeros_like(state_ref)

    cos, sin = cos_ref[...], sin_ref[...]
    half = RET_HEADS * RET_KEY_DIM // 2

    def rot(t):
        t1, t2 = t[:, :half], t[:, half:]
        return jnp.concatenate([t1 * cos - t2 * sin, t1 * sin + t2 * cos], axis=-1)

    lane_head = (lax.broadcasted_iota(jnp.int32, (1, 2 * half), 1) % half) // (RET_KEY_DIM // 2)
    for bi in range(q_ref.shape[0]):
        rq = rot(q_ref[bi])
        rk = rot(k_ref[bi]) * (RET_KEY_DIM ** -0.5)
        v = v_ref[bi]
        rkb = rk.astype(BF16)
        state = state_ref[bi]
        o_cross = _dot((rq * dq_ref[...]).astype(BF16), state.astype(BF16))
        outs = []
        for h in range(RET_HEADS):
            qh = jnp.where(lane_head == h, rq, 0.0).astype(BF16)
            s = _dot_nt(qh, rkb) * dec_ref[h]
            outs.append(_dot(s.astype(BF16), v[:, h * RET_VAL_DIM:(h + 1) * RET_VAL_DIM]))
        o = jnp.concatenate(outs, axis=-1) + o_cross
        k_end_t = (rk * dk_ref[...]).T.astype(BF16)
        kv = _dot(k_end_t, v)
        row_head = (lax.broadcasted_iota(jnp.int32, kv.shape, 0) % half) // (RET_KEY_DIM // 2)
        col_head = lax.broadcasted_iota(jnp.int32, kv.shape, 1) // RET_VAL_DIM
        state_ref[bi] = state * cdec_ref[...] + jnp.where(row_head == col_head, kv, 0.0)
        normed = []
        for h in range(RET_HEADS):
            oh = o[:, h * RET_VAL_DIM:(h + 1) * RET_VAL_DIM]
            mu = jnp.mean(oh, axis=-1, keepdims=True)
            cen = oh - mu
            var = jnp.mean(cen * cen, axis=-1, keepdims=True)
            normed.append(cen * lax.rsqrt(var + EPS))
        y = jnp.concatenate(normed, axis=-1) * gn_ref[...]
        o_ref[bi] = (g_ref[bi].astype(F32) * y).astype(o_ref.dtype)


def _retention_tables(s):
    h, dk, dv = RET_HEADS, RET_KEY_DIM, RET_VAL_DIM
    quarter = dk // 2
    pos = np.arange(s, dtype=np.float64)
    inv_freq = ROPE_BASE ** (-np.arange(quarter, dtype=np.float64) / quarter)
    ang = pos[:, None] * np.tile(inv_freq, h)[None, :]
    log_gamma = np.log1p(-np.exp2(-5.0 - np.arange(h, dtype=np.float64)))
    idx = np.arange(BLK, dtype=np.float64)
    rel = idx[:, None] - idx[None, :]
    decay = np.where(rel >= 0, np.exp(log_gamma[:, None, None] * np.maximum(rel, 0.0)), 0.0)
    lane_gamma = np.tile(np.repeat(log_gamma, quarter), 2)
    dq = np.exp((idx + 1.0)[:, None] * lane_gamma[None, :])
    dkt = np.exp((BLK - 1 - idx)[:, None] * lane_gamma[None, :])
    cdec = np.repeat(np.exp(log_gamma * BLK), dv)[None, :]
    return tuple(t.astype(np.float32) for t in (np.cos(ang), np.sin(ang), dq, dkt, decay, cdec))


def _retention_call(q, k, v, g, gn_g):
    b, s, wk = q.shape
    wv = v.shape[-1]
    cos, sin, dq, dkt, decay, cdec = _retention_tables(s)
    tok = lambda w: pl.BlockSpec((b, BLK, w), lambda c: (0, c, 0))
    const = lambda shape: pl.BlockSpec(shape, lambda c: (0,) * len(shape))
    in_specs = [tok(wk), tok(wk), tok(wv), tok(wv),
                pl.BlockSpec((BLK, wk // 2), lambda c: (c, 0)),
                pl.BlockSpec((BLK, wk // 2), lambda c: (c, 0)),
                const((BLK, wk)), const((BLK, wk)), const((RET_HEADS, BLK, BLK)),
                const((1, wv)), const((1, wv))]
    args = [q, k, v, g, cos, sin, dq, dkt, decay, cdec, gn_g.reshape(1, wv)]
    return (args, in_specs, tok(wv), jax.ShapeDtypeStruct((b, s, wv), BF16),
            [pltpu.VMEM((b, wk, wv), F32)])


def _recurrent_mixers_kernel(*refs, n_ret_in, n_ssd_in):
    ret_in, ssd_in = refs[:n_ret_in], refs[n_ret_in:n_ret_in + n_ssd_in]
    ret_out, ssd_out = refs[n_ret_in + n_ssd_in:n_ret_in + n_ssd_in + 2]
    scratch = refs[n_ret_in + n_ssd_in + 2:]
    _retention_kernel(*ret_in, ret_out, scratch[0])
    _ssd_kernel(*ssd_in, ssd_out, *scratch[1:])


def _recurrent_mixers(ret_call, ssd_call, steps):
    (ra, rs, ro, rshape, rscr), (sa, ss, so, sshape, sscr) = ret_call, ssd_call
    return pl.pallas_call(
        functools.partial(_recurrent_mixers_kernel, n_ret_in=len(ra), n_ssd_in=len(sa)),
        grid=(steps,),
        in_specs=rs + ss, out_specs=[ro, so], out_shape=[rshape, sshape],
        scratch_shapes=rscr + sscr,
        compiler_params=_params(("arbitrary",), 40),
        name="recurrent_mixers",
    )(*ra, *sa)


def _diff_attn_kernel(lam_ref, g_ref, q_ref, k_ref, v_ref, o_ref, s_ref, m_ref, vext_ref, acc_ref, *,
                      tq, lam_init):
    d = DIFF_HEAD_DIM
    hw = 2 * d
    nl = tq // LANES
    qi = pl.program_id(2)
    q = q_ref[0]
    qs = (q[:, :d], q[:, d:])

    @pl.when(qi == 0)
    def _():
        vext_ref[:, :hw] = v_ref[0]
        vext_ref[:, hw:] = jnp.ones((vext_ref.shape[0], hw), BF16)

    def fold_max(x):
        r = x[:, :LANES]
        for c in range(1, nl):
            r = jnp.maximum(r, x[:, c * LANES:(c + 1) * LANES])
        return r

    m_ref[...] = jnp.full(m_ref.shape, NEG, F32)

    def scores(j, diagonal):
        k = k_ref[0, pl.ds(pl.multiple_of(j * tq, tq), tq), :]
        for c in range(2):
            s = _dot_nt(qs[c], k[:, c * d:(c + 1) * d]) * LOG2E
            if diagonal:
                row = lax.broadcasted_iota(jnp.int32, (tq, tq), 0)
                col = lax.broadcasted_iota(jnp.int32, (tq, tq), 1)
                s = jnp.where(col <= row, s, NEG)
            s_ref[c, j] = s
            m_ref[c] = jnp.maximum(m_ref[c], fold_max(s))

    def for_each_block(n, body):
        def quad(jj, carry):
            for u in range(4):
                body(4 * jj + u)
            return carry

        lax.fori_loop(0, lax.shift_right_logical(n, 2), quad, 0)
        done = n & ~3

        @pl.when((n & 2) == 2)
        def _():
            body(done)
            body(done + 1)

        @pl.when((n & 1) == 1)
        def _():
            body(n - 1)

    for_each_block(qi, lambda j: scores(j, False))
    scores(qi, True)

    for c in range(2):
        m_ref[c] = jnp.broadcast_to(jnp.max(m_ref[c], axis=-1, keepdims=True), (tq, LANES))
    acc_ref[...] = jnp.zeros(acc_ref.shape, F32)

    def accumulate(j):
        v = vext_ref[pl.ds(pl.multiple_of(j * tq, tq), tq), :]
        for c in range(2):
            m = m_ref[c]
            s = s_ref[c, j]
            p = jnp.concatenate([jnp.exp2(s[:, t * LANES:(t + 1) * LANES] - m) for t in range(nl)], axis=-1)
            acc_ref[c] += _dot(p.astype(BF16), v)

    for_each_block(qi + 1, accumulate)

    lp = lam_ref[...]
    lam = (jnp.exp(jnp.sum(lp[0:1] * lp[1:2], keepdims=True))
           - jnp.exp(jnp.sum(lp[2:3] * lp[3:4], keepdims=True)) + lam_init)
    o1 = acc_ref[0, :, :hw] / acc_ref[0, :, hw:]
    o2 = acc_ref[1, :, :hw] / acc_ref[1, :, hw:]
    o_ref[0] = (_rms(o1 - lam * o2, g_ref[...]) * (1.0 - lam_init)).astype(o_ref.dtype)


def _diff_attention(q, k, v, lam_params, subln_g, lam_init, tq=512):
    b, s, w = q.shape
    hw = 2 * DIFF_HEAD_DIM
    return pl.pallas_call(
        functools.partial(_diff_attn_kernel, tq=tq, lam_init=lam_init),
        grid=(b, DIFF_HEADS, s // tq),
        in_specs=[pl.BlockSpec(lam_params.shape, lambda bi, h, i: (0, 0)),
                  pl.BlockSpec((1, hw), lambda bi, h, i: (0, 0)),
                  pl.BlockSpec((1, tq, hw), lambda bi, h, i: (bi, i, h)),
                  pl.BlockSpec((1, s, hw), lambda bi, h, i: (bi, 0, h)),
                  pl.BlockSpec((1, s, hw), lambda bi, h, i: (bi, 0, h))],
        out_specs=pl.BlockSpec((1, tq, hw), lambda bi, h, i: (bi, i, h)),
        out_shape=jax.ShapeDtypeStruct((b, s, w), BF16),
        scratch_shapes=[pltpu.VMEM((2, s // tq, tq, tq), F32), pltpu.VMEM((2, tq, LANES), F32),
                        pltpu.VMEM((s, 2 * hw), BF16), pltpu.VMEM((2, tq, 2 * hw), F32)],
        compiler_params=_params(("parallel", "parallel", "arbitrary"), 40),
        name="diff_attention",
    )(lam_params, subln_g.reshape(1, hw), q, k, v)


def _ssd_kernel(z_ref, xbc_ref, dt_ref, cw_ref, cb_ref, dtb_ref, alog_ref, dskip_ref, ng_ref,
                tri_ref, exp_ref, o_ref, carry_ref, ext_ref, state_ref):
    gn = SSM_GROUPS * SSM_STATE
    hpg = SSM_HEADS // SSM_GROUPS
    gw = hpg * SSM_HEAD_DIM
    tail = 8

    @pl.when(pl.program_id(0) == 0)
    def _():
        carry_ref[...] = jnp.zeros_like(carry_ref)
        state_ref[...] = jnp.zeros_like(state_ref)

    cw = cw_ref[...]
    lane = lax.broadcasted_iota(jnp.int32, (1, LANES), 1)
    a_neg = jnp.where(lane < SSM_HEADS, -jnp.exp(alog_ref[...]), 0.0)
    expand = exp_ref[...]
    row = lax.broadcasted_iota(jnp.int32, (BLK, BLK), 0)
    col = lax.broadcasted_iota(jnp.int32, (BLK, BLK), 1)
    col_head = lax.broadcasted_iota(jnp.int32, (1, gw), 1) // SSM_HEAD_DIM
    for bi in range(z_ref.shape[0]):
        xbc = xbc_ref[bi]
        ext_ref[bi, 0:tail, :] = carry_ref[bi]
        ext_ref[bi, tail:tail + BLK, :] = xbc
        carry_ref[bi] = xbc[BLK - tail:, :]
        conv = cb_ref[...] + cw[SSM_CONV - 1:SSM_CONV, :] * xbc
        for sft in range(1, SSM_CONV):
            conv = conv + (cw[SSM_CONV - 1 - sft:SSM_CONV - sft, :]
                           * ext_ref[bi, tail - sft:tail - sft + BLK, :])
        xc = _silu(conv)
        xs, bm, cm = xc[:, :SSM_D_INNER], xc[:, SSM_D_INNER:SSM_D_INNER + gn], xc[:, SSM_D_INNER + gn:]

        pre = dt_ref[bi] + dtb_ref[...]
        dt = jnp.maximum(pre, 0.0) + jnp.log1p(jnp.exp(-jnp.abs(pre)))
        a_col = _sel_dot(tri_ref[...], dt * a_neg)
        a_row = a_col.T
        a_exp = _dot_sel(a_col, expand)
        dt_exp = _dot_sel(dt, expand)
        total = a_exp[BLK - 1:BLK, :]
        xdt = xs * dt_exp
        x_end = (xdt * jnp.exp(total - a_exp)).astype(BF16)
        xdt_b = xdt.astype(BF16)
        ys = []
        for g in range(SSM_GROUPS):
            b_g = bm[:, g * SSM_STATE:(g + 1) * SSM_STATE]
            c_g = cm[:, g * SSM_STATE:(g + 1) * SSM_STATE].astype(BF16)
            cb = _dot_nt(c_g, b_g.astype(BF16))
            xg = xdt_b[:, g * gw:(g + 1) * gw]
            y_g = jnp.zeros((BLK, gw), F32)
            for hh in range(hpg):
                h = g * hpg + hh
                seg = a_col[:, h:h + 1] - a_row[h:h + 1, :]
                m_h = (cb * jnp.exp(jnp.where(col <= row, seg, NEG))).astype(BF16)
                y_g = y_g + _dot(m_h, jnp.where(col_head == hh, xg, jnp.zeros_like(xg)))
            state = state_ref[bi, g]
            y_g = y_g + _dot(c_g, state.astype(BF16)) * jnp.exp(a_exp[:, g * gw:(g + 1) * gw])
            new = _dot(b_g.T.astype(BF16), x_end[:, g * gw:(g + 1) * gw])
            state_ref[bi, g] = state * jnp.exp(total[:, g * gw:(g + 1) * gw]) + new
            ys.append(y_g)
        y = jnp.concatenate(ys, axis=-1) + xs * dskip_ref[...]
        y = y * z_ref[bi].astype(F32)
        o_ref[bi] = _rms(y, ng_ref[...]).astype(o_ref.dtype)


def _ssd_call(z, xbc, dt, conv_w, conv_b, dt_bias, a_log, d_skip, norm_g):
    b, s, c = xbc.shape
    di = SSM_D_INNER
    pad = lambda t: jnp.pad(t.astype(F32), (0, LANES - t.shape[0])).reshape(1, LANES)
    tri = jnp.asarray(np.tril(np.ones((BLK, BLK), np.float32)), BF16)
    expand = jnp.asarray(np.arange(LANES)[:, None] == (np.arange(di) // SSM_HEAD_DIM)[None, :], BF16)
    tok = lambda w: pl.BlockSpec((b, BLK, w), lambda ci: (0, ci, 0))
    const = lambda shape: pl.BlockSpec(shape, lambda ci: (0,) * len(shape))
    in_specs = [tok(di), tok(c), tok(LANES), const((SSM_CONV, c)), const((1, c)),
                const((1, LANES)), const((1, LANES)), const((1, di)), const((1, di)),
                const((BLK, BLK)), const((LANES, di))]
    args = [z, xbc, dt, conv_w, conv_b.reshape(1, c), pad(dt_bias), pad(a_log),
            jnp.repeat(d_skip.astype(F32), SSM_HEAD_DIM).reshape(1, di), norm_g.reshape(1, di), tri, expand]
    scratch = [pltpu.VMEM((b, 8, c), F32), pltpu.VMEM((b, 8 + BLK, c), F32),
               pltpu.VMEM((b, SSM_GROUPS, SSM_STATE, di // SSM_GROUPS), F32)]
    return args, in_specs, tok(di), jax.ShapeDtypeStruct((b, s, di), BF16), scratch


def _from_residue_view(src_ref, slab_ref, dil, width):
    if dil == 1:
        return src_ref[...].astype(F32)
    rows = src_ref.shape[0]
    tiles = width // LANES
    for r in range(dil):
        for t in range(tiles):
            col = r * width + t * LANES
            slab_ref[t, pl.ds(r, rows, stride=dil), :] = src_ref[:, col:col + LANES].astype(F32)
    return jnp.concatenate([slab_ref[t] for t in range(tiles)], axis=-1)


def _merge_kernel(x_ref, *refs, dils):
    n = len(dils)
    o_refs, l_refs = refs[:n], refs[n:2 * n]
    yb_ref, yc_ref, yd_ref, gate_ref, wb_ref, wo_ref, hexp_ref, out_ref = refs[2 * n:2 * n + 8]
    o_slabs, l_slabs = refs[2 * n + 8:3 * n + 8], refs[3 * n + 8:]
    d = x_ref.shape[-1]
    bw = yb_ref.shape[-1]
    lses = [_from_residue_view(l_refs[i], l_slabs[i], dils[i], LANES) for i in range(n)]
    m = functools.reduce(jnp.maximum, lses)
    es = [jnp.exp(l - m) for l in lses]
    inv = 1.0 / functools.reduce(jnp.add, es)
    ya = jnp.zeros((x_ref.shape[0], bw), F32)
    for i in range(n):
        wt = _dot((es[i] * inv).astype(BF16), hexp_ref[...])
        ya = ya + wt * _from_residue_view(o_refs[i], o_slabs[i], dils[i], bw)
    branches = (ya.astype(BF16), yb_ref[...], yc_ref[...], yd_ref[...])
    merged = jnp.zeros((x_ref.shape[0], d), F32)
    for i, y in enumerate(branches):
        gate = gate_ref[:, i * d:(i + 1) * d].astype(F32)
        merged = merged + gate * _dot(y, wb_ref[i])
    out_ref[...] = x_ref[...] + _dot(merged.astype(BF16), wo_ref[...])


def _merge(x2d, o_parts, lse_parts, dils, yb, yc, yd, gates, w_branch, w_out, tm=256):
    t, d = x2d.shape
    bw = yb.shape[-1]
    row = lambda w: pl.BlockSpec((tm, w), lambda i: (i, 0))
    view = lambda w, dil: pl.BlockSpec((tm // dil, dil * w), lambda i: (i, 0))
    per_head = LANES // DIL_HEADS
    hexp = jnp.asarray(np.arange(LANES)[:, None] == (np.arange(bw) // DIL_HEAD_DIM * per_head)[None, :],
                       BF16)
    return pl.pallas_call(
        functools.partial(_merge_kernel, dils=tuple(dils)),
        grid=(t // tm,),
        in_specs=[row(d)] + [view(bw, dil) for dil in dils] + [view(LANES, dil) for dil in dils]
                 + [row(bw)] * 3 + [row(N_BRANCH * d),
                                    pl.BlockSpec(w_branch.shape, lambda i: (0, 0, 0)),
                                    pl.BlockSpec(w_out.shape, lambda i: (0, 0)),
                                    pl.BlockSpec(hexp.shape, lambda i: (0, 0))],
        out_specs=row(d),
        out_shape=jax.ShapeDtypeStruct((t, d), F32),
        scratch_shapes=[pltpu.VMEM((bw // LANES, tm, LANES), F32) for _ in dils]
                       + [pltpu.VMEM((1, tm, LANES), F32) for _ in dils],
        compiler_params=_params(("parallel",), 48),
        name="merge",
    )(x2d, *o_parts, *lse_parts, yb, yc, yd, gates, w_branch, w_out, hexp)


def _xattn_ffn_kernel(x_ref, gx_ref, wq_ref, k_ref, v_ref, wxo_ref, gf_ref, wa_ref, wb_ref, wo_ref,
                      fg_ref, out_ref, *, final_norm):
    x = x_ref[0]
    q = _dot(_rms(x, gx_ref[...]).astype(BF16), wq_ref[...]).astype(BF16)
    k, v = k_ref[0], v_ref[0]
    outs = []
    for h in range(X_HEADS):
        sl = slice(h * X_HEAD_DIM, (h + 1) * X_HEAD_DIM)
        s = _dot_nt(q[:, sl], k[:, sl]) * (X_HEAD_DIM ** -0.5)
        p = jnp.exp(s - jnp.max(s, axis=-1, keepdims=True))
        a = p / jnp.sum(p, axis=-1, keepdims=True)
        outs.append(_dot(a.astype(BF16), v[:, sl]))
    x = x + _dot(jnp.concatenate(outs, axis=-1).astype(BF16), wxo_ref[...])
    h = _rms(x, gf_ref[...]).astype(BF16)
    a = _dot(h, wa_ref[...])
    b = _dot(h, wb_ref[...])
    y = x + _dot((_silu(a) * b).astype(BF16), wo_ref[...])
    if final_norm:
        y = _rms(y, fg_ref[...])
    out_ref[0] = y


def _xattn_ffn(x, gx, w_q, k, v, w_xo, gf, w_a, w_b, w_o, final_g, final_norm, tm=512):
    b, s, d = x.shape
    m, w = k.shape[1], k.shape[2]
    full = lambda wt: pl.BlockSpec(wt.shape, lambda bi, i: (0, 0), pipeline_mode=pl.Buffered(1))
    vec = pl.BlockSpec((1, d), lambda bi, i: (0, 0))
    mem = pl.BlockSpec((1, m, w), lambda bi, i: (bi, 0, 0))
    return pl.pallas_call(
        functools.partial(_xattn_ffn_kernel, final_norm=final_norm),
        grid=(b, s // tm),
        in_specs=[pl.BlockSpec((1, tm, d), lambda bi, i: (bi, i, 0)), vec, full(w_q), mem, mem,
                  full(w_xo), vec, full(w_a), full(w_b), full(w_o), vec],
        out_specs=pl.BlockSpec((1, tm, d), lambda bi, i: (bi, i, 0)),
        out_shape=jax.ShapeDtypeStruct((b, s, d), F32),
        compiler_params=_params(("parallel", "parallel"), 56),
        name="xattn_ffn",
    )(x, gx.reshape(1, d), w_q, k, v, w_xo, gf.reshape(1, d), w_a, w_b, w_o, final_g.reshape(1, d))


def _ret_feature_order():
    quarter = RET_KEY_DIM // 2
    first = [h * RET_KEY_DIM + f for h in range(RET_HEADS) for f in range(quarter)]
    return np.array(first + [c + quarter for c in first])


def _layer(x, mem, p, lam_init, final_g, final_norm):
    b, s, d = x.shape
    t = b * s
    x2d = x.reshape(t, d)
    a_w = DIL_HEADS * DIL_HEAD_DIM
    b_qk, b_vg = RET_HEADS * RET_KEY_DIM, RET_HEADS * RET_VAL_DIM
    c_w = DIFF_HEADS * 2 * DIFF_HEAD_DIM
    d_xbc = SSM_D_INNER + 2 * SSM_GROUPS * SSM_STATE
    sizes = (a_w, a_w, a_w, b_qk, b_qk, b_vg, b_vg, c_w, c_w, c_w, SSM_D_INNER, d_xbc, SSM_HEADS,
             N_BRANCH * d)
    offs = np.concatenate([[0], np.cumsum(sizes)])
    w_in = p['w_in']
    col = lambda i: w_in[:, offs[i]:offs[i + 1]]
    order = _ret_feature_order()
    w_dt = jnp.pad(col(12), ((0, 0), (0, LANES - SSM_HEADS)))
    bf = lambda w: w.astype(BF16)

    dils = [dil for _, dil in DIL_PATTERNS]
    nd = len(dils)
    proj = _proj_views(x2d, p['norm_mix_g'], [bf(col(i)) for i in (0, 1, 2)],
                       [DIL_HEAD_DIM ** -0.5, 1.0, 1.0], [bf(col(i)) for i in (7, 8, 9)],
                       [DIFF_HEAD_DIM ** -0.5, 1.0, 1.0], dils)
    aq, ak, av = proj[0:nd], proj[nd:2 * nd], proj[2 * nd:3 * nd]
    cq, ck, cv = proj[3 * nd:]
    bq, bk, bv, bg, dz, dxbc, ddt = _norm_matmul(
        x2d, p['norm_mix_g'],
        [bf(col(3)[:, order]), bf(col(4)[:, order]), bf(col(5)), bf(col(6)), bf(col(10)), bf(col(11)),
         bf(w_dt)],
        [F32, F32, BF16, BF16, BF16, F32, F32], [None, None, None, _silu, _silu, None, None], tm=512)
    (gates,) = _norm_matmul(x2d, p['norm_mix_g'], [bf(col(13))], [BF16], [_sigmoid], tm=512)

    r3 = lambda y: y.reshape(b, s, y.shape[-1])
    o_parts, lse_parts = _band_attention(aq, ak, av, b, s)
    y_b, y_d = _recurrent_mixers(
        _retention_call(r3(bq), r3(bk), r3(bv), r3(bg), p['ret_gn_g']),
        _ssd_call(r3(dz), r3(dxbc), r3(ddt), p['ssm_conv_w'], p['ssm_conv_b'], p['ssm_dt_bias'],
                  p['ssm_A_log'], p['ssm_D'], p['ssm_norm_g']), s // BLK)
    y_c = _diff_attention(r3(cq), r3(ck), r3(cv), p['diff_lambda'], p['diff_subln_g'], lam_init)
    x2d = _merge(x2d, o_parts, lse_parts, dils, y_b.reshape(t, -1), y_c.reshape(t, -1),
                 y_d.reshape(t, -1), gates, bf(p['w_branch']), bf(p['w_mix_out']))

    xw = X_HEADS * X_HEAD_DIM
    m2d = mem.reshape(-1, d)
    mk, mv = _norm_matmul(m2d, p['norm_mem_g'], [bf(p['w_xkv'][:, :xw]), bf(p['w_xkv'][:, xw:])],
                          [BF16, BF16], [None, None], tm=256)
    mlen = mem.shape[1]
    hid = p['w_ffn_out'].shape[0]
    return _xattn_ffn(x2d.reshape(b, s, d), p['norm_x_g'], bf(p['w_xq']), mk.reshape(b, mlen, xw),
                      mv.reshape(b, mlen, xw), bf(p['w_xo']), p['norm_ffn_g'], bf(p['w_ffn_in'][:, :hid]),
                      bf(p['w_ffn_in'][:, hid:]), bf(p['w_ffn_out']), final_g, final_norm)


def kernel(x, mem, norm_mix_g, w_in, ret_gn_g, diff_lambda, diff_subln_g, ssm_conv_w, ssm_conv_b,
           ssm_dt_bias, ssm_A_log, ssm_D, ssm_norm_g, w_branch, w_mix_out, norm_x_g, norm_mem_g, w_xq,
           w_xkv, w_xo, norm_ffn_g, w_ffn_in, w_ffn_out, norm_f_g):
    stacked = dict(norm_mix_g=norm_mix_g, w_in=w_in, ret_gn_g=ret_gn_g, diff_lambda=diff_lambda,
                   diff_subln_g=diff_subln_g, ssm_conv_w=ssm_conv_w, ssm_conv_b=ssm_conv_b,
                   ssm_dt_bias=ssm_dt_bias, ssm_A_log=ssm_A_log, ssm_D=ssm_D, ssm_norm_g=ssm_norm_g,
                   w_branch=w_branch, w_mix_out=w_mix_out, norm_x_g=norm_x_g, norm_mem_g=norm_mem_g,
                   w_xq=w_xq, w_xkv=w_xkv, w_xo=w_xo, norm_ffn_g=norm_ffn_g, w_ffn_in=w_ffn_in,
                   w_ffn_out=w_ffn_out)
    depth = w_in.shape[0]
    for l in range(depth):
        lam_init = 0.8 - 0.6 * math.exp(-0.3 * l)
        p = {name: val[l] for name, val in stacked.items()}
        x = _layer(x, mem, p, lam_init, norm_f_g, final_norm=(l == depth - 1))
    return x
```

```python
import functools
import math

import jax
import jax.numpy as jnp
import numpy as np
from jax import lax
from jax.experimental import pallas as pl
from jax.experimental.pallas import tpu as pltpu

F32 = jnp.float32
BF16 = jnp.bfloat16

EPS = 1e-6
NEG = -1e30
LOG2E = math.log2(math.e)
BLK = 128

DIL_HEADS = 8
DIL_HEAD_DIM = 64
DIL_PATTERNS = ((128, 1), (512, 4), (2048, 16))
RET_HEADS = 4
RET_KEY_DIM = 64
RET_VAL_DIM = 128
ROPE_BASE = 10000.0
DIFF_HEADS = 4
DIFF_HEAD_DIM = 64
SSM_D_INNER = 512
SSM_HEAD_DIM = 64
SSM_HEADS = 8
SSM_GROUPS = 2
SSM_STATE = 128
SSM_CONV = 4
N_BRANCH = 4
BRANCH_WIDTH = 512
X_HEADS = 4
X_HEAD_DIM = 128
LANES = 128

MIB = 1024 * 1024


def _params(semantics, vmem_mib):
    return pltpu.CompilerParams(dimension_semantics=semantics,
                                vmem_limit_bytes=vmem_mib * MIB)


def _dot(a, b):
    return jnp.dot(a, b, preferred_element_type=F32)


def _dot_nt(a, b):
    return lax.dot_general(a, b, (((1,), (1,)), ((), ())), preferred_element_type=F32)


def _split3(x):
    x1 = x.astype(BF16)
    r1 = x - x1.astype(F32)
    x2 = r1.astype(BF16)
    x3 = (r1 - x2.astype(F32)).astype(BF16)
    return x1, x2, x3


def _dot_sel(x, sel):
    x1, x2, x3 = _split3(x)
    return _dot(x1, sel) + _dot(x2, sel) + _dot(x3, sel)


def _sel_dot(sel, x):
    x1, x2, x3 = _split3(x)
    return _dot(sel, x1) + _dot(sel, x2) + _dot(sel, x3)


def _sigmoid(x):
    return 0.5 * jnp.tanh(0.5 * x) + 0.5


def _silu(x):
    h = 0.5 * x
    return h * jnp.tanh(h) + h


def _rms(x, g):
    ms = jnp.mean(x * x, axis=-1, keepdims=True)
    return x * lax.rsqrt(ms + EPS) * g


def _norm_matmul_kernel(x_ref, g_ref, *refs, n_out, acts):
    w_refs, o_refs = refs[:n_out], refs[n_out:]
    h = _rms(x_ref[...], g_ref[...]).astype(BF16)
    for w_ref, o_ref, act in zip(w_refs, o_refs, acts):
        y = _dot(h, w_ref[...])
        if act is not None:
            y = act(y)
        o_ref[...] = y.astype(o_ref.dtype)


def _norm_matmul(x2d, g, ws, out_dtypes, acts, tm, vmem_mib=48):
    t, d = x2d.shape
    in_specs = [pl.BlockSpec((tm, d), lambda i: (i, 0)),
                pl.BlockSpec((1, d), lambda i: (0, 0))]
    in_specs += [pl.BlockSpec(w.shape, lambda i: (0, 0), pipeline_mode=pl.Buffered(1)) for w in ws]
    out_specs = [pl.BlockSpec((tm, w.shape[1]), lambda i: (i, 0)) for w in ws]
    out_shape = [jax.ShapeDtypeStruct((t, w.shape[1]), dt) for w, dt in zip(ws, out_dtypes)]
    return pl.pallas_call(
        functools.partial(_norm_matmul_kernel, n_out=len(ws), acts=tuple(acts)),
        grid=(t // tm,), in_specs=in_specs, out_specs=out_specs, out_shape=out_shape,
        compiler_params=_params(("parallel",), vmem_mib),
        name="norm_matmul",
    )(x2d, g.reshape(1, d), *ws)


def _proj_views_kernel(x_ref, g_ref, *refs, n_view, n_plain, dils, view_scales, plain_scales):
    nd = len(dils)
    n_w = n_view + n_plain
    w_refs = refs[:n_w]
    view_refs = refs[n_w:n_w + n_view * nd]
    plain_refs = refs[n_w + n_view * nd:n_w + n_view * nd + n_plain]
    slab_ref = refs[-1]
    tm = x_ref.shape[0]
    h = _rms(x_ref[...], g_ref[...]).astype(BF16)
    for i in range(n_view):
        y = _dot(h, w_refs[i][...])
        if view_scales[i] != 1.0:
            y = y * view_scales[i]
        width = y.shape[1]
        tiles = width // LANES
        for t in range(tiles):
            slab_ref[t] = y[:, t * LANES:(t + 1) * LANES]
        for di, dil in enumerate(dils):
            o_ref = view_refs[i * nd + di]
            if dil == 1:
                o_ref[...] = y.astype(o_ref.dtype)
                continue
            for r in range(dil):
                for t in range(tiles):
                    col = r * width + t * LANES
                    o_ref[:, col:col + LANES] = slab_ref[t, pl.ds(r, tm // dil, stride=dil), :].astype(o_ref.dtype)
    for i in range(n_plain):
        y = _dot(h, w_refs[n_view + i][...])
        if plain_scales[i] != 1.0:
            y = y * plain_scales[i]
        plain_refs[i][...] = y.astype(plain_refs[i].dtype)


def _proj_views(x2d, g, view_ws, view_scales, plain_ws, plain_scales, dils, tm=1024):
    t, d = x2d.shape
    ws = list(view_ws) + list(plain_ws)
    full = lambda w: pl.BlockSpec(w.shape, lambda i: (0, 0), pipeline_mode=pl.Buffered(1))
    out_specs, out_shape = [], []
    for w in view_ws:
        n = w.shape[1]
        for dil in dils:
            out_specs.append(pl.BlockSpec((tm // dil, dil * n), lambda i: (i, 0)))
            out_shape.append(jax.ShapeDtypeStruct((t // dil, dil * n), BF16))
    for w in plain_ws:
        out_specs.append(pl.BlockSpec((tm, w.shape[1]), lambda i: (i, 0)))
        out_shape.append(jax.ShapeDtypeStruct((t, w.shape[1]), BF16))
    width = view_ws[0].shape[1]
    return pl.pallas_call(
        functools.partial(_proj_views_kernel, n_view=len(view_ws), n_plain=len(plain_ws), dils=tuple(dils),
                          view_scales=tuple(view_scales), plain_scales=tuple(plain_scales)),
        grid=(t // tm,),
        in_specs=[pl.BlockSpec((tm, d), lambda i: (i, 0)), pl.BlockSpec((1, d), lambda i: (0, 0))]
                 + [full(w) for w in ws],
        out_specs=out_specs, out_shape=out_shape,
        scratch_shapes=[pltpu.VMEM((width // LANES, tm, LANES), F32)],
        compiler_params=_params(("parallel",), 48),
        name="proj_views",
    )(x2d, g.reshape(1, d), *ws)


def _band_attn_kernel(*refs, steps_per_residue):
    n_pat = len(steps_per_residue)
    ins, outs = refs[:6 * n_pat], refs[6 * n_pat:]
    step = pl.program_id(1)
    for i in range(n_pat):
        bias_ref, q_ref, kp_ref, kc_ref, vp_ref, vc_ref = ins[6 * i:6 * i + 6]
        o_ref, lse_ref = outs[2 * i:2 * i + 2]
        first = jnp.minimum(step % steps_per_residue[i], 1)
        lo, hi = slice(0, BLK), slice(BLK, 2 * BLK)
        _band_block(bias_ref[first], q_ref.at[0, lo], kp_ref.at[0], kc_ref.at[0, lo], vp_ref.at[0],
                    vc_ref.at[0, lo], o_ref.at[0, lo], lse_ref.at[0, lo])
        _band_block(bias_ref[1], q_ref.at[0, hi], kc_ref.at[0, lo], kc_ref.at[0, hi], vc_ref.at[0, lo],
                    vc_ref.at[0, hi], o_ref.at[0, hi], lse_ref.at[0, hi])


def _band_block(bias, q_ref, kp_ref, kc_ref, vp_ref, vc_ref, o_ref, lse_ref):
    hd = DIL_HEAD_DIM
    lane = lax.broadcasted_iota(jnp.int32, (1, LANES), 1)
    ones = jnp.ones((2 * BLK, LANES), BF16)
    lse_tile = jnp.zeros((BLK, LANES), F32)
    for t in range(DIL_HEADS * hd // LANES):
        sl = slice(t * LANES, (t + 1) * LANES)
        q = q_ref[:, sl]
        k2 = jnp.concatenate([kp_ref[:, sl], kc_ref[:, sl]], axis=0)
        v2 = jnp.concatenate([vp_ref[:, sl], vc_ref[:, sl]], axis=0)
        v_ext = jnp.concatenate([v2, ones], axis=-1)
        pair = []
        for half in range(LANES // hd):
            in_head = (lane // hd) == half
            s = _dot_nt(jnp.where(in_head, q, jnp.zeros_like(q)), k2) + bias
            m = jnp.max(jnp.maximum(s[:, :BLK], s[:, BLK:]), axis=-1, keepdims=True)
            p = jnp.exp((s - m).astype(BF16))
            pv = _dot(p, v_ext)
            den = pv[:, LANES:]
            pair.append(pv[:, :LANES] / den)
            h = t * (LANES // hd) + half
            per_head = LANES // DIL_HEADS
            lse_tile = jnp.where((lane // per_head) == h, m + jnp.log(den), lse_tile)
        o_ref[:, sl] = jnp.where((lane // hd) == 0, pair[0], pair[1]).astype(o_ref.dtype)
    lse_ref[...] = lse_tile


def _band_bias(window):
    i = np.arange(BLK)[:, None]
    j = np.arange(2 * BLK)[None, :]
    dist = BLK + i - j
    band = (dist >= 0) & (dist <= window)
    first = band & (j >= BLK)
    return np.where(np.stack([first, band]), 0.0, NEG).astype(np.float32)


def _band_attention(qs, ks, vs, batch, seq):
    w = DIL_HEADS * DIL_HEAD_DIM
    steps = seq // (2 * BLK)
    args, in_specs, out_specs, out_shape, per_residue = [], [], [], [], []
    for (window, dil), q, k, v in zip(DIL_PATTERNS, qs, ks, vs):
        l = seq // dil
        ns = l // (2 * BLK)
        per_residue.append(ns)
        view = lambda t, l=l, dil=dil: t.reshape(batch, l, dil * w)
        cur = lambda wd, ns=ns: pl.BlockSpec((1, 2 * BLK, wd), lambda bi, n: (bi, n % ns, n // ns))
        prev = pl.BlockSpec((1, BLK, w),
                            lambda bi, n, ns=ns: (bi, jnp.maximum(2 * (n % ns) - 1, 0), n // ns))
        bias = pl.BlockSpec((2, BLK, 2 * BLK), lambda bi, n: (0, 0, 0))
        args += [_band_bias(window // dil), view(q), view(k), view(k), view(v), view(v)]
        in_specs += [bias, cur(w), prev, cur(w), prev, cur(w)]
        out_specs += [cur(w), cur(LANES)]
        out_shape += [jax.ShapeDtypeStruct((batch, l, dil * w), BF16),
                      jax.ShapeDtypeStruct((batch, l, dil * LANES), F32)]
    outs = pl.pallas_call(
        functools.partial(_band_attn_kernel, steps_per_residue=tuple(per_residue)),
        grid=(batch, steps),
        in_specs=in_specs, out_specs=out_specs, out_shape=out_shape,
        compiler_params=_params(("parallel", "parallel"), 32),
        name="band_attention",
    )(*args)
    flat = [o.reshape(-1, o.shape[-1]) for o in outs]
    return flat[0::2], flat[1::2]


def _retention_kernel(q_ref, k_ref, v_ref, g_ref, cos_ref, sin_ref, dq_ref, dk_ref, dec_ref,
                      cdec_ref, gn_ref, o_ref, state_ref):
    @pl.when(pl.program_id(0) == 0)
    def _():
        state_ref[...] = jnp.zeros_like(state_ref)

    cos, sin = cos_ref[...], sin_ref[...]
    half = RET_HEADS * RET_KEY_DIM // 2

    def rot(t):
        t1, t2 = t[:, :half], t[:, half:]
        return jnp.concatenate([t1 * cos - t2 * sin, t1 * sin + t2 * cos], axis=-1)

    lane_head = (lax.broadcasted_iota(jnp.int32, (1, 2 * half), 1) % half) // (RET_KEY_DIM // 2)
    for bi in range(q_ref.shape[0]):
        rq = rot(q_ref[bi])
        rk = rot(k_ref[bi]) * (RET_KEY_DIM ** -0.5)
        v = v_ref[bi]
        rkb = rk.astype(BF16)
        state = state_ref[bi]
        o_cross = _dot((rq * dq_ref[...]).astype(BF16), state.astype(BF16))
        outs = []
        for h in range(RET_HEADS):
            qh = jnp.where(lane_head == h, rq, 0.0).astype(BF16)
            s = _dot_nt(qh, rkb) * dec_ref[h]
            outs.append(_dot(s.astype(BF16), v[:, h * RET_VAL_DIM:(h + 1) * RET_VAL_DIM]))
        o = jnp.concatenate(outs, axis=-1) + o_cross
        k_end_t = (rk * dk_ref[...]).T.astype(BF16)
        kv = _dot(k_end_t, v)
        row_head = (lax.broadcasted_iota(jnp.int32, kv.shape, 0) % half) // (RET_KEY_DIM // 2)
        col_head = lax.broadcasted_iota(jnp.int32, kv.shape, 1) // RET_VAL_DIM
        state_ref[bi] = state * cdec_ref[...] + jnp.where(row_head == col_head, kv, 0.0)
        normed = []
        for h in range(RET_HEADS):
            oh = o[:, h * RET_VAL_DIM:(h + 1) * RET_VAL_DIM]
            mu = jnp.mean(oh, axis=-1, keepdims=True)
            cen = oh - mu
            var = jnp.mean(cen * cen, axis=-1, keepdims=True)
            normed.append(cen * lax.rsqrt(var + EPS))
        y = jnp.concatenate(normed, axis=-1) * gn_ref[...]
        o_ref[bi] = (g_ref[bi].astype(F32) * y).astype(o_ref.dtype)


def _retention_tables(s):
    h, dk, dv = RET_HEADS, RET_KEY_DIM, RET_VAL_DIM
    quarter = dk // 2
    pos = np.arange(s, dtype=np.float64)
    inv_freq = ROPE_BASE ** (-np.arange(quarter, dtype=np.float64) / quarter)
    ang = pos[:, None] * np.tile(inv_freq, h)[None, :]
    log_gamma = np.log1p(-np.exp2(-5.0 - np.arange(h, dtype=np.float64)))
    idx = np.arange(BLK, dtype=np.float64)
    rel = idx[:, None] - idx[None, :]
    decay = np.where(rel >= 0, np.exp(log_gamma[:, None, None] * np.maximum(rel, 0.0)), 0.0)
    lane_gamma = np.tile(np.repeat(log_gamma, quarter), 2)
    dq = np.exp((idx + 1.0)[:, None] * lane_gamma[None, :])
    dkt = np.exp((BLK - 1 - idx)[:, None] * lane_gamma[None, :])
    cdec = np.repeat(np.exp(log_gamma * BLK), dv)[None, :]
    return tuple(t.astype(np.float32) for t in (np.cos(ang), np.sin(ang), dq, dkt, decay, cdec))


def _retention_call(q, k, v, g, gn_g):
    b, s, wk = q.shape
    wv = v.shape[-1]
    cos, sin, dq, dkt, decay, cdec = _retention_tables(s)
    tok = lambda w: pl.BlockSpec((b, BLK, w), lambda c: (0, c, 0))
    const = lambda shape: pl.BlockSpec(shape, lambda c: (0,) * len(shape))
    in_specs = [tok(wk), tok(wk), tok(wv), tok(wv),
                pl.BlockSpec((BLK, wk // 2), lambda c: (c, 0)),
                pl.BlockSpec((BLK, wk // 2), lambda c: (c, 0)),
                const((BLK, wk)), const((BLK, wk)), const((RET_HEADS, BLK, BLK)),
                const((1, wv)), const((1, wv))]
    args = [q, k, v, g, cos, sin, dq, dkt, decay, cdec, gn_g.reshape(1, wv)]
    return (args, in_specs, tok(wv), jax.ShapeDtypeStruct((b, s, wv), BF16),
            [pltpu.VMEM((b, wk, wv), F32)])


def _recurrent_mixers_kernel(*refs, n_ret_in, n_ssd_in):
    ret_in, ssd_in = refs[:n_ret_in], refs[n_ret_in:n_ret_in + n_ssd_in]
    ret_out, ssd_out = refs[n_ret_in + n_ssd_in:n_ret_in + n_ssd_in + 2]
    scratch = refs[n_ret_in + n_ssd_in + 2:]
    _retention_kernel(*ret_in, ret_out, scratch[0])
    _ssd_kernel(*ssd_in, ssd_out, *scratch[1:])


def _recurrent_mixers(ret_call, ssd_call, steps):
    (ra, rs, ro, rshape, rscr), (sa, ss, so, sshape, sscr) = ret_call, ssd_call
    return pl.pallas_call(
        functools.partial(_recurrent_mixers_kernel, n_ret_in=len(ra), n_ssd_in=len(sa)),
        grid=(steps,),
        in_specs=rs + ss, out_specs=[ro, so], out_shape=[rshape, sshape],
        scratch_shapes=rscr + sscr,
        compiler_params=_params(("arbitrary",), 40),
        name="recurrent_mixers",
    )(*ra, *sa)


def _diff_attn_kernel(lam_ref, g_ref, q_ref, k_ref, v_ref, o_ref, s_ref, m_ref, vext_ref, acc_ref, *,
                      tq, lam_init):
    d = DIFF_HEAD_DIM
    hw = 2 * d
    nl = tq // LANES
    qi = pl.program_id(2)
    q = q_ref[0]
    qs = (q[:, :d], q[:, d:])

    @pl.when(qi == 0)
    def _():
        vext_ref[:, :hw] = v_ref[0]
        vext_ref[:, hw:] = jnp.ones((vext_ref.shape[0], hw), BF16)

    def fold_max(x):
        r = x[:, :LANES]
        for c in range(1, nl):
            r = jnp.maximum(r, x[:, c * LANES:(c + 1) * LANES])
        return r

    m_ref[...] = jnp.full(m_ref.shape, NEG, F32)

    def scores(j, diagonal):
        k = k_ref[0, pl.ds(pl.multiple_of(j * tq, tq), tq), :]
        for c in range(2):
            s = _dot_nt(qs[c], k[:, c * d:(c + 1) * d]) * LOG2E
            if diagonal:
                row = lax.broadcasted_iota(jnp.int32, (tq, tq), 0)
                col = lax.broadcasted_iota(jnp.int32, (tq, tq), 1)
                s = jnp.where(col <= row, s, NEG)
            s_ref[c, j] = s
            m_ref[c] = jnp.maximum(m_ref[c], fold_max(s))

    def for_each_block(n, body):
        def quad(jj, carry):
            for u in range(4):
                body(4 * jj + u)
            return carry

        lax.fori_loop(0, lax.shift_right_logical(n, 2), quad, 0)
        done = n & ~3

        @pl.when((n & 2) == 2)
        def _():
            body(done)
            body(done + 1)

        @pl.when((n & 1) == 1)
        def _():
            body(n - 1)

    for_each_block(qi, lambda j: scores(j, False))
    scores(qi, True)

    for c in range(2):
        m_ref[c] = jnp.broadcast_to(jnp.max(m_ref[c], axis=-1, keepdims=True), (tq, LANES))
    acc_ref[...] = jnp.zeros(acc_ref.shape, F32)

    def accumulate(j):
        v = vext_ref[pl.ds(pl.multiple_of(j * tq, tq), tq), :]
        for c in range(2):
            m = m_ref[c]
            s = s_ref[c, j]
            p = jnp.concatenate([jnp.exp2(s[:, t * LANES:(t + 1) * LANES] - m) for t in range(nl)], axis=-1)
            acc_ref[c] += _dot(p.astype(BF16), v)

    for_each_block(qi + 1, accumulate)

    lp = lam_ref[...]
    lam = (jnp.exp(jnp.sum(lp[0:1] * lp[1:2], keepdims=True))
           - jnp.exp(jnp.sum(lp[2:3] * lp[3:4], keepdims=True)) + lam_init)
    o1 = acc_ref[0, :, :hw] / acc_ref[0, :, hw:]
    o2 = acc_ref[1, :, :hw] / acc_ref[1, :, hw:]
    o_ref[0] = (_rms(o1 - lam * o2, g_ref[...]) * (1.0 - lam_init)).astype(o_ref.dtype)


def _diff_attention(q, k, v, lam_params, subln_g, lam_init, tq=512):
    b, s, w = q.shape
    hw = 2 * DIFF_HEAD_DIM
    return pl.pallas_call(
        functools.partial(_diff_attn_kernel, tq=tq, lam_init=lam_init),
        grid=(b, DIFF_HEADS, s // tq),
        in_specs=[pl.BlockSpec(lam_params.shape, lambda bi, h, i: (0, 0)),
                  pl.BlockSpec((1, hw), lambda bi, h, i: (0, 0)),
                  pl.BlockSpec((1, tq, hw), lambda bi, h, i: (bi, i, h)),
                  pl.BlockSpec((1, s, hw), lambda bi, h, i: (bi, 0, h)),
                  pl.BlockSpec((1, s, hw), lambda bi, h, i: (bi, 0, h))],
        out_specs=pl.BlockSpec((1, tq, hw), lambda bi, h, i: (bi, i, h)),
        out_shape=jax.ShapeDtypeStruct((b, s, w), BF16),
        scratch_shapes=[pltpu.VMEM((2, s // tq, tq, tq), F32), pltpu.VMEM((2, tq, LANES), F32),
                        pltpu.VMEM((s, 2 * hw), BF16), pltpu.VMEM((2, tq, 2 * hw), F32)],
        compiler_params=_params(("parallel", "parallel", "arbitrary"), 40),
        name="diff_attention",
    )(lam_params, subln_g.reshape(1, hw), q, k, v)


def _ssd_kernel(z_ref, xbc_ref, dt_ref, cw_ref, cb_ref, dtb_ref, alog_ref, dskip_ref, ng_ref,
                tri_ref, exp_ref, o_ref, carry_ref, ext_ref, state_ref):
    gn = SSM_GROUPS * SSM_STATE
    hpg = SSM_HEADS // SSM_GROUPS
    gw = hpg * SSM_HEAD_DIM
    tail = 8

    @pl.when(pl.program_id(0) == 0)
    def _():
        carry_ref[...] = jnp.zeros_like(carry_ref)
        state_ref[...] = jnp.zeros_like(state_ref)

    cw = cw_ref[...]
    lane = lax.broadcasted_iota(jnp.int32, (1, LANES), 1)
    a_neg = jnp.where(lane < SSM_HEADS, -jnp.exp(alog_ref[...]), 0.0)
    expand = exp_ref[...]
    row = lax.broadcasted_iota(jnp.int32, (BLK, BLK), 0)
    col = lax.broadcasted_iota(jnp.int32, (BLK, BLK), 1)
    col_head = lax.broadcasted_iota(jnp.int32, (1, gw), 1) // SSM_HEAD_DIM
    for bi in range(z_ref.shape[0]):
        xbc = xbc_ref[bi]
        ext_ref[bi, 0:tail, :] = carry_ref[bi]
        ext_ref[bi, tail:tail + BLK, :] = xbc
        carry_ref[bi] = xbc[BLK - tail:, :]
        conv = cb_ref[...] + cw[SSM_CONV - 1:SSM_CONV, :] * xbc
        for sft in range(1, SSM_CONV):
            conv = conv + (cw[SSM_CONV - 1 - sft:SSM_CONV - sft, :]
                           * ext_ref[bi, tail - sft:tail - sft + BLK, :])
        xc = _silu(conv)
        xs, bm, cm = xc[:, :SSM_D_INNER], xc[:, SSM_D_INNER:SSM_D_INNER + gn], xc[:, SSM_D_INNER + gn:]

        pre = dt_ref[bi] + dtb_ref[...]
        dt = jnp.maximum(pre, 0.0) + jnp.log1p(jnp.exp(-jnp.abs(pre)))
        a_col = _sel_dot(tri_ref[...], dt * a_neg)
        a_row = a_col.T
        a_exp = _dot_sel(a_col, expand)
        dt_exp = _dot_sel(dt, expand)
        total = a_exp[BLK - 1:BLK, :]
        xdt = xs * dt_exp
        x_end = (xdt * jnp.exp(total - a_exp)).astype(BF16)
        xdt_b = xdt.astype(BF16)
        ys = []
        for g in range(SSM_GROUPS):
            b_g = bm[:, g * SSM_STATE:(g + 1) * SSM_STATE]
            c_g = cm[:, g * SSM_STATE:(g + 1) * SSM_STATE].astype(BF16)
            cb = _dot_nt(c_g, b_g.astype(BF16))
            xg = xdt_b[:, g * gw:(g + 1) * gw]
            y_g = jnp.zeros((BLK, gw), F32)
            for hh in range(hpg):
                h = g * hpg + hh
                seg = a_col[:, h:h + 1] - a_row[h:h + 1, :]
                m_h = (cb * jnp.exp(jnp.where(col <= row, seg, NEG))).astype(BF16)
                y_g = y_g + _dot(m_h, jnp.where(col_head == hh, xg, jnp.zeros_like(xg)))
            state = state_ref[bi, g]
            y_g = y_g + _dot(c_g, state.astype(BF16)) * jnp.exp(a_exp[:, g * gw:(g + 1) * gw])
            new = _dot(b_g.T.astype(BF16), x_end[:, g * gw:(g + 1) * gw])
            state_ref[bi, g] = state * jnp.exp(total[:, g * gw:(g + 1) * gw]) + new
            ys.append(y_g)
        y = jnp.concatenate(ys, axis=-1) + xs * dskip_ref[...]
        y = y * z_ref[bi].astype(F32)
        o_ref[bi] = _rms(y, ng_ref[...]).astype(o_ref.dtype)


def _ssd_call(z, xbc, dt, conv_w, conv_b, dt_bias, a_log, d_skip, norm_g):
    b, s, c = xbc.shape
    di = SSM_D_INNER
    pad = lambda t: jnp.pad(t.astype(F32), (0, LANES - t.shape[0])).reshape(1, LANES)
    tri = jnp.asarray(np.tril(np.ones((BLK, BLK), np.float32)), BF16)
    expand = jnp.asarray(np.arange(LANES)[:, None] == (np.arange(di) // SSM_HEAD_DIM)[None, :], BF16)
    tok = lambda w: pl.BlockSpec((b, BLK, w), lambda ci: (0, ci, 0))
    const = lambda shape: pl.BlockSpec(shape, lambda ci: (0,) * len(shape))
    in_specs = [tok(di), tok(c), tok(LANES), const((SSM_CONV, c)), const((1, c)),
                const((1, LANES)), const((1, LANES)), const((1, di)), const((1, di)),
                const((BLK, BLK)), const((LANES, di))]
    args = [z, xbc, dt, conv_w, conv_b.reshape(1, c), pad(dt_bias), pad(a_log),
            jnp.repeat(d_skip.astype(F32), SSM_HEAD_DIM).reshape(1, di), norm_g.reshape(1, di), tri, expand]
    scratch = [pltpu.VMEM((b, 8, c), F32), pltpu.VMEM((b, 8 + BLK, c), F32),
               pltpu.VMEM((b, SSM_GROUPS, SSM_STATE, di // SSM_GROUPS), F32)]
    return args, in_specs, tok(di), jax.ShapeDtypeStruct((b, s, di), BF16), scratch


def _from_residue_view(src_ref, slab_ref, dil, width):
    if dil == 1:
        return src_ref[...].astype(F32)
    rows = src_ref.shape[0]
    tiles = width // LANES
    for r in range(dil):
        for t in range(tiles):
            col = r * width + t * LANES
            slab_ref[t, pl.ds(r, rows, stride=dil), :] = src_ref[:, col:col + LANES].astype(F32)
    return jnp.concatenate([slab_ref[t] for t in range(tiles)], axis=-1)


def _merge_kernel(x_ref, *refs, dils):
    n = len(dils)
    o_refs, l_refs = refs[:n], refs[n:2 * n]
    yb_ref, yc_ref, yd_ref, gate_ref, wb_ref, wo_ref, hexp_ref, out_ref = refs[2 * n:2 * n + 8]
    o_slabs, l_slabs = refs[2 * n + 8:3 * n + 8], refs[3 * n + 8:]
    d = x_ref.shape[-1]
    bw = yb_ref.shape[-1]
    lses = [_from_residue_view(l_refs[i], l_slabs[i], dils[i], LANES) for i in range(n)]
    m = functools.reduce(jnp.maximum, lses)
    es = [jnp.exp(l - m) for l in lses]
    inv = 1.0 / functools.reduce(jnp.add, es)
    ya = jnp.zeros((x_ref.shape[0], bw), F32)
    for i in range(n):
        wt = _dot((es[i] * inv).astype(BF16), hexp_ref[...])
        ya = ya + wt * _from_residue_view(o_refs[i], o_slabs[i], dils[i], bw)
    branches = (ya.astype(BF16), yb_ref[...], yc_ref[...], yd_ref[...])
    merged = jnp.zeros((x_ref.shape[0], d), F32)
    for i, y in enumerate(branches):
        gate = gate_ref[:, i * d:(i + 1) * d].astype(F32)
        merged = merged + gate * _dot(y, wb_ref[i])
    out_ref[...] = x_ref[...] + _dot(merged.astype(BF16), wo_ref[...])


def _merge(x2d, o_parts, lse_parts, dils, yb, yc, yd, gates, w_branch, w_out, tm=512):
    t, d = x2d.shape
    bw = yb.shape[-1]
    row = lambda w: pl.BlockSpec((tm, w), lambda i: (i, 0))
    view = lambda w, dil: pl.BlockSpec((tm // dil, dil * w), lambda i: (i, 0))
    per_head = LANES // DIL_HEADS
    hexp = jnp.asarray(np.arange(LANES)[:, None] == (np.arange(bw) // DIL_HEAD_DIM * per_head)[None, :],
                       BF16)
    return pl.pallas_call(
        functools.partial(_merge_kernel, dils=tuple(dils)),
        grid=(t // tm,),
        in_specs=[row(d)] + [view(bw, dil) for dil in dils] + [view(LANES, dil) for dil in dils]
                 + [row(bw)] * 3 + [row(N_BRANCH * d),
                                    pl.BlockSpec(w_branch.shape, lambda i: (0, 0, 0),
                                                 pipeline_mode=pl.Buffered(1)),
                                    pl.BlockSpec(w_out.shape, lambda i: (0, 0), pipeline_mode=pl.Buffered(1)),
                                    pl.BlockSpec(hexp.shape, lambda i: (0, 0), pipeline_mode=pl.Buffered(1))],
        out_specs=row(d),
        out_shape=jax.ShapeDtypeStruct((t, d), F32),
        scratch_shapes=[pltpu.VMEM((bw // LANES, tm, LANES), F32) for _ in dils]
                       + [pltpu.VMEM((1, tm, LANES), F32) for _ in dils],
        compiler_params=_params(("parallel",), 48),
        name="merge",
    )(x2d, *o_parts, *lse_parts, yb, yc, yd, gates, w_branch, w_out, hexp)


def _xattn_ffn_kernel(x_ref, gx_ref, wq_ref, k_ref, v_ref, wxo_ref, gf_ref, wa_ref, wb_ref, wo_ref,
                      fg_ref, out_ref, *, final_norm):
    x = x_ref[0]
    q = _dot(_rms(x, gx_ref[...]).astype(BF16), wq_ref[...]).astype(BF16)
    k, v = k_ref[0], v_ref[0]
    outs = []
    for h in range(X_HEADS):
        sl = slice(h * X_HEAD_DIM, (h + 1) * X_HEAD_DIM)
        s = _dot_nt(q[:, sl], k[:, sl]) * (X_HEAD_DIM ** -0.5)
        p = jnp.exp(s - jnp.max(s, axis=-1, keepdims=True))
        a = p / jnp.sum(p, axis=-1, keepdims=True)
        outs.append(_dot(a.astype(BF16), v[:, sl]))
    x = x + _dot(jnp.concatenate(outs, axis=-1).astype(BF16), wxo_ref[...])
    h = _rms(x, gf_ref[...]).astype(BF16)
    a = _dot(h, wa_ref[...])
    b = _dot(h, wb_ref[...])
    y = x + _dot((_silu(a) * b).astype(BF16), wo_ref[...])
    if final_norm:
        y = _rms(y, fg_ref[...])
    out_ref[0] = y


def _xattn_ffn(x, gx, w_q, k, v, w_xo, gf, w_a, w_b, w_o, final_g, final_norm, tm=512):
    b, s, d = x.shape
    m, w = k.shape[1], k.shape[2]
    full = lambda wt: pl.BlockSpec(wt.shape, lambda bi, i: (0, 0), pipeline_mode=pl.Buffered(1))
    vec = pl.BlockSpec((1, d), lambda bi, i: (0, 0))
    mem = pl.BlockSpec((1, m, w), lambda bi, i: (bi, 0, 0))
    return pl.pallas_call(
        functools.partial(_xattn_ffn_kernel, final_norm=final_norm),
        grid=(b, s // tm),
        in_specs=[pl.BlockSpec((1, tm, d), lambda bi, i: (bi, i, 0)), vec, full(w_q), mem, mem,
                  full(w_xo), vec, full(w_a), full(w_b), full(w_o), vec],
        out_specs=pl.BlockSpec((1, tm, d), lambda bi, i: (bi, i, 0)),
        out_shape=jax.ShapeDtypeStruct((b, s, d), F32),
        compiler_params=_params(("parallel", "parallel"), 56),
        name="xattn_ffn",
    )(x, gx.reshape(1, d), w_q, k, v, w_xo, gf.reshape(1, d), w_a, w_b, w_o, final_g.reshape(1, d))


def _ret_feature_order():
    quarter = RET_KEY_DIM // 2
    first = [h * RET_KEY_DIM + f for h in range(RET_HEADS) for f in range(quarter)]
    return np.array(first + [c + quarter for c in first])


def _layer(x, mem, p, lam_init, final_g, final_norm):
    b, s, d = x.shape
    t = b * s
    x2d = x.reshape(t, d)
    a_w = DIL_HEADS * DIL_HEAD_DIM
    b_qk, b_vg = RET_HEADS * RET_KEY_DIM, RET_HEADS * RET_VAL_DIM
    c_w = DIFF_HEADS * 2 * DIFF_HEAD_DIM
    d_xbc = SSM_D_INNER + 2 * SSM_GROUPS * SSM_STATE
    sizes = (a_w, a_w, a_w, b_qk, b_qk, b_vg, b_vg, c_w, c_w, c_w, SSM_D_INNER, d_xbc, SSM_HEADS,
             N_BRANCH * d)
    offs = np.concatenate([[0], np.cumsum(sizes)])
    w_in = p['w_in']
    col = lambda i: w_in[:, offs[i]:offs[i + 1]]
    order = _ret_feature_order()
    w_dt = jnp.pad(col(12), ((0, 0), (0, LANES - SSM_HEADS)))
    bf = lambda w: w.astype(BF16)

    dils = [dil for _, dil in DIL_PATTERNS]
    nd = len(dils)
    proj = _proj_views(x2d, p['norm_mix_g'], [bf(col(i)) for i in (0, 1, 2)],
                       [DIL_HEAD_DIM ** -0.5, 1.0, 1.0], [bf(col(i)) for i in (7, 8, 9)],
                       [DIFF_HEAD_DIM ** -0.5, 1.0, 1.0], dils)
    aq, ak, av = proj[0:nd], proj[nd:2 * nd], proj[2 * nd:3 * nd]
    cq, ck, cv = proj[3 * nd:]
    bq, bk, bv, bg, dz, dxbc, ddt = _norm_matmul(
        x2d, p['norm_mix_g'],
        [bf(col(3)[:, order]), bf(col(4)[:, order]), bf(col(5)), bf(col(6)), bf(col(10)), bf(col(11)),
         bf(w_dt)],
        [F32, F32, BF16, BF16, BF16, F32, F32], [None, None, None, _silu, _silu, None, None], tm=1024)
    (gates,) = _norm_matmul(x2d, p['norm_mix_g'], [bf(col(13))], [BF16], [_sigmoid], tm=1024)

    r3 = lambda y: y.reshape(b, s, y.shape[-1])
    o_parts, lse_parts = _band_attention(aq, ak, av, b, s)
    y_b, y_d = _recurrent_mixers(
        _retention_call(r3(bq), r3(bk), r3(bv), r3(bg), p['ret_gn_g']),
        _ssd_call(r3(dz), r3(dxbc), r3(ddt), p['ssm_conv_w'], p['ssm_conv_b'], p['ssm_dt_bias'],
                  p['ssm_A_log'], p['ssm_D'], p['ssm_norm_g']), s // BLK)
    y_c = _diff_attention(r3(cq), r3(ck), r3(cv), p['diff_lambda'], p['diff_subln_g'], lam_init)
    x2d = _merge(x2d, o_parts, lse_parts, dils, y_b.reshape(t, -1), y_c.reshape(t, -1),
                 y_d.reshape(t, -1), gates, bf(p['w_branch']), bf(p['w_mix_out']))

    xw = X_HEADS * X_HEAD_DIM
    m2d = mem.reshape(-1, d)
    mk, mv = _norm_matmul(m2d, p['norm_mem_g'], [bf(p['w_xkv'][:, :xw]), bf(p['w_xkv'][:, xw:])],
                          [BF16, BF16], [None, None], tm=256)
    mlen = mem.shape[1]
    hid = p['w_ffn_out'].shape[0]
    return _xattn_ffn(x2d.reshape(b, s, d), p['norm_x_g'], bf(p['w_xq']), mk.reshape(b, mlen, xw),
                      mv.reshape(b, mlen, xw), bf(p['w_xo']), p['norm_ffn_g'], bf(p['w_ffn_in'][:, :hid]),
                      bf(p['w_ffn_in'][:, hid:]), bf(p['w_ffn_out']), final_g, final_norm)


def kernel(x, mem, norm_mix_g, w_in, ret_gn_g, diff_lambda, diff_subln_g, ssm_conv_w, ssm_conv_b,
           ssm_dt_bias, ssm_A_log, ssm_D, ssm_norm_g, w_branch, w_mix_out, norm_x_g, norm_mem_g, w_xq,
           w_xkv, w_xo, norm_ffn_g, w_ffn_in, w_ffn_out, norm_f_g):
    stacked = dict(norm_mix_g=norm_mix_g, w_in=w_in, ret_gn_g=ret_gn_g, diff_lambda=diff_lambda,
                   diff_subln_g=diff_subln_g, ssm_conv_w=ssm_conv_w, ssm_conv_b=ssm_conv_b,
                   ssm_dt_bias=ssm_dt_bias, ssm_A_log=ssm_A_log, ssm_D=ssm_D, ssm_norm_g=ssm_norm_g,
                   w_branch=w_branch, w_mix_out=w_mix_out, norm_x_g=norm_x_g, norm_mem_g=norm_mem_g,
                   w_xq=w_xq, w_xkv=w_xkv, w_xo=w_xo, norm_ffn_g=norm_ffn_g, w_ffn_in=w_ffn_in,
                   w_ffn_out=w_ffn_out)
    depth = w_in.shape[0]
    for l in range(depth):
        lam_init = 0.8 - 0.6 * math.exp(-0.3 * l)
        p = {name: val[l] for name, val in stacked.items()}
        x = _layer(x, mem, p, lam_init, norm_f_g, final_norm=(l == depth - 1))
    return x
```

```python
import functools
import math

import jax
import jax.numpy as jnp
import numpy as np
from jax import lax
from jax.experimental import pallas as pl
from jax.experimental.pallas import tpu as pltpu

F32 = jnp.float32
BF16 = jnp.bfloat16

EPS = 1e-6
NEG = -1e30
LOG2E = math.log2(math.e)
BLK = 128

DIL_HEADS = 8
DIL_HEAD_DIM = 64
DIL_PATTERNS = ((128, 1), (512, 4), (2048, 16))
RET_HEADS = 4
RET_KEY_DIM = 64
RET_VAL_DIM = 128
ROPE_BASE = 10000.0
DIFF_HEADS = 4
DIFF_HEAD_DIM = 64
SSM_D_INNER = 512
SSM_HEAD_DIM = 64
SSM_HEADS = 8
SSM_GROUPS = 2
SSM_STATE = 128
SSM_CONV = 4
N_BRANCH = 4
BRANCH_WIDTH = 512
X_HEADS = 4
X_HEAD_DIM = 128
LANES = 128
STAGE_RATIO = 4

MIB = 1024 * 1024


def _params(semantics, vmem_mib):
    return pltpu.CompilerParams(dimension_semantics=semantics,
                                vmem_limit_bytes=vmem_mib * MIB)


def _dot(a, b):
    return jnp.dot(a, b, preferred_element_type=F32)


def _dot_nt(a, b):
    return lax.dot_general(a, b, (((1,), (1,)), ((), ())), preferred_element_type=F32)


def _split3(x):
    x1 = x.astype(BF16)
    r1 = x - x1.astype(F32)
    x2 = r1.astype(BF16)
    x3 = (r1 - x2.astype(F32)).astype(BF16)
    return x1, x2, x3


def _dot_sel(x, sel):
    x1, x2, x3 = _split3(x)
    return _dot(x1, sel) + _dot(x2, sel) + _dot(x3, sel)


def _sel_dot(sel, x):
    x1, x2, x3 = _split3(x)
    return _dot(sel, x1) + _dot(sel, x2) + _dot(sel, x3)


def _sigmoid(x):
    return 0.5 * jnp.tanh(0.5 * x) + 0.5


def _silu(x):
    h = 0.5 * x
    return h * jnp.tanh(h) + h


def _rms(x, g):
    ms = jnp.mean(x * x, axis=-1, keepdims=True)
    return x * lax.rsqrt(ms + EPS) * g


def _norm_matmul_kernel(x_ref, g_ref, *refs, n_out, acts):
    w_refs, o_refs = refs[:n_out], refs[n_out:]
    h = _rms(x_ref[...], g_ref[...]).astype(BF16)
    for w_ref, o_ref, act in zip(w_refs, o_refs, acts):
        y = _dot(h, w_ref[...])
        if act is not None:
            y = act(y)
        o_ref[...] = y.astype(o_ref.dtype)


def _norm_matmul(x2d, g, ws, out_dtypes, acts, tm, vmem_mib=48):
    t, d = x2d.shape
    in_specs = [pl.BlockSpec((tm, d), lambda i: (i, 0)),
                pl.BlockSpec((1, d), lambda i: (0, 0))]
    in_specs += [pl.BlockSpec(w.shape, lambda i: (0, 0), pipeline_mode=pl.Buffered(1)) for w in ws]
    out_specs = [pl.BlockSpec((tm, w.shape[1]), lambda i: (i, 0)) for w in ws]
    out_shape = [jax.ShapeDtypeStruct((t, w.shape[1]), dt) for w, dt in zip(ws, out_dtypes)]
    return pl.pallas_call(
        functools.partial(_norm_matmul_kernel, n_out=len(ws), acts=tuple(acts)),
        grid=(t // tm,), in_specs=in_specs, out_specs=out_specs, out_shape=out_shape,
        compiler_params=_params(("parallel",), vmem_mib),
        name="norm_matmul",
    )(x2d, g.reshape(1, d), *ws)


def _proj_views_kernel(x_ref, g_ref, *refs, n_view, n_plain, dils, view_scales, plain_scales):
    nd = len(dils)
    n_w = n_view + n_plain
    w_refs = refs[:n_w]
    view_refs = refs[n_w:n_w + n_view * nd]
    plain_refs = refs[n_w + n_view * nd:n_w + n_view * nd + n_plain]
    slab_ref, staged_ref = refs[-2:]
    tm = x_ref.shape[0]
    h = _rms(x_ref[...], g_ref[...]).astype(BF16)
    for i in range(n_view):
        y = _dot(h, w_refs[i][...])
        if view_scales[i] != 1.0:
            y = y * view_scales[i]
        width = y.shape[1]
        tiles = width // LANES
        for t in range(tiles):
            slab_ref[t] = y[:, t * LANES:(t + 1) * LANES]
        staged = None
        for di, dil in enumerate(dils):
            o_ref = view_refs[i * nd + di]
            if dil == 1:
                o_ref[...] = y.astype(o_ref.dtype)
                continue
            from_staged = staged is not None and dil == STAGE_RATIO * staged
            keep = (not from_staged) and (STAGE_RATIO * dil in dils)
            for r in range(dil):
                for t in range(tiles):
                    if from_staged:
                        src = staged_ref[(r % staged) * tiles + t, pl.ds(r // staged, tm // dil, stride=STAGE_RATIO), :]
                    else:
                        src = slab_ref[t, pl.ds(r, tm // dil, stride=dil), :]
                        if keep:
                            staged_ref[r * tiles + t, 0:tm // dil, :] = src
                    col = r * width + t * LANES
                    o_ref[:, col:col + LANES] = src.astype(o_ref.dtype)
            if keep:
                staged = dil
    for i in range(n_plain):
        y = _dot(h, w_refs[n_view + i][...])
        if plain_scales[i] != 1.0:
            y = y * plain_scales[i]
        plain_refs[i][...] = y.astype(plain_refs[i].dtype)


def _proj_views(x2d, g, view_ws, view_scales, plain_ws, plain_scales, dils, tm=1024):
    t, d = x2d.shape
    ws = list(view_ws) + list(plain_ws)
    full = lambda w: pl.BlockSpec(w.shape, lambda i: (0, 0), pipeline_mode=pl.Buffered(1))
    out_specs, out_shape = [], []
    for w in view_ws:
        n = w.shape[1]
        for dil in dils:
            out_specs.append(pl.BlockSpec((tm // dil, dil * n), lambda i: (i, 0)))
            out_shape.append(jax.ShapeDtypeStruct((t // dil, dil * n), BF16))
    for w in plain_ws:
        out_specs.append(pl.BlockSpec((tm, w.shape[1]), lambda i: (i, 0)))
        out_shape.append(jax.ShapeDtypeStruct((t, w.shape[1]), BF16))
    width = view_ws[0].shape[1]
    stage = next(dil for dil in dils if dil > 1 and STAGE_RATIO * dil in dils)
    return pl.pallas_call(
        functools.partial(_proj_views_kernel, n_view=len(view_ws), n_plain=len(plain_ws), dils=tuple(dils),
                          view_scales=tuple(view_scales), plain_scales=tuple(plain_scales)),
        grid=(t // tm,),
        in_specs=[pl.BlockSpec((tm, d), lambda i: (i, 0)), pl.BlockSpec((1, d), lambda i: (0, 0))]
                 + [full(w) for w in ws],
        out_specs=out_specs, out_shape=out_shape,
        scratch_shapes=[pltpu.VMEM((width // LANES, tm, LANES), F32),
                        pltpu.VMEM((stage * width // LANES, tm // stage, LANES), F32)],
        compiler_params=_params(("parallel",), 48),
        name="proj_views",
    )(x2d, g.reshape(1, d), *ws)


def _band_attn_kernel(*refs, steps_per_residue):
    n_pat = len(steps_per_residue)
    ins, outs = refs[:6 * n_pat], refs[6 * n_pat:]
    step = pl.program_id(1)
    for i in range(n_pat):
        bias_ref, q_ref, kp_ref, kc_ref, vp_ref, vc_ref = ins[6 * i:6 * i + 6]
        o_ref, lse_ref = outs[2 * i:2 * i + 2]
        first = jnp.minimum(step % steps_per_residue[i], 1)
        lo, hi = slice(0, BLK), slice(BLK, 2 * BLK)
        _band_block(bias_ref[first], q_ref.at[0, lo], kp_ref.at[0], kc_ref.at[0, lo], vp_ref.at[0],
                    vc_ref.at[0, lo], o_ref.at[0, lo], lse_ref.at[0, lo])
        _band_block(bias_ref[1], q_ref.at[0, hi], kc_ref.at[0, lo], kc_ref.at[0, hi], vc_ref.at[0, lo],
                    vc_ref.at[0, hi], o_ref.at[0, hi], lse_ref.at[0, hi])


def _band_block(bias, q_ref, kp_ref, kc_ref, vp_ref, vc_ref, o_ref, lse_ref):
    hd = DIL_HEAD_DIM
    lane = lax.broadcasted_iota(jnp.int32, (1, LANES), 1)
    ones = jnp.ones((2 * BLK, LANES), BF16)
    lse_tile = jnp.zeros((BLK, LANES), F32)
    for t in range(DIL_HEADS * hd // LANES):
        sl = slice(t * LANES, (t + 1) * LANES)
        q = q_ref[:, sl]
        k2 = jnp.concatenate([kp_ref[:, sl], kc_ref[:, sl]], axis=0)
        v2 = jnp.concatenate([vp_ref[:, sl], vc_ref[:, sl]], axis=0)
        v_ext = jnp.concatenate([v2, ones], axis=-1)
        pair = []
        for half in range(LANES // hd):
            in_head = (lane // hd) == half
            s = _dot_nt(jnp.where(in_head, q, jnp.zeros_like(q)), k2) + bias
            m = jnp.max(jnp.maximum(s[:, :BLK], s[:, BLK:]), axis=-1, keepdims=True)
            p = jnp.exp((s - m).astype(BF16))
            pv = _dot(p, v_ext)
            den = pv[:, LANES:]
            pair.append(pv[:, :LANES] / den)
            h = t * (LANES // hd) + half
            per_head = LANES // DIL_HEADS
            lse_tile = jnp.where((lane // per_head) == h, m + jnp.log(den), lse_tile)
        o_ref[:, sl] = jnp.where((lane // hd) == 0, pair[0], pair[1]).astype(o_ref.dtype)
    lse_ref[...] = lse_tile


def _band_bias(window):
    i = np.arange(BLK)[:, None]
    j = np.arange(2 * BLK)[None, :]
    dist = BLK + i - j
    band = (dist >= 0) & (dist <= window)
    first = band & (j >= BLK)
    return np.where(np.stack([first, band]), 0.0, NEG).astype(np.float32)


def _band_attention(qs, ks, vs, batch, seq):
    w = DIL_HEADS * DIL_HEAD_DIM
    steps = seq // (2 * BLK)
    args, in_specs, out_specs, out_shape, per_residue = [], [], [], [], []
    for (window, dil), q, k, v in zip(DIL_PATTERNS, qs, ks, vs):
        l = seq // dil
        ns = l // (2 * BLK)
        per_residue.append(ns)
        view = lambda t, l=l, dil=dil: t.reshape(batch, l, dil * w)
        cur = lambda wd, ns=ns: pl.BlockSpec((1, 2 * BLK, wd), lambda bi, n: (bi, n % ns, n // ns))
        prev = pl.BlockSpec((1, BLK, w),
                            lambda bi, n, ns=ns: (bi, jnp.maximum(2 * (n % ns) - 1, 0), n // ns))
        bias = pl.BlockSpec((2, BLK, 2 * BLK), lambda bi, n: (0, 0, 0))
        args += [_band_bias(window // dil), view(q), view(k), view(k), view(v), view(v)]
        in_specs += [bias, cur(w), prev, cur(w), prev, cur(w)]
        out_specs += [cur(w), cur(LANES)]
        out_shape += [jax.ShapeDtypeStruct((batch, l, dil * w), BF16),
                      jax.ShapeDtypeStruct((batch, l, dil * LANES), F32)]
    outs = pl.pallas_call(
        functools.partial(_band_attn_kernel, steps_per_residue=tuple(per_residue)),
        grid=(batch, steps),
        in_specs=in_specs, out_specs=out_specs, out_shape=out_shape,
        compiler_params=_params(("parallel", "parallel"), 32),
        name="band_attention",
    )(*args)
    flat = [o.reshape(-1, o.shape[-1]) for o in outs]
    return flat[0::2], flat[1::2]


def _retention_kernel(q_ref, k_ref, v_ref, g_ref, cos_ref, sin_ref, dq_ref, dk_ref, dec_ref,
                      cdec_ref, gn_ref, o_ref, state_ref):
    @pl.when(pl.program_id(0) == 0)
    def _():
        state_ref[...] = jnp.zeros_like(state_ref)

    cos, sin = cos_ref[...], sin_ref[...]
    half = RET_HEADS * RET_KEY_DIM // 2

    def rot(t):
        t1, t2 = t[:, :half], t[:, half:]
        return jnp.concatenate([t1 * cos - t2 * sin, t1 * sin + t2 * cos], axis=-1)

    lane_head = (lax.broadcasted_iota(jnp.int32, (1, 2 * half), 1) % half) // (RET_KEY_DIM // 2)
    for bi in range(q_ref.shape[0]):
        rq = rot(q_ref[bi])
        rk = rot(k_ref[bi]) * (RET_KEY_DIM ** -0.5)
        v = v_ref[bi]
        rkb = rk.astype(BF16)
        state = state_ref[bi]
        o_cross = _dot((rq * dq_ref[...]).astype(BF16), state.astype(BF16))
        outs = []
        for h in range(RET_HEADS):
            qh = jnp.where(lane_head == h, rq, 0.0).astype(BF16)
            s = _dot_nt(qh, rkb) * dec_ref[h]
            outs.append(_dot(s.astype(BF16), v[:, h * RET_VAL_DIM:(h + 1) * RET_VAL_DIM]))
        o = jnp.concatenate(outs, axis=-1) + o_cross
        k_end_t = (rk * dk_ref[...]).T.astype(BF16)
        kv = _dot(k_end_t, v)
        row_head = (lax.broadcasted_iota(jnp.int32, kv.shape, 0) % half) // (RET_KEY_DIM // 2)
        col_head = lax.broadcasted_iota(jnp.int32, kv.shape, 1) // RET_VAL_DIM
        state_ref[bi] = state * cdec_ref[...] + jnp.where(row_head == col_head, kv, 0.0)
        normed = []
        for h in range(RET_HEADS):
            oh = o[:, h * RET_VAL_DIM:(h + 1) * RET_VAL_DIM]
            mu = jnp.mean(oh, axis=-1, keepdims=True)
            cen = oh - mu
            var = jnp.mean(cen * cen, axis=-1, keepdims=True)
            normed.append(cen * lax.rsqrt(var + EPS))
        y = jnp.concatenate(normed, axis=-1) * gn_ref[...]
        o_ref[bi] = (g_ref[bi].astype(F32) * y).astype(o_ref.dtype)


def _retention_tables(s):
    h, dk, dv = RET_HEADS, RET_KEY_DIM, RET_VAL_DIM
    quarter = dk // 2
    pos = np.arange(s, dtype=np.float64)
    inv_freq = ROPE_BASE ** (-np.arange(quarter, dtype=np.float64) / quarter)
    ang = pos[:, None] * np.tile(inv_freq, h)[None, :]
    log_gamma = np.log1p(-np.exp2(-5.0 - np.arange(h, dtype=np.float64)))
    idx = np.arange(BLK, dtype=np.float64)
    rel = idx[:, None] - idx[None, :]
    decay = np.where(rel >= 0, np.exp(log_gamma[:, None, None] * np.maximum(rel, 0.0)), 0.0)
    lane_gamma = np.tile(np.repeat(log_gamma, quarter), 2)
    dq = np.exp((idx + 1.0)[:, None] * lane_gamma[None, :])
    dkt = np.exp((BLK - 1 - idx)[:, None] * lane_gamma[None, :])
    cdec = np.repeat(np.exp(log_gamma * BLK), dv)[None, :]
    return tuple(t.astype(np.float32) for t in (np.cos(ang), np.sin(ang), dq, dkt, decay, cdec))


def _retention_call(q, k, v, g, gn_g):
    b, s, wk = q.shape
    wv = v.shape[-1]
    cos, sin, dq, dkt, decay, cdec = _retention_tables(s)
    tok = lambda w: pl.BlockSpec((b, BLK, w), lambda c: (0, c, 0))
    const = lambda shape: pl.BlockSpec(shape, lambda c: (0,) * len(shape))
    in_specs = [tok(wk), tok(wk), tok(wv), tok(wv),
                pl.BlockSpec((BLK, wk // 2), lambda c: (c, 0)),
                pl.BlockSpec((BLK, wk // 2), lambda c: (c, 0)),
                const((BLK, wk)), const((BLK, wk)), const((RET_HEADS, BLK, BLK)),
                const((1, wv)), const((1, wv))]
    args = [q, k, v, g, cos, sin, dq, dkt, decay, cdec, gn_g.reshape(1, wv)]
    return (args, in_specs, tok(wv), jax.ShapeDtypeStruct((b, s, wv), BF16),
            [pltpu.VMEM((b, wk, wv), F32)])


def _recurrent_mixers_kernel(*refs, n_ret_in, n_ssd_in):
    ret_in, ssd_in = refs[:n_ret_in], refs[n_ret_in:n_ret_in + n_ssd_in]
    ret_out, ssd_out = refs[n_ret_in + n_ssd_in:n_ret_in + n_ssd_in + 2]
    scratch = refs[n_ret_in + n_ssd_in + 2:]
    _retention_kernel(*ret_in, ret_out, scratch[0])
    _ssd_kernel(*ssd_in, ssd_out, *scratch[1:])


def _recurrent_mixers(ret_call, ssd_call, steps):
    (ra, rs, ro, rshape, rscr), (sa, ss, so, sshape, sscr) = ret_call, ssd_call
    return pl.pallas_call(
        functools.partial(_recurrent_mixers_kernel, n_ret_in=len(ra), n_ssd_in=len(sa)),
        grid=(steps,),
        in_specs=rs + ss, out_specs=[ro, so], out_shape=[rshape, sshape],
        scratch_shapes=rscr + sscr,
        compiler_params=_params(("arbitrary",), 40),
        name="recurrent_mixers",
    )(*ra, *sa)


def _diff_attn_kernel(lam_ref, g_ref, q_ref, k_ref, v_ref, o_ref, s_ref, m_ref, vext_ref, acc_ref, *,
                      tq, lam_init):
    d = DIFF_HEAD_DIM
    hw = 2 * d
    nl = tq // LANES
    qi = pl.program_id(2)
    q = q_ref[0]
    qs = (q[:, :d], q[:, d:])

    @pl.when(qi == 0)
    def _():
        vext_ref[:, :hw] = v_ref[0]
        vext_ref[:, hw:] = jnp.ones((vext_ref.shape[0], hw), BF16)

    def fold_max(x):
        r = x[:, :LANES]
        for c in range(1, nl):
            r = jnp.maximum(r, x[:, c * LANES:(c + 1) * LANES])
        return r

    m_ref[...] = jnp.full(m_ref.shape, NEG, F32)

    def scores(j, diagonal):
        k = k_ref[0, pl.ds(pl.multiple_of(j * tq, tq), tq), :]
        for c in range(2):
            s = _dot_nt(qs[c], k[:, c * d:(c + 1) * d]) * LOG2E
            if diagonal:
                row = lax.broadcasted_iota(jnp.int32, (tq, tq), 0)
                col = lax.broadcasted_iota(jnp.int32, (tq, tq), 1)
                s = jnp.where(col <= row, s, NEG)
            s_ref[c, j] = s
            m_ref[c] = jnp.maximum(m_ref[c], fold_max(s))

    def for_each_block(n, body):
        def quad(jj, carry):
            for u in range(4):
                body(4 * jj + u)
            return carry

        lax.fori_loop(0, lax.shift_right_logical(n, 2), quad, 0)
        done = n & ~3

        @pl.when((n & 2) == 2)
        def _():
            body(done)
            body(done + 1)

        @pl.when((n & 1) == 1)
        def _():
            body(n - 1)

    for_each_block(qi, lambda j: scores(j, False))
    scores(qi, True)

    for c in range(2):
        m_ref[c] = jnp.broadcast_to(jnp.max(m_ref[c], axis=-1, keepdims=True), (tq, LANES))
    acc_ref[...] = jnp.zeros(acc_ref.shape, F32)

    def accumulate(j):
        v = vext_ref[pl.ds(pl.multiple_of(j * tq, tq), tq), :]
        for c in range(2):
            m = m_ref[c]
            s = s_ref[c, j]
            p = jnp.concatenate([jnp.exp2(s[:, t * LANES:(t + 1) * LANES] - m) for t in range(nl)], axis=-1)
            acc_ref[c] += _dot(p.astype(BF16), v)

    for_each_block(qi + 1, accumulate)

    lp = lam_ref[...]
    lam = (jnp.exp(jnp.sum(lp[0:1] * lp[1:2], keepdims=True))
           - jnp.exp(jnp.sum(lp[2:3] * lp[3:4], keepdims=True)) + lam_init)
    o1 = acc_ref[0, :, :hw] / acc_ref[0, :, hw:]
    o2 = acc_ref[1, :, :hw] / acc_ref[1, :, hw:]
    o_ref[0] = (_rms(o1 - lam * o2, g_ref[...]) * (1.0 - lam_init)).astype(o_ref.dtype)


def _diff_attention(q, k, v, lam_params, subln_g, lam_init, tq=512):
    b, s, w = q.shape
    hw = 2 * DIFF_HEAD_DIM
    return pl.pallas_call(
        functools.partial(_diff_attn_kernel, tq=tq, lam_init=lam_init),
        grid=(b, DIFF_HEADS, s // tq),
        in_specs=[pl.BlockSpec(lam_params.shape, lambda bi, h, i: (0, 0)),
                  pl.BlockSpec((1, hw), lambda bi, h, i: (0, 0)),
                  pl.BlockSpec((1, tq, hw), lambda bi, h, i: (bi, i, h)),
                  pl.BlockSpec((1, s, hw), lambda bi, h, i: (bi, 0, h)),
                  pl.BlockSpec((1, s, hw), lambda bi, h, i: (bi, 0, h))],
        out_specs=pl.BlockSpec((1, tq, hw), lambda bi, h, i: (bi, i, h)),
        out_shape=jax.ShapeDtypeStruct((b, s, w), BF16),
        scratch_shapes=[pltpu.VMEM((2, s // tq, tq, tq), F32), pltpu.VMEM((2, tq, LANES), F32),
                        pltpu.VMEM((s, 2 * hw), BF16), pltpu.VMEM((2, tq, 2 * hw), F32)],
        compiler_params=_params(("parallel", "parallel", "arbitrary"), 40),
        name="diff_attention",
    )(lam_params, subln_g.reshape(1, hw), q, k, v)


def _ssd_kernel(z_ref, xbc_ref, dt_ref, cw_ref, cb_ref, dtb_ref, alog_ref, dskip_ref, ng_ref,
                tri_ref, exp_ref, o_ref, carry_ref, ext_ref, state_ref):
    gn = SSM_GROUPS * SSM_STATE
    hpg = SSM_HEADS // SSM_GROUPS
    gw = hpg * SSM_HEAD_DIM
    tail = 8

    @pl.when(pl.program_id(0) == 0)
    def _():
        carry_ref[...] = jnp.zeros_like(carry_ref)
        state_ref[...] = jnp.zeros_like(state_ref)

    cw = cw_ref[...]
    lane = lax.broadcasted_iota(jnp.int32, (1, LANES), 1)
    a_neg = jnp.where(lane < SSM_HEADS, -jnp.exp(alog_ref[...]), 0.0)
    expand = exp_ref[...]
    row = lax.broadcasted_iota(jnp.int32, (BLK, BLK), 0)
    col = lax.broadcasted_iota(jnp.int32, (BLK, BLK), 1)
    col_head = lax.broadcasted_iota(jnp.int32, (1, gw), 1) // SSM_HEAD_DIM
    for bi in range(z_ref.shape[0]):
        xbc = xbc_ref[bi]
        ext_ref[bi, 0:tail, :] = carry_ref[bi]
        ext_ref[bi, tail:tail + BLK, :] = xbc
        carry_ref[bi] = xbc[BLK - tail:, :]
        conv = cb_ref[...] + cw[SSM_CONV - 1:SSM_CONV, :] * xbc
        for sft in range(1, SSM_CONV):
            conv = conv + (cw[SSM_CONV - 1 - sft:SSM_CONV - sft, :]
                           * ext_ref[bi, tail - sft:tail - sft + BLK, :])
        xc = _silu(conv)
        xs, bm, cm = xc[:, :SSM_D_INNER], xc[:, SSM_D_INNER:SSM_D_INNER + gn], xc[:, SSM_D_INNER + gn:]

        pre = dt_ref[bi] + dtb_ref[...]
        dt = jnp.maximum(pre, 0.0) + jnp.log1p(jnp.exp(-jnp.abs(pre)))
        a_col = _sel_dot(tri_ref[...], dt * a_neg)
        a_row = a_col.T
        a_exp = _dot_sel(a_col, expand)
        dt_exp = _dot_sel(dt, expand)
        total = a_exp[BLK - 1:BLK, :]
        xdt = xs * dt_exp
        x_end = (xdt * jnp.exp(total - a_exp)).astype(BF16)
        xdt_b = xdt.astype(BF16)
        ys = []
        for g in range(SSM_GROUPS):
            b_g = bm[:, g * SSM_STATE:(g + 1) * SSM_STATE]
            c_g = cm[:, g * SSM_STATE:(g + 1) * SSM_STATE].astype(BF16)
            cb = _dot_nt(c_g, b_g.astype(BF16))
            xg = xdt_b[:, g * gw:(g + 1) * gw]
            y_g = jnp.zeros((BLK, gw), F32)
            for hh in range(hpg):
                h = g * hpg + hh
                seg = a_col[:, h:h + 1] - a_row[h:h + 1, :]
                m_h = (cb * jnp.exp(jnp.where(col <= row, seg, NEG))).astype(BF16)
                y_g = y_g + _dot(m_h, jnp.where(col_head == hh, xg, jnp.zeros_like(xg)))
            state = state_ref[bi, g]
            y_g = y_g + _dot(c_g, state.astype(BF16)) * jnp.exp(a_exp[:, g * gw:(g + 1) * gw])
            new = _dot(b_g.T.astype(BF16), x_end[:, g * gw:(g + 1) * gw])
            state_ref[bi, g] = state * jnp.exp(total[:, g * gw:(g + 1) * gw]) + new
            ys.append(y_g)
        y = jnp.concatenate(ys, axis=-1) + xs * dskip_ref[...]
        y = y * z_ref[bi].astype(F32)
        o_ref[bi] = _rms(y, ng_ref[...]).astype(o_ref.dtype)


def _ssd_call(z, xbc, dt, conv_w, conv_b, dt_bias, a_log, d_skip, norm_g):
    b, s, c = xbc.shape
    di = SSM_D_INNER
    pad = lambda t: jnp.pad(t.astype(F32), (0, LANES - t.shape[0])).reshape(1, LANES)
    tri = jnp.asarray(np.tril(np.ones((BLK, BLK), np.float32)), BF16)
    expand = jnp.asarray(np.arange(LANES)[:, None] == (np.arange(di) // SSM_HEAD_DIM)[None, :], BF16)
    tok = lambda w: pl.BlockSpec((b, BLK, w), lambda ci: (0, ci, 0))
    const = lambda shape: pl.BlockSpec(shape, lambda ci: (0,) * len(shape))
    in_specs = [tok(di), tok(c), tok(LANES), const((SSM_CONV, c)), const((1, c)),
                const((1, LANES)), const((1, LANES)), const((1, di)), const((1, di)),
                const((BLK, BLK)), const((LANES, di))]
    args = [z, xbc, dt, conv_w, conv_b.reshape(1, c), pad(dt_bias), pad(a_log),
            jnp.repeat(d_skip.astype(F32), SSM_HEAD_DIM).reshape(1, di), norm_g.reshape(1, di), tri, expand]
    scratch = [pltpu.VMEM((b, 8, c), F32), pltpu.VMEM((b, 8 + BLK, c), F32),
               pltpu.VMEM((b, SSM_GROUPS, SSM_STATE, di // SSM_GROUPS), F32)]
    return args, in_specs, tok(di), jax.ShapeDtypeStruct((b, s, di), BF16), scratch


def _from_residue_view(src_ref, slab_ref, dil, width):
    if dil == 1:
        return src_ref[...].astype(F32)
    rows = src_ref.shape[0]
    tiles = width // LANES
    for r in range(dil):
        for t in range(tiles):
            col = r * width + t * LANES
            slab_ref[t, pl.ds(r, rows, stride=dil), :] = src_ref[:, col:col + LANES].astype(F32)
    return jnp.concatenate([slab_ref[t] for t in range(tiles)], axis=-1)


def _merge_kernel(x_ref, *refs, dils):
    n = len(dils)
    o_refs, l_refs = refs[:n], refs[n:2 * n]
    yb_ref, yc_ref, yd_ref, gate_ref, wb_ref, wo_ref, hexp_ref, out_ref = refs[2 * n:2 * n + 8]
    o_slabs, l_slabs = refs[2 * n + 8:3 * n + 8], refs[3 * n + 8:]
    d = x_ref.shape[-1]
    bw = yb_ref.shape[-1]
    lses = [_from_residue_view(l_refs[i], l_slabs[i], dils[i], LANES) for i in range(n)]
    m = functools.reduce(jnp.maximum, lses)
    es = [jnp.exp(l - m) for l in lses]
    inv = 1.0 / functools.reduce(jnp.add, es)
    ya = jnp.zeros((x_ref.shape[0], bw), F32)
    for i in range(n):
        wt = _dot((es[i] * inv).astype(BF16), hexp_ref[...])
        ya = ya + wt * _from_residue_view(o_refs[i], o_slabs[i], dils[i], bw)
    branches = (ya.astype(BF16), yb_ref[...], yc_ref[...], yd_ref[...])
    merged = jnp.zeros((x_ref.shape[0], d), F32)
    for i, y in enumerate(branches):
        gate = gate_ref[:, i * d:(i + 1) * d].astype(F32)
        merged = merged + gate * _dot(y, wb_ref[i])
    out_ref[...] = x_ref[...] + _dot(merged.astype(BF16), wo_ref[...])


def _merge(x2d, o_parts, lse_parts, dils, yb, yc, yd, gates, w_branch, w_out, tm=512):
    t, d = x2d.shape
    bw = yb.shape[-1]
    row = lambda w: pl.BlockSpec((tm, w), lambda i: (i, 0))
    view = lambda w, dil: pl.BlockSpec((tm // dil, dil * w), lambda i: (i, 0))
    per_head = LANES // DIL_HEADS
    hexp = jnp.asarray(np.arange(LANES)[:, None] == (np.arange(bw) // DIL_HEAD_DIM * per_head)[None, :],
                       BF16)
    return pl.pallas_call(
        functools.partial(_merge_kernel, dils=tuple(dils)),
        grid=(t // tm,),
        in_specs=[row(d)] + [view(bw, dil) for dil in dils] + [view(LANES, dil) for dil in dils]
                 + [row(bw)] * 3 + [row(N_BRANCH * d),
                                    pl.BlockSpec(w_branch.shape, lambda i: (0, 0, 0),
                                                 pipeline_mode=pl.Buffered(1)),
                                    pl.BlockSpec(w_out.shape, lambda i: (0, 0), pipeline_mode=pl.Buffered(1)),
                                    pl.BlockSpec(hexp.shape, lambda i: (0, 0), pipeline_mode=pl.Buffered(1))],
        out_specs=row(d),
        out_shape=jax.ShapeDtypeStruct((t, d), F32),
        scratch_shapes=[pltpu.VMEM((bw // LANES, tm, LANES), F32) for _ in dils]
                       + [pltpu.VMEM((1, tm, LANES), F32) for _ in dils],
        compiler_params=_params(("parallel",), 48),
        name="merge",
    )(x2d, *o_parts, *lse_parts, yb, yc, yd, gates, w_branch, w_out, hexp)


def _xattn_ffn_kernel(x_ref, gx_ref, wq_ref, k_ref, v_ref, wxo_ref, gf_ref, wa_ref, wb_ref, wo_ref,
                      fg_ref, out_ref, *, final_norm):
    x = x_ref[0]
    q = _dot(_rms(x, gx_ref[...]).astype(BF16), wq_ref[...]).astype(BF16)
    k, v = k_ref[0], v_ref[0]
    outs = []
    for h in range(X_HEADS):
        sl = slice(h * X_HEAD_DIM, (h + 1) * X_HEAD_DIM)
        s = _dot_nt(q[:, sl], k[:, sl]) * (X_HEAD_DIM ** -0.5)
        p = jnp.exp(s - jnp.max(s, axis=-1, keepdims=True))
        a = p / jnp.sum(p, axis=-1, keepdims=True)
        outs.append(_dot(a.astype(BF16), v[:, sl]))
    x = x + _dot(jnp.concatenate(outs, axis=-1).astype(BF16), wxo_ref[...])
    h = _rms(x, gf_ref[...]).astype(BF16)
    a = _dot(h, wa_ref[...])
    b = _dot(h, wb_ref[...])
    y = x + _dot((_silu(a) * b).astype(BF16), wo_ref[...])
    if final_norm:
        y = _rms(y, fg_ref[...])
    out_ref[0] = y


def _xattn_ffn(x, gx, w_q, k, v, w_xo, gf, w_a, w_b, w_o, final_g, final_norm, tm=512):
    b, s, d = x.shape
    m, w = k.shape[1], k.shape[2]
    full = lambda wt: pl.BlockSpec(wt.shape, lambda bi, i: (0, 0), pipeline_mode=pl.Buffered(1))
    vec = pl.BlockSpec((1, d), lambda bi, i: (0, 0))
    mem = pl.BlockSpec((1, m, w), lambda bi, i: (bi, 0, 0))
    return pl.pallas_call(
        functools.partial(_xattn_ffn_kernel, final_norm=final_norm),
        grid=(b, s // tm),
        in_specs=[pl.BlockSpec((1, tm, d), lambda bi, i: (bi, i, 0)), vec, full(w_q), mem, mem,
                  full(w_xo), vec, full(w_a), full(w_b), full(w_o), vec],
        out_specs=pl.BlockSpec((1, tm, d), lambda bi, i: (bi, i, 0)),
        out_shape=jax.ShapeDtypeStruct((b, s, d), F32),
        compiler_params=_params(("parallel", "parallel"), 56),
        name="xattn_ffn",
    )(x, gx.reshape(1, d), w_q, k, v, w_xo, gf.reshape(1, d), w_a, w_b, w_o, final_g.reshape(1, d))


def _ret_feature_order():
    quarter = RET_KEY_DIM // 2
    first = [h * RET_KEY_DIM + f for h in range(RET_HEADS) for f in range(quarter)]
    return np.array(first + [c + quarter for c in first])


def _layer(x, mem, p, lam_init, final_g, final_norm):
    b, s, d = x.shape
    t = b * s
    x2d = x.reshape(t, d)
    a_w = DIL_HEADS * DIL_HEAD_DIM
    b_qk, b_vg = RET_HEADS * RET_KEY_DIM, RET_HEADS * RET_VAL_DIM
    c_w = DIFF_HEADS * 2 * DIFF_HEAD_DIM
    d_xbc = SSM_D_INNER + 2 * SSM_GROUPS * SSM_STATE
    sizes = (a_w, a_w, a_w, b_qk, b_qk, b_vg, b_vg, c_w, c_w, c_w, SSM_D_INNER, d_xbc, SSM_HEADS,
             N_BRANCH * d)
    offs = np.concatenate([[0], np.cumsum(sizes)])
    w_in = p['w_in']
    col = lambda i: w_in[:, offs[i]:offs[i + 1]]
    order = _ret_feature_order()
    w_dt = jnp.pad(col(12), ((0, 0), (0, LANES - SSM_HEADS)))
    bf = lambda w: w.astype(BF16)

    dils = [dil for _, dil in DIL_PATTERNS]
    nd = len(dils)
    proj = _proj_views(x2d, p['norm_mix_g'], [bf(col(i)) for i in (0, 1, 2)],
                       [DIL_HEAD_DIM ** -0.5, 1.0, 1.0], [bf(col(i)) for i in (7, 8, 9)],
                       [DIFF_HEAD_DIM ** -0.5, 1.0, 1.0], dils)
    aq, ak, av = proj[0:nd], proj[nd:2 * nd], proj[2 * nd:3 * nd]
    cq, ck, cv = proj[3 * nd:]
    bq, bk, bv, bg, dz, dxbc, ddt = _norm_matmul(
        x2d, p['norm_mix_g'],
        [bf(col(3)[:, order]), bf(col(4)[:, order]), bf(col(5)), bf(col(6)), bf(col(10)), bf(col(11)),
         bf(w_dt)],
        [F32, F32, BF16, BF16, BF16, F32, F32], [None, None, None, _silu, _silu, None, None], tm=1024)
    (gates,) = _norm_matmul(x2d, p['norm_mix_g'], [bf(col(13))], [BF16], [_sigmoid], tm=1024)

    r3 = lambda y: y.reshape(b, s, y.shape[-1])
    o_parts, lse_parts = _band_attention(aq, ak, av, b, s)
    y_b, y_d = _recurrent_mixers(
        _retention_call(r3(bq), r3(bk), r3(bv), r3(bg), p['ret_gn_g']),
        _ssd_call(r3(dz), r3(dxbc), r3(ddt), p['ssm_conv_w'], p['ssm_conv_b'], p['ssm_dt_bias'],
                  p['ssm_A_log'], p['ssm_D'], p['ssm_norm_g']), s // BLK)
    y_c = _diff_attention(r3(cq), r3(ck), r3(cv), p['diff_lambda'], p['diff_subln_g'], lam_init)
    x2d = _merge(x2d, o_parts, lse_parts, dils, y_b.reshape(t, -1), y_c.reshape(t, -1),
                 y_d.reshape(t, -1), gates, bf(p['w_branch']), bf(p['w_mix_out']))

    xw = X_HEADS * X_HEAD_DIM
    m2d = mem.reshape(-1, d)
    mk, mv = _norm_matmul(m2d, p['norm_mem_g'], [bf(p['w_xkv'][:, :xw]), bf(p['w_xkv'][:, xw:])],
                          [BF16, BF16], [None, None], tm=256)
    mlen = mem.shape[1]
    hid = p['w_ffn_out'].shape[0]
    return _xattn_ffn(x2d.reshape(b, s, d), p['norm_x_g'], bf(p['w_xq']), mk.reshape(b, mlen, xw),
                      mv.reshape(b, mlen, xw), bf(p['w_xo']), p['norm_ffn_g'], bf(p['w_ffn_in'][:, :hid]),
                      bf(p['w_ffn_in'][:, hid:]), bf(p['w_ffn_out']), final_g, final_norm)


def kernel(x, mem, norm_mix_g, w_in, ret_gn_g, diff_lambda, diff_subln_g, ssm_conv_w, ssm_conv_b,
           ssm_dt_bias, ssm_A_log, ssm_D, ssm_norm_g, w_branch, w_mix_out, norm_x_g, norm_mem_g, w_xq,
           w_xkv, w_xo, norm_ffn_g, w_ffn_in, w_ffn_out, norm_f_g):
    stacked = dict(norm_mix_g=norm_mix_g, w_in=w_in, ret_gn_g=ret_gn_g, diff_lambda=diff_lambda,
                   diff_subln_g=diff_subln_g, ssm_conv_w=ssm_conv_w, ssm_conv_b=ssm_conv_b,
                   ssm_dt_bias=ssm_dt_bias, ssm_A_log=ssm_A_log, ssm_D=ssm_D, ssm_norm_g=ssm_norm_g,
                   w_branch=w_branch, w_mix_out=w_mix_out, norm_x_g=norm_x_g, norm_mem_g=norm_mem_g,
                   w_xq=w_xq, w_xkv=w_xkv, w_xo=w_xo, norm_ffn_g=norm_ffn_g, w_ffn_in=w_ffn_in,
                   w_ffn_out=w_ffn_out)
    depth = w_in.shape[0]
    for l in range(depth):
        lam_init = 0.8 - 0.6 * math.exp(-0.3 * l)
        p = {name: val[l] for name, val in stacked.items()}
        x = _layer(x, mem, p, lam_init, norm_f_g, final_norm=(l == depth - 1))
    return x
```

```python
import functools
import math

import jax
import jax.numpy as jnp
import numpy as np
from jax import lax
from jax.experimental import pallas as pl
from jax.experimental.pallas import tpu as pltpu

F32 = jnp.float32
BF16 = jnp.bfloat16

EPS = 1e-6
NEG = -1e30
LOG2E = math.log2(math.e)
BLK = 128

DIL_HEADS = 8
DIL_HEAD_DIM = 64
DIL_PATTERNS = ((128, 1), (512, 4), (2048, 16))
RET_HEADS = 4
RET_KEY_DIM = 64
RET_VAL_DIM = 128
ROPE_BASE = 10000.0
DIFF_HEADS = 4
DIFF_HEAD_DIM = 64
SSM_D_INNER = 512
SSM_HEAD_DIM = 64
SSM_HEADS = 8
SSM_GROUPS = 2
SSM_STATE = 128
SSM_CONV = 4
N_BRANCH = 4
BRANCH_WIDTH = 512
X_HEADS = 4
X_HEAD_DIM = 128
LANES = 128
STAGE_RATIO = 4

MIB = 1024 * 1024


def _params(semantics, vmem_mib):
    return pltpu.CompilerParams(dimension_semantics=semantics,
                                vmem_limit_bytes=vmem_mib * MIB)


def _dot(a, b):
    return jnp.dot(a, b, preferred_element_type=F32)


def _dot_nt(a, b):
    return lax.dot_general(a, b, (((1,), (1,)), ((), ())), preferred_element_type=F32)


def _split3(x):
    x1 = x.astype(BF16)
    r1 = x - x1.astype(F32)
    x2 = r1.astype(BF16)
    x3 = (r1 - x2.astype(F32)).astype(BF16)
    return x1, x2, x3


def _dot_sel(x, sel):
    x1, x2, x3 = _split3(x)
    return _dot(x1, sel) + _dot(x2, sel) + _dot(x3, sel)


def _sel_dot(sel, x):
    x1, x2, x3 = _split3(x)
    return _dot(sel, x1) + _dot(sel, x2) + _dot(sel, x3)


def _sigmoid(x):
    return 0.5 * jnp.tanh(0.5 * x) + 0.5


def _silu(x):
    h = 0.5 * x
    return h * jnp.tanh(h) + h


def _rms(x, g):
    ms = jnp.mean(x * x, axis=-1, keepdims=True)
    return x * lax.rsqrt(ms + EPS) * g


def _norm_matmul_kernel(x_ref, g_ref, *refs, n_out, acts):
    w_refs, o_refs = refs[:n_out], refs[n_out:]
    h = _rms(x_ref[...], g_ref[...]).astype(BF16)
    for w_ref, o_ref, act in zip(w_refs, o_refs, acts):
        y = _dot(h, w_ref[...])
        if act is not None:
            y = act(y)
        o_ref[...] = y.astype(o_ref.dtype)


def _norm_matmul(x2d, g, ws, out_dtypes, acts, tm, vmem_mib=48):
    t, d = x2d.shape
    in_specs = [pl.BlockSpec((tm, d), lambda i: (i, 0)),
                pl.BlockSpec((1, d), lambda i: (0, 0))]
    in_specs += [pl.BlockSpec(w.shape, lambda i: (0, 0), pipeline_mode=pl.Buffered(1)) for w in ws]
    out_specs = [pl.BlockSpec((tm, w.shape[1]), lambda i: (i, 0)) for w in ws]
    out_shape = [jax.ShapeDtypeStruct((t, w.shape[1]), dt) for w, dt in zip(ws, out_dtypes)]
    return pl.pallas_call(
        functools.partial(_norm_matmul_kernel, n_out=len(ws), acts=tuple(acts)),
        grid=(t // tm,), in_specs=in_specs, out_specs=out_specs, out_shape=out_shape,
        compiler_params=_params(("parallel",), vmem_mib),
        name="norm_matmul",
    )(x2d, g.reshape(1, d), *ws)


def _proj_views_kernel(x_ref, g_ref, *refs, n_view, n_plain, dils, view_scales, plain_scales):
    nd = len(dils)
    n_w = n_view + n_plain
    w_refs = refs[:n_w]
    view_refs = refs[n_w:n_w + n_view * nd]
    plain_refs = refs[n_w + n_view * nd:n_w + n_view * nd + n_plain]
    slab_ref, staged_ref = refs[-2:]
    tm = x_ref.shape[0]
    h = _rms(x_ref[...], g_ref[...]).astype(BF16)
    for i in range(n_view):
        y = _dot(h, w_refs[i][...])
        if view_scales[i] != 1.0:
            y = y * view_scales[i]
        width = y.shape[1]
        tiles = width // LANES
        for t in range(tiles):
            slab_ref[t] = y[:, t * LANES:(t + 1) * LANES]
        staged = None
        for di, dil in enumerate(dils):
            o_ref = view_refs[i * nd + di]
            if dil == 1:
                o_ref[...] = y.astype(o_ref.dtype)
                continue
            from_staged = staged is not None and dil == STAGE_RATIO * staged
            keep = (not from_staged) and (STAGE_RATIO * dil in dils)
            for r in range(dil):
                for t in range(tiles):
                    if from_staged:
                        src = staged_ref[(r % staged) * tiles + t, pl.ds(r // staged, tm // dil, stride=STAGE_RATIO), :]
                    else:
                        src = slab_ref[t, pl.ds(r, tm // dil, stride=dil), :]
                        if keep:
                            staged_ref[r * tiles + t, 0:tm // dil, :] = src
                    col = r * width + t * LANES
                    o_ref[:, col:col + LANES] = src.astype(o_ref.dtype)
            if keep:
                staged = dil
    for i in range(n_plain):
        y = _dot(h, w_refs[n_view + i][...])
        if plain_scales[i] != 1.0:
            y = y * plain_scales[i]
        plain_refs[i][...] = y.astype(plain_refs[i].dtype)


def _proj_views(x2d, g, view_ws, view_scales, plain_ws, plain_scales, dils, tm=1024):
    t, d = x2d.shape
    ws = list(view_ws) + list(plain_ws)
    full = lambda w: pl.BlockSpec(w.shape, lambda i: (0, 0), pipeline_mode=pl.Buffered(1))
    out_specs, out_shape = [], []
    for w in view_ws:
        n = w.shape[1]
        for dil in dils:
            out_specs.append(pl.BlockSpec((tm // dil, dil * n), lambda i: (i, 0)))
            out_shape.append(jax.ShapeDtypeStruct((t // dil, dil * n), BF16))
    for w in plain_ws:
        out_specs.append(pl.BlockSpec((tm, w.shape[1]), lambda i: (i, 0)))
        out_shape.append(jax.ShapeDtypeStruct((t, w.shape[1]), BF16))
    width = view_ws[0].shape[1]
    stage = next(dil for dil in dils if dil > 1 and STAGE_RATIO * dil in dils)
    return pl.pallas_call(
        functools.partial(_proj_views_kernel, n_view=len(view_ws), n_plain=len(plain_ws), dils=tuple(dils),
                          view_scales=tuple(view_scales), plain_scales=tuple(plain_scales)),
        grid=(t // tm,),
        in_specs=[pl.BlockSpec((tm, d), lambda i: (i, 0)), pl.BlockSpec((1, d), lambda i: (0, 0))]
                 + [full(w) for w in ws],
        out_specs=out_specs, out_shape=out_shape,
        scratch_shapes=[pltpu.VMEM((width // LANES, tm, LANES), F32),
                        pltpu.VMEM((stage * width // LANES, tm // stage, LANES), F32)],
        compiler_params=_params(("parallel",), 48),
        name="proj_views",
    )(x2d, g.reshape(1, d), *ws)


def _band_attn_kernel(*refs, steps_per_residue):
    n_pat = len(steps_per_residue)
    ins, outs = refs[:6 * n_pat], refs[6 * n_pat:]
    step = pl.program_id(1)
    for i in range(n_pat):
        bias_ref, q_ref, kp_ref, kc_ref, vp_ref, vc_ref = ins[6 * i:6 * i + 6]
        o_ref, lse_ref = outs[2 * i:2 * i + 2]
        first = jnp.minimum(step % steps_per_residue[i], 1)
        lo, hi = slice(0, BLK), slice(BLK, 2 * BLK)
        _band_block(bias_ref[first], q_ref.at[0, lo], kp_ref.at[0], kc_ref.at[0, lo], vp_ref.at[0],
                    vc_ref.at[0, lo], o_ref.at[0, lo], lse_ref.at[0, lo])
        _band_block(bias_ref[1], q_ref.at[0, hi], kc_ref.at[0, lo], kc_ref.at[0, hi], vc_ref.at[0, lo],
                    vc_ref.at[0, hi], o_ref.at[0, hi], lse_ref.at[0, hi])


def _band_block(bias, q_ref, kp_ref, kc_ref, vp_ref, vc_ref, o_ref, lse_ref):
    hd = DIL_HEAD_DIM
    lane = lax.broadcasted_iota(jnp.int32, (1, LANES), 1)
    ones = jnp.ones((2 * BLK, LANES), BF16)
    lse_tile = jnp.zeros((BLK, LANES), F32)
    for t in range(DIL_HEADS * hd // LANES):
        sl = slice(t * LANES, (t + 1) * LANES)
        q = q_ref[:, sl]
        k2 = jnp.concatenate([kp_ref[:, sl], kc_ref[:, sl]], axis=0)
        v2 = jnp.concatenate([vp_ref[:, sl], vc_ref[:, sl]], axis=0)
        v_ext = jnp.concatenate([v2, ones], axis=-1)
        pair = []
        for half in range(LANES // hd):
            in_head = (lane // hd) == half
            s = _dot_nt(jnp.where(in_head, q, jnp.zeros_like(q)), k2) + bias
            m = jnp.max(jnp.maximum(s[:, :BLK], s[:, BLK:]), axis=-1, keepdims=True)
            p = jnp.exp((s - m).astype(BF16))
            pv = _dot(p, v_ext)
            den = pv[:, LANES:]
            pair.append(pv[:, :LANES] / den)
            h = t * (LANES // hd) + half
            per_head = LANES // DIL_HEADS
            lse_tile = jnp.where((lane // per_head) == h, m + jnp.log(den), lse_tile)
        o_ref[:, sl] = jnp.where((lane // hd) == 0, pair[0], pair[1]).astype(o_ref.dtype)
    lse_ref[...] = lse_tile


def _band_bias(window):
    i = np.arange(BLK)[:, None]
    j = np.arange(2 * BLK)[None, :]
    dist = BLK + i - j
    band = (dist >= 0) & (dist <= window)
    first = band & (j >= BLK)
    return np.where(np.stack([first, band]), 0.0, NEG).astype(np.float32)


def _band_attention(qs, ks, vs, batch, seq):
    w = DIL_HEADS * DIL_HEAD_DIM
    steps = seq // (2 * BLK)
    args, in_specs, out_specs, out_shape, per_residue = [], [], [], [], []
    for (window, dil), q, k, v in zip(DIL_PATTERNS, qs, ks, vs):
        l = seq // dil
        ns = l // (2 * BLK)
        per_residue.append(ns)
        view = lambda t, l=l, dil=dil: t.reshape(batch, l, dil * w)
        cur = lambda wd, ns=ns: pl.BlockSpec((1, 2 * BLK, wd), lambda bi, n: (bi, n % ns, n // ns))
        prev = pl.BlockSpec((1, BLK, w),
                            lambda bi, n, ns=ns: (bi, jnp.maximum(2 * (n % ns) - 1, 0), n // ns))
        bias = pl.BlockSpec((2, BLK, 2 * BLK), lambda bi, n: (0, 0, 0))
        args += [_band_bias(window // dil), view(q), view(k), view(k), view(v), view(v)]
        in_specs += [bias, cur(w), prev, cur(w), prev, cur(w)]
        out_specs += [cur(w), cur(LANES)]
        out_shape += [jax.ShapeDtypeStruct((batch, l, dil * w), BF16),
                      jax.ShapeDtypeStruct((batch, l, dil * LANES), F32)]
    outs = pl.pallas_call(
        functools.partial(_band_attn_kernel, steps_per_residue=tuple(per_residue)),
        grid=(batch, steps),
        in_specs=in_specs, out_specs=out_specs, out_shape=out_shape,
        compiler_params=_params(("parallel", "parallel"), 32),
        name="band_attention",
    )(*args)
    flat = [o.reshape(-1, o.shape[-1]) for o in outs]
    return flat[0::2], flat[1::2]


def _retention_kernel(q_ref, k_ref, v_ref, g_ref, cos_ref, sin_ref, dq_ref, dk_ref, dec_ref,
                      cdec_ref, gn_ref, o_ref, state_ref):
    @pl.when(pl.program_id(0) == 0)
    def _():
        state_ref[...] = jnp.zeros_like(state_ref)

    cos, sin = cos_ref[...], sin_ref[...]
    half = RET_HEADS * RET_KEY_DIM // 2

    def rot(t):
        t1, t2 = t[:, :half], t[:, half:]
        return jnp.concatenate([t1 * cos - t2 * sin, t1 * sin + t2 * cos], axis=-1)

    lane_head = (lax.broadcasted_iota(jnp.int32, (1, 2 * half), 1) % half) // (RET_KEY_DIM // 2)
    for bi in range(q_ref.shape[0]):
        rq = rot(q_ref[bi])
        rk = rot(k_ref[bi]) * (RET_KEY_DIM ** -0.5)
        v = v_ref[bi]
        rkb = rk.astype(BF16)
        state = state_ref[bi]
        o_cross = _dot((rq * dq_ref[...]).astype(BF16), state.astype(BF16))
        outs = []
        for h in range(RET_HEADS):
            qh = jnp.where(lane_head == h, rq, 0.0).astype(BF16)
            s = _dot_nt(qh, rkb) * dec_ref[h]
            outs.append(_dot(s.astype(BF16), v[:, h * RET_VAL_DIM:(h + 1) * RET_VAL_DIM]))
        o = jnp.concatenate(outs, axis=-1) + o_cross
        k_end_t = (rk * dk_ref[...]).T.astype(BF16)
        kv = _dot(k_end_t, v)
        row_head = (lax.broadcasted_iota(jnp.int32, kv.shape, 0) % half) // (RET_KEY_DIM // 2)
        col_head = lax.broadcasted_iota(jnp.int32, kv.shape, 1) // RET_VAL_DIM
        state_ref[bi] = state * cdec_ref[...] + jnp.where(row_head == col_head, kv, 0.0)
        normed = []
        for h in range(RET_HEADS):
            oh = o[:, h * RET_VAL_DIM:(h + 1) * RET_VAL_DIM]
            mu = jnp.mean(oh, axis=-1, keepdims=True)
            cen = oh - mu
            var = jnp.mean(cen * cen, axis=-1, keepdims=True)
            normed.append(cen * lax.rsqrt(var + EPS))
        y = jnp.concatenate(normed, axis=-1) * gn_ref[...]
        o_ref[bi] = (g_ref[bi].astype(F32) * y).astype(o_ref.dtype)


def _retention_tables(s):
    h, dk, dv = RET_HEADS, RET_KEY_DIM, RET_VAL_DIM
    quarter = dk // 2
    pos = np.arange(s, dtype=np.float64)
    inv_freq = ROPE_BASE ** (-np.arange(quarter, dtype=np.float64) / quarter)
    ang = pos[:, None] * np.tile(inv_freq, h)[None, :]
    log_gamma = np.log1p(-np.exp2(-5.0 - np.arange(h, dtype=np.float64)))
    idx = np.arange(BLK, dtype=np.float64)
    rel = idx[:, None] - idx[None, :]
    decay = np.where(rel >= 0, np.exp(log_gamma[:, None, None] * np.maximum(rel, 0.0)), 0.0)
    lane_gamma = np.tile(np.repeat(log_gamma, quarter), 2)
    dq = np.exp((idx + 1.0)[:, None] * lane_gamma[None, :])
    dkt = np.exp((BLK - 1 - idx)[:, None] * lane_gamma[None, :])
    cdec = np.repeat(np.exp(log_gamma * BLK), dv)[None, :]
    return tuple(t.astype(np.float32) for t in (np.cos(ang), np.sin(ang), dq, dkt, decay, cdec))


def _retention_call(q, k, v, g, gn_g):
    b, s, wk = q.shape
    wv = v.shape[-1]
    cos, sin, dq, dkt, decay, cdec = _retention_tables(s)
    tok = lambda w: pl.BlockSpec((b, BLK, w), lambda c: (0, c, 0))
    const = lambda shape: pl.BlockSpec(shape, lambda c: (0,) * len(shape))
    in_specs = [tok(wk), tok(wk), tok(wv), tok(wv),
                pl.BlockSpec((BLK, wk // 2), lambda c: (c, 0)),
                pl.BlockSpec((BLK, wk // 2), lambda c: (c, 0)),
                const((BLK, wk)), const((BLK, wk)), const((RET_HEADS, BLK, BLK)),
                const((1, wv)), const((1, wv))]
    args = [q, k, v, g, cos, sin, dq, dkt, decay, cdec, gn_g.reshape(1, wv)]
    return (args, in_specs, tok(wv), jax.ShapeDtypeStruct((b, s, wv), BF16),
            [pltpu.VMEM((b, wk, wv), F32)])


def _recurrent_mixers_kernel(*refs, n_ret_in, n_ssd_in):
    ret_in, ssd_in = refs[:n_ret_in], refs[n_ret_in:n_ret_in + n_ssd_in]
    ret_out, ssd_out = refs[n_ret_in + n_ssd_in:n_ret_in + n_ssd_in + 2]
    scratch = refs[n_ret_in + n_ssd_in + 2:]
    _retention_kernel(*ret_in, ret_out, scratch[0])
    _ssd_kernel(*ssd_in, ssd_out, *scratch[1:])


def _recurrent_mixers(ret_call, ssd_call, steps):
    (ra, rs, ro, rshape, rscr), (sa, ss, so, sshape, sscr) = ret_call, ssd_call
    return pl.pallas_call(
        functools.partial(_recurrent_mixers_kernel, n_ret_in=len(ra), n_ssd_in=len(sa)),
        grid=(steps,),
        in_specs=rs + ss, out_specs=[ro, so], out_shape=[rshape, sshape],
        scratch_shapes=rscr + sscr,
        compiler_params=_params(("arbitrary",), 40),
        name="recurrent_mixers",
    )(*ra, *sa)


def _diff_attn_kernel(lam_ref, g_ref, q_ref, k_ref, v_ref, o_ref, s_ref, m_ref, vext_ref, acc_ref, *,
                      tq, lam_init):
    d = DIFF_HEAD_DIM
    hw = 2 * d
    nl = tq // LANES
    qi = pl.program_id(2)
    q = q_ref[0]
    qs = (q[:, :d], q[:, d:])

    @pl.when(qi == 0)
    def _():
        vext_ref[:, :hw] = v_ref[0]
        vext_ref[:, hw:] = jnp.ones((vext_ref.shape[0], hw), BF16)

    def fold_max(x):
        r = x[:, :LANES]
        for c in range(1, nl):
            r = jnp.maximum(r, x[:, c * LANES:(c + 1) * LANES])
        return r

    m_ref[...] = jnp.full(m_ref.shape, NEG, F32)

    def scores(j, diagonal):
        k = k_ref[0, pl.ds(pl.multiple_of(j * tq, tq), tq), :]
        for c in range(2):
            s = _dot_nt(qs[c], k[:, c * d:(c + 1) * d]) * LOG2E
            if diagonal:
                row = lax.broadcasted_iota(jnp.int32, (tq, tq), 0)
                col = lax.broadcasted_iota(jnp.int32, (tq, tq), 1)
                s = jnp.where(col <= row, s, NEG)
            s_ref[c, j] = s
            m_ref[c] = jnp.maximum(m_ref[c], fold_max(s))

    def for_each_block(n, body):
        def quad(jj, carry):
            for u in range(4):
                body(4 * jj + u)
            return carry

        lax.fori_loop(0, lax.shift_right_logical(n, 2), quad, 0)
        done = n & ~3

        @pl.when((n & 2) == 2)
        def _():
            body(done)
            body(done + 1)

        @pl.when((n & 1) == 1)
        def _():
            body(n - 1)

    for_each_block(qi, lambda j: scores(j, False))
    scores(qi, True)

    for c in range(2):
        m_ref[c] = jnp.broadcast_to(jnp.max(m_ref[c], axis=-1, keepdims=True), (tq, LANES))
    acc_ref[...] = jnp.zeros(acc_ref.shape, F32)

    def accumulate(j):
        v = vext_ref[pl.ds(pl.multiple_of(j * tq, tq), tq), :]
        for c in range(2):
            m = m_ref[c]
            s = s_ref[c, j]
            p = jnp.concatenate([jnp.exp2(s[:, t * LANES:(t + 1) * LANES] - m) for t in range(nl)], axis=-1)
            acc_ref[c] += _dot(p.astype(BF16), v)

    for_each_block(qi + 1, accumulate)

    lp = lam_ref[...]
    lam = (jnp.exp(jnp.sum(lp[0:1] * lp[1:2], keepdims=True))
           - jnp.exp(jnp.sum(lp[2:3] * lp[3:4], keepdims=True)) + lam_init)
    o1 = acc_ref[0, :, :hw] / acc_ref[0, :, hw:]
    o2 = acc_ref[1, :, :hw] / acc_ref[1, :, hw:]
    o_ref[0] = (_rms(o1 - lam * o2, g_ref[...]) * (1.0 - lam_init)).astype(o_ref.dtype)


def _diff_attention(q, k, v, lam_params, subln_g, lam_init, tq=512):
    b, s, w = q.shape
    hw = 2 * DIFF_HEAD_DIM
    return pl.pallas_call(
        functools.partial(_diff_attn_kernel, tq=tq, lam_init=lam_init),
        grid=(b, DIFF_HEADS, s // tq),
        in_specs=[pl.BlockSpec(lam_params.shape, lambda bi, h, i: (0, 0)),
                  pl.BlockSpec((1, hw), lambda bi, h, i: (0, 0)),
                  pl.BlockSpec((1, tq, hw), lambda bi, h, i: (bi, i, h)),
                  pl.BlockSpec((1, s, hw), lambda bi, h, i: (bi, 0, h)),
                  pl.BlockSpec((1, s, hw), lambda bi, h, i: (bi, 0, h))],
        out_specs=pl.BlockSpec((1, tq, hw), lambda bi, h, i: (bi, i, h)),
        out_shape=jax.ShapeDtypeStruct((b, s, w), BF16),
        scratch_shapes=[pltpu.VMEM((2, s // tq, tq, tq), F32), pltpu.VMEM((2, tq, LANES), F32),
                        pltpu.VMEM((s, 2 * hw), BF16), pltpu.VMEM((2, tq, 2 * hw), F32)],
        compiler_params=_params(("parallel", "parallel", "arbitrary"), 40),
        name="diff_attention",
    )(lam_params, subln_g.reshape(1, hw), q, k, v)


def _ssd_kernel(z_ref, xbc_ref, dt_ref, cw_ref, cb_ref, dtb_ref, alog_ref, dskip_ref, ng_ref,
                tri_ref, exp_ref, o_ref, carry_ref, ext_ref, state_ref):
    gn = SSM_GROUPS * SSM_STATE
    hpg = SSM_HEADS // SSM_GROUPS
    gw = hpg * SSM_HEAD_DIM
    tail = 8

    @pl.when(pl.program_id(0) == 0)
    def _():
        carry_ref[...] = jnp.zeros_like(carry_ref)
        state_ref[...] = jnp.zeros_like(state_ref)

    cw = cw_ref[...]
    lane = lax.broadcasted_iota(jnp.int32, (1, LANES), 1)
    a_neg = jnp.where(lane < SSM_HEADS, -jnp.exp(alog_ref[...]), 0.0)
    expand = exp_ref[...]
    row = lax.broadcasted_iota(jnp.int32, (BLK, BLK), 0)
    col = lax.broadcasted_iota(jnp.int32, (BLK, BLK), 1)
    col_head = lax.broadcasted_iota(jnp.int32, (1, gw), 1) // SSM_HEAD_DIM
    for bi in range(z_ref.shape[0]):
        xbc = xbc_ref[bi]
        ext_ref[bi, 0:tail, :] = carry_ref[bi]
        ext_ref[bi, tail:tail + BLK, :] = xbc
        carry_ref[bi] = xbc[BLK - tail:, :]
        conv = cb_ref[...] + cw[SSM_CONV - 1:SSM_CONV, :] * xbc
        for sft in range(1, SSM_CONV):
            conv = conv + (cw[SSM_CONV - 1 - sft:SSM_CONV - sft, :]
                           * ext_ref[bi, tail - sft:tail - sft + BLK, :])
        xc = _silu(conv)
        xs, bm, cm = xc[:, :SSM_D_INNER], xc[:, SSM_D_INNER:SSM_D_INNER + gn], xc[:, SSM_D_INNER + gn:]

        pre = dt_ref[bi] + dtb_ref[...]
        dt = jnp.maximum(pre, 0.0) + jnp.log1p(jnp.exp(-jnp.abs(pre)))
        a_col = _sel_dot(tri_ref[...], dt * a_neg)
        a_row = a_col.T
        a_exp = _dot_sel(a_col, expand)
        dt_exp = _dot_sel(dt, expand)
        total = a_exp[BLK - 1:BLK, :]
        xdt = xs * dt_exp
        x_end = (xdt * jnp.exp(total - a_exp)).astype(BF16)
        xdt_b = xdt.astype(BF16)
        ys = []
        for g in range(SSM_GROUPS):
            b_g = bm[:, g * SSM_STATE:(g + 1) * SSM_STATE]
            c_g = cm[:, g * SSM_STATE:(g + 1) * SSM_STATE].astype(BF16)
            cb = _dot_nt(c_g, b_g.astype(BF16))
            xg = xdt_b[:, g * gw:(g + 1) * gw]
            y_g = jnp.zeros((BLK, gw), F32)
            for hh in range(hpg):
                h = g * hpg + hh
                seg = a_col[:, h:h + 1] - a_row[h:h + 1, :]
                m_h = (cb * jnp.exp(jnp.where(col <= row, seg, NEG))).astype(BF16)
                y_g = y_g + _dot(m_h, jnp.where(col_head == hh, xg, jnp.zeros_like(xg)))
            state = state_ref[bi, g]
            y_g = y_g + _dot(c_g, state.astype(BF16)) * jnp.exp(a_exp[:, g * gw:(g + 1) * gw])
            new = _dot(b_g.T.astype(BF16), x_end[:, g * gw:(g + 1) * gw])
            state_ref[bi, g] = state * jnp.exp(total[:, g * gw:(g + 1) * gw]) + new
            ys.append(y_g)
        y = jnp.concatenate(ys, axis=-1) + xs * dskip_ref[...]
        y = y * z_ref[bi].astype(F32)
        o_ref[bi] = _rms(y, ng_ref[...]).astype(o_ref.dtype)


def _ssd_call(z, xbc, dt, conv_w, conv_b, dt_bias, a_log, d_skip, norm_g):
    b, s, c = xbc.shape
    di = SSM_D_INNER
    pad = lambda t: jnp.pad(t.astype(F32), (0, LANES - t.shape[0])).reshape(1, LANES)
    tri = jnp.asarray(np.tril(np.ones((BLK, BLK), np.float32)), BF16)
    expand = jnp.asarray(np.arange(LANES)[:, None] == (np.arange(di) // SSM_HEAD_DIM)[None, :], BF16)
    tok = lambda w: pl.BlockSpec((b, BLK, w), lambda ci: (0, ci, 0))
    const = lambda shape: pl.BlockSpec(shape, lambda ci: (0,) * len(shape))
    in_specs = [tok(di), tok(c), tok(LANES), const((SSM_CONV, c)), const((1, c)),
                const((1, LANES)), const((1, LANES)), const((1, di)), const((1, di)),
                const((BLK, BLK)), const((LANES, di))]
    args = [z, xbc, dt, conv_w, conv_b.reshape(1, c), pad(dt_bias), pad(a_log),
            jnp.repeat(d_skip.astype(F32), SSM_HEAD_DIM).reshape(1, di), norm_g.reshape(1, di), tri, expand]
    scratch = [pltpu.VMEM((b, 8, c), F32), pltpu.VMEM((b, 8 + BLK, c), F32),
               pltpu.VMEM((b, SSM_GROUPS, SSM_STATE, di // SSM_GROUPS), F32)]
    return args, in_specs, tok(di), jax.ShapeDtypeStruct((b, s, di), BF16), scratch


def _from_residue_view(src_ref, slab_ref, dil, width):
    if dil == 1:
        return src_ref[...].astype(F32)
    rows = src_ref.shape[0]
    tiles = width // LANES
    for r in range(dil):
        for t in range(tiles):
            col = r * width + t * LANES
            slab_ref[t, pl.ds(r, rows, stride=dil), :] = src_ref[:, col:col + LANES].astype(F32)
    return jnp.concatenate([slab_ref[t] for t in range(tiles)], axis=-1)


def _merge_kernel(x_ref, *refs, dils):
    n = len(dils)
    o_refs, l_refs = refs[:n], refs[n:2 * n]
    yb_ref, yc_ref, yd_ref, gate_ref, wb_ref, wo_ref, hexp_ref, out_ref = refs[2 * n:2 * n + 8]
    o_slabs, l_slabs = refs[2 * n + 8:3 * n + 8], refs[3 * n + 8:]
    d = x_ref.shape[-1]
    bw = yb_ref.shape[-1]
    lses = [_from_residue_view(l_refs[i], l_slabs[i], dils[i], LANES) for i in range(n)]
    m = functools.reduce(jnp.maximum, lses)
    es = [jnp.exp(l - m) for l in lses]
    inv = 1.0 / functools.reduce(jnp.add, es)
    ya = jnp.zeros((x_ref.shape[0], bw), F32)
    for i in range(n):
        wt = _dot((es[i] * inv).astype(BF16), hexp_ref[...])
        ya = ya + wt * _from_residue_view(o_refs[i], o_slabs[i], dils[i], bw)
    branches = (ya.astype(BF16), yb_ref[...], yc_ref[...], yd_ref[...])
    half = d // 2
    out_ref[...] = x_ref[...]

    def half_step(c, carry):
        merged = jnp.zeros((x_ref.shape[0], half), F32)
        for i, y in enumerate(branches):
            lo = pl.multiple_of(i * d + c * half, half)
            gate = gate_ref[:, pl.ds(lo, half)].astype(F32)
            merged = merged + gate * _dot(y, wb_ref[i, :, pl.ds(pl.multiple_of(c * half, half), half)])
        out_ref[...] += _dot(merged.astype(BF16), wo_ref[pl.ds(pl.multiple_of(c * half, half), half), :])
        return carry

    lax.fori_loop(0, 2, half_step, 0)


def _merge(x2d, o_parts, lse_parts, dils, yb, yc, yd, gates, w_branch, w_out, tm=512):
    t, d = x2d.shape
    bw = yb.shape[-1]
    row = lambda w: pl.BlockSpec((tm, w), lambda i: (i, 0))
    view = lambda w, dil: pl.BlockSpec((tm // dil, dil * w), lambda i: (i, 0))
    per_head = LANES // DIL_HEADS
    hexp = jnp.asarray(np.arange(LANES)[:, None] == (np.arange(bw) // DIL_HEAD_DIM * per_head)[None, :],
                       BF16)
    return pl.pallas_call(
        functools.partial(_merge_kernel, dils=tuple(dils)),
        grid=(t // tm,),
        in_specs=[row(d)] + [view(bw, dil) for dil in dils] + [view(LANES, dil) for dil in dils]
                 + [row(bw)] * 3 + [row(N_BRANCH * d),
                                    pl.BlockSpec(w_branch.shape, lambda i: (0, 0, 0),
                                                 pipeline_mode=pl.Buffered(1)),
                                    pl.BlockSpec(w_out.shape, lambda i: (0, 0), pipeline_mode=pl.Buffered(1)),
                                    pl.BlockSpec(hexp.shape, lambda i: (0, 0), pipeline_mode=pl.Buffered(1))],
        out_specs=row(d),
        out_shape=jax.ShapeDtypeStruct((t, d), F32),
        scratch_shapes=[pltpu.VMEM((bw // LANES, tm, LANES), F32) for _ in dils]
                       + [pltpu.VMEM((1, tm, LANES), F32) for _ in dils],
        compiler_params=_params(("parallel",), 48),
        name="merge",
    )(x2d, *o_parts, *lse_parts, yb, yc, yd, gates, w_branch, w_out, hexp)


def _xattn_ffn_kernel(x_ref, gx_ref, wq_ref, k_ref, v_ref, wxo_ref, gf_ref, wa_ref, wb_ref, wo_ref,
                      fg_ref, out_ref, *, final_norm):
    x = x_ref[0]
    q = _dot(_rms(x, gx_ref[...]).astype(BF16), wq_ref[...]).astype(BF16)
    k, v = k_ref[0], v_ref[0]
    outs = []
    for h in range(X_HEADS):
        sl = slice(h * X_HEAD_DIM, (h + 1) * X_HEAD_DIM)
        s = _dot_nt(q[:, sl], k[:, sl]) * (X_HEAD_DIM ** -0.5)
        p = jnp.exp(s - jnp.max(s, axis=-1, keepdims=True))
        a = p / jnp.sum(p, axis=-1, keepdims=True)
        outs.append(_dot(a.astype(BF16), v[:, sl]))
    x = x + _dot(jnp.concatenate(outs, axis=-1).astype(BF16), wxo_ref[...])
    h = _rms(x, gf_ref[...]).astype(BF16)
    a = _dot(h, wa_ref[...])
    b = _dot(h, wb_ref[...])
    y = x + _dot((_silu(a) * b).astype(BF16), wo_ref[...])
    if final_norm:
        y = _rms(y, fg_ref[...])
    out_ref[0] = y


def _xattn_ffn(x, gx, w_q, k, v, w_xo, gf, w_a, w_b, w_o, final_g, final_norm, tm=512):
    b, s, d = x.shape
    m, w = k.shape[1], k.shape[2]
    full = lambda wt: pl.BlockSpec(wt.shape, lambda bi, i: (0, 0), pipeline_mode=pl.Buffered(1))
    vec = pl.BlockSpec((1, d), lambda bi, i: (0, 0))
    mem = pl.BlockSpec((1, m, w), lambda bi, i: (bi, 0, 0))
    return pl.pallas_call(
        functools.partial(_xattn_ffn_kernel, final_norm=final_norm),
        grid=(b, s // tm),
        in_specs=[pl.BlockSpec((1, tm, d), lambda bi, i: (bi, i, 0)), vec, full(w_q), mem, mem,
                  full(w_xo), vec, full(w_a), full(w_b), full(w_o), vec],
        out_specs=pl.BlockSpec((1, tm, d), lambda bi, i: (bi, i, 0)),
        out_shape=jax.ShapeDtypeStruct((b, s, d), F32),
        compiler_params=_params(("parallel", "parallel"), 56),
        name="xattn_ffn",
    )(x, gx.reshape(1, d), w_q, k, v, w_xo, gf.reshape(1, d), w_a, w_b, w_o, final_g.reshape(1, d))


def _ret_feature_order():
    quarter = RET_KEY_DIM // 2
    first = [h * RET_KEY_DIM + f for h in range(RET_HEADS) for f in range(quarter)]
    return np.array(first + [c + quarter for c in first])


def _layer(x, mem, p, lam_init, final_g, final_norm):
    b, s, d = x.shape
    t = b * s
    x2d = x.reshape(t, d)
    a_w = DIL_HEADS * DIL_HEAD_DIM
    b_qk, b_vg = RET_HEADS * RET_KEY_DIM, RET_HEADS * RET_VAL_DIM
    c_w = DIFF_HEADS * 2 * DIFF_HEAD_DIM
    d_xbc = SSM_D_INNER + 2 * SSM_GROUPS * SSM_STATE
    sizes = (a_w, a_w, a_w, b_qk, b_qk, b_vg, b_vg, c_w, c_w, c_w, SSM_D_INNER, d_xbc, SSM_HEADS,
             N_BRANCH * d)
    offs = np.concatenate([[0], np.cumsum(sizes)])
    w_in = p['w_in']
    col = lambda i: w_in[:, offs[i]:offs[i + 1]]
    order = _ret_feature_order()
    w_dt = jnp.pad(col(12), ((0, 0), (0, LANES - SSM_HEADS)))
    bf = lambda w: w.astype(BF16)

    dils = [dil for _, dil in DIL_PATTERNS]
    nd = len(dils)
    proj = _proj_views(x2d, p['norm_mix_g'], [bf(col(i)) for i in (0, 1, 2)],
                       [DIL_HEAD_DIM ** -0.5, 1.0, 1.0], [bf(col(i)) for i in (7, 8, 9)],
                       [DIFF_HEAD_DIM ** -0.5, 1.0, 1.0], dils)
    aq, ak, av = proj[0:nd], proj[nd:2 * nd], proj[2 * nd:3 * nd]
    cq, ck, cv = proj[3 * nd:]
    bq, bk, bv, bg, dz, dxbc, ddt = _norm_matmul(
        x2d, p['norm_mix_g'],
        [bf(col(3)[:, order]), bf(col(4)[:, order]), bf(col(5)), bf(col(6)), bf(col(10)), bf(col(11)),
         bf(w_dt)],
        [F32, F32, BF16, BF16, BF16, F32, F32], [None, None, None, _silu, _silu, None, None], tm=1024)
    (gates,) = _norm_matmul(x2d, p['norm_mix_g'], [bf(col(13))], [BF16], [_sigmoid], tm=1024)

    r3 = lambda y: y.reshape(b, s, y.shape[-1])
    o_parts, lse_parts = _band_attention(aq, ak, av, b, s)
    y_b, y_d = _recurrent_mixers(
        _retention_call(r3(bq), r3(bk), r3(bv), r3(bg), p['ret_gn_g']),
        _ssd_call(r3(dz), r3(dxbc), r3(ddt), p['ssm_conv_w'], p['ssm_conv_b'], p['ssm_dt_bias'],
                  p['ssm_A_log'], p['ssm_D'], p['ssm_norm_g']), s // BLK)
    y_c = _diff_attention(r3(cq), r3(ck), r3(cv), p['diff_lambda'], p['diff_subln_g'], lam_init)
    x2d = _merge(x2d, o_parts, lse_parts, dils, y_b.reshape(t, -1), y_c.reshape(t, -1),
                 y_d.reshape(t, -1), gates, bf(p['w_branch']), bf(p['w_mix_out']))

    xw = X_HEADS * X_HEAD_DIM
    m2d = mem.reshape(-1, d)
    mk, mv = _norm_matmul(m2d, p['norm_mem_g'], [bf(p['w_xkv'][:, :xw]), bf(p['w_xkv'][:, xw:])],
                          [BF16, BF16], [None, None], tm=256)
    mlen = mem.shape[1]
    hid = p['w_ffn_out'].shape[0]
    return _xattn_ffn(x2d.reshape(b, s, d), p['norm_x_g'], bf(p['w_xq']), mk.reshape(b, mlen, xw),
                      mv.reshape(b, mlen, xw), bf(p['w_xo']), p['norm_ffn_g'], bf(p['w_ffn_in'][:, :hid]),
                      bf(p['w_ffn_in'][:, hid:]), bf(p['w_ffn_out']), final_g, final_norm)


def kernel(x, mem, norm_mix_g, w_in, ret_gn_g, diff_lambda, diff_subln_g, ssm_conv_w, ssm_conv_b,
           ssm_dt_bias, ssm_A_log, ssm_D, ssm_norm_g, w_branch, w_mix_out, norm_x_g, norm_mem_g, w_xq,
           w_xkv, w_xo, norm_ffn_g, w_ffn_in, w_ffn_out, norm_f_g):
    stacked = dict(norm_mix_g=norm_mix_g, w_in=w_in, ret_gn_g=ret_gn_g, diff_lambda=diff_lambda,
                   diff_subln_g=diff_subln_g, ssm_conv_w=ssm_conv_w, ssm_conv_b=ssm_conv_b,
                   ssm_dt_bias=ssm_dt_bias, ssm_A_log=ssm_A_log, ssm_D=ssm_D, ssm_norm_g=ssm_norm_g,
                   w_branch=w_branch, w_mix_out=w_mix_out, norm_x_g=norm_x_g, norm_mem_g=norm_mem_g,
                   w_xq=w_xq, w_xkv=w_xkv, w_xo=w_xo, norm_ffn_g=norm_ffn_g, w_ffn_in=w_ffn_in,
                   w_ffn_out=w_ffn_out)
    depth = w_in.shape[0]
    for l in range(depth):
        lam_init = 0.8 - 0.6 * math.exp(-0.3 * l)
        p = {name: val[l] for name, val in stacked.items()}
        x = _layer(x, mem, p, lam_init, norm_f_g, final_norm=(l == depth - 1))
    return x
```
